```python
import jax, jax.numpy as jnp
from jax import lax
import numpy as np

D_MODEL = 2048
BATCH = 1
SEQ = 8192
DEPTH = 4

GRID_W = 64
N_MEM = 256
EPS = 1e-6
RW_HEADS = 16
RW_HEAD = 64
RW_WIDTH = RW_HEADS * RW_HEAD
W_LORA = 64
A_LORA = 64
V_LORA = 32
G_LORA = 160
RW_GN_EPS = 64e-5
AT_HEADS = 16
AT_KV_HEADS = 4
AT_HEAD = 64
AT_Q = AT_HEADS * AT_HEAD
AT_KV = AT_KV_HEADS * AT_HEAD
Q_BLOCK = 128
ROPE_THETA = 10000.0
X_HEADS = 4
X_HEAD = 128
X_WIDTH = X_HEADS * X_HEAD
D_FF = -(-8 * D_MODEL // (3 * 256)) * 256
RW_COLS = 3 * RW_WIDTH + 2 * W_LORA + 2 * A_LORA + G_LORA
AT_COLS = AT_Q + 2 * AT_KV
GATE_COLS = 2 * D_MODEL
IN_COLS = RW_COLS + AT_COLS + GATE_COLS

kernel_name = "bidir_rwkv7_axial_gqa_hybrid"


def rmsnorm(x, g):
    xf = x.astype(jnp.float32)
    y = xf * lax.rsqrt(jnp.mean(xf * xf, axis=-1, keepdims=True) + EPS)
    return (y * g.astype(jnp.float32)).astype(x.dtype)


def centred_shift(z, mu_prev, mu_next):
    zp = jnp.pad(z, ((0, 0), (1, 0), (0, 0)))[:, :-1]
    zn = jnp.pad(z, ((0, 0), (0, 1), (0, 0)))[:, 1:]
    return z + mu_prev * (zp - z) + mu_next * (zn - z)


def rwkv7_step(state, inp):
    r_t, w_t, k_t, v_t, a_t, b_t = inp
    sa = jnp.einsum('dbhij,dbhj->dbhi', state, a_t)
    state = (state * w_t[..., None, :] + sa[..., :, None] * b_t[..., None, :]
             + v_t[..., :, None] * k_t[..., None, :])
    y = jnp.einsum('dbhij,dbhj->dbhi', state, r_t)
    return state, y


def rwkv7_branch(z, mu_prev, mu_next, w0, w_up, a0, a_up, g_up, k_k, k_a, r_k,
                 ln_w, ln_b, v_first, v_mix):
    B, S, _ = z.shape
    f32 = jnp.float32
    z = centred_shift(z, mu_prev, mu_next)
    o = 0
    r = z[..., o:o + RW_WIDTH]; o += RW_WIDTH
    k = z[..., o:o + RW_WIDTH]; o += RW_WIDTH
    v = z[..., o:o + RW_WIDTH]; o += RW_WIDTH
    wd = z[..., o:o + 2 * W_LORA].reshape(B, S, 2, W_LORA); o += 2 * W_LORA
    ad = z[..., o:o + 2 * A_LORA].reshape(B, S, 2, A_LORA); o += 2 * A_LORA
    gd = z[..., o:o + G_LORA]
    if v_mix is None:
        v_first = v
    else:
        v0, v_down, v_up = v_mix
        v = v + (v_first - v) * jax.nn.sigmoid(v0 + (v @ v_down) @ v_up)
    w_log = -jax.nn.softplus(-(w0 + jnp.einsum('bsdr,drc->bsdc', jnp.tanh(wd), w_up)).astype(f32)) - 0.5
    decay = jnp.exp(-jnp.exp(w_log))
    a = jax.nn.sigmoid((a0 + jnp.einsum('bsdr,drc->bsdc', ad, a_up)).astype(f32))
    g = jax.nn.sigmoid(gd) @ g_up
    kf = k.astype(f32)
    kk = (kf * k_k).reshape(B, S, RW_HEADS, RW_HEAD)
    kk = kk / jnp.maximum(jnp.linalg.norm(kk, axis=-1, keepdims=True), 1e-12)
    kd = kf[:, :, None, :] * (1.0 + (a - 1.0) * k_a)

    hv = lambda t: t.reshape(B, S, 2, RW_HEADS, RW_HEAD)
    two = lambda t: jnp.broadcast_to(t.reshape(B, S, 1, RW_HEADS, RW_HEAD), (B, S, 2, RW_HEADS, RW_HEAD))

    def time_major(t):
        fwd = t[:, :, 0]
        bwd = jnp.flip(t[:, :, 1], axis=1)
        return jnp.transpose(jnp.stack([fwd, bwd], 0), (2, 0, 1, 3, 4))

    rr = two(r.astype(f32))
    vv = two(v.astype(f32))
    kk2 = two(kk)
    kd_h = hv(kd)
    seqs = (time_major(rr), time_major(hv(decay)), time_major(kd_h), time_major(vv),
            time_major(-kk2), time_major(kk2 * hv(a)))
    state0 = jnp.zeros((2, B, RW_HEADS, RW_HEAD, RW_HEAD), f32)
    _, y = lax.scan(rwkv7_step, state0, seqs)
    y = y[:, 0] + jnp.flip(y[:, 1], axis=0)
    y = jnp.transpose(y, (1, 0, 2, 3))
    mu = jnp.mean(y, axis=-1, keepdims=True)
    var = jnp.mean(jnp.square(y - mu), axis=-1, keepdims=True)
    y = ((y - mu) * lax.rsqrt(var + RW_GN_EPS)).reshape(B, S, RW_WIDTH) * ln_w + ln_b
    bonus = jnp.sum(rr * kd_h * r_k, axis=-1, keepdims=True) * vv
    y = y + jnp.sum(bonus, axis=2).reshape(B, S, RW_WIDTH)
    y = y * g
    return y.astype(z.dtype), v_first


def axial_rope(S):
    rows = S // GRID_W
    row = jnp.repeat(jnp.arange(rows), GRID_W)
    col = jnp.tile(jnp.arange(GRID_W), rows)
    n_freq = AT_HEAD // 4
    inv = ROPE_THETA ** (-jnp.arange(n_freq, dtype=jnp.float32) / n_freq)
    ang = jnp.stack([row, col], -1).astype(jnp.float32)[..., None] * inv
    return jnp.cos(ang), jnp.sin(ang)


def apply_rope(t, cos, sin):
    B, S, H, _ = t.shape
    tf = t.astype(jnp.float32).reshape(B, S, H, 2, 2, AT_HEAD // 4)
    t1, t2 = tf[..., 0, :], tf[..., 1, :]
    c = cos[None, :, None]
    s = sin[None, :, None]
    out = jnp.stack([t1 * c - t2 * s, t2 * c + t1 * s], axis=-2)
    return out.reshape(B, S, H, AT_HEAD).astype(t.dtype)


def axial_gqa_branch(z, q_norm, k_norm):
    B, S, _ = z.shape
    q = z[..., :AT_Q].reshape(B, S, AT_HEADS, AT_HEAD)
    k = z[..., AT_Q:AT_Q + AT_KV].reshape(B, S, AT_KV_HEADS, AT_HEAD)
    v = z[..., AT_Q + AT_KV:].reshape(B, S, AT_KV_HEADS, AT_HEAD)
    cos, sin = axial_rope(S)
    q = apply_rope(rmsnorm(q, q_norm), cos, sin)
    k = apply_rope(rmsnorm(k, k_norm), cos, sin)
    G = AT_HEADS // AT_KV_HEADS
    nb = S // Q_BLOCK
    qb = q.reshape(B, nb, Q_BLOCK, AT_KV_HEADS, G, AT_HEAD).transpose(1, 0, 2, 3, 4, 5)
    scale = AT_HEAD ** -0.5

    def block(qi):
        s = jnp.einsum('bqkgd,bskd->bkgqs', qi, k, preferred_element_type=jnp.float32) * scale
        p = jax.nn.softmax(s, axis=-1).astype(v.dtype)
        return jnp.einsum('bkgqs,bskd->bqkgd', p, v)

    o = lax.map(block, qb)
    return o.transpose(1, 0, 2, 3, 4, 5).reshape(B, S, AT_Q)


def memory_cross_attention(h, mem_n, wq, wkv, wo):
    B, S, _ = h.shape
    M = mem_n.shape[1]
    q = (h @ wq).reshape(B, S, X_HEADS, X_HEAD)
    kv = (mem_n @ wkv).reshape(B, M, 2, X_HEADS, X_HEAD)
    s = jnp.einsum('bshd,bmhd->bhsm', q, kv[:, :, 0], preferred_element_type=jnp.float32) * (X_HEAD ** -0.5)
    p = jax.nn.softmax(s, axis=-1).astype(h.dtype)
    o = jnp.einsum('bhsm,bmhd->bshd', p, kv[:, :, 1]).reshape(B, S, X_WIDTH)
    return o @ wo


def swiglu(h, wg, wu, wd):
    return (jax.nn.silu(h @ wg) * (h @ wu)) @ wd


def setup_inputs(seed: int = 0) -> dict:
    key = jax.random.key(seed)
    keys = list(jax.random.split(key, 64))
    idx = [0]

    def nk():
        k = keys[idx[0]]
        idx[0] += 1
        return k

    f32 = jnp.float32
    nrm = lambda shape, scale: jax.random.normal(nk(), shape, f32) * scale
    gain = lambda shape: 1.0 + nrm(shape, 0.02)
    unif = lambda shape, lo, hi: jax.random.uniform(nk(), shape, f32, lo, hi)
    L, D = DEPTH, D_MODEL
    return {
        "x": nrm((BATCH, SEQ, D), 1.0),
        "mem": nrm((BATCH, N_MEM, D), 1.0),
        "n_mix_pre": gain((L, D)),
        "n_mix_post": gain((L, D)),
        "n_x_pre": gain((L, D)),
        "n_x_post": gain((L, D)),
        "n_ffn_pre": gain((L, D)),
        "n_ffn_post": gain((L, D)),
        "n_mem": gain((L, D)),
        "w_in": nrm((L, D, IN_COLS), D ** -0.5),
        "rw_mu_prev": unif((L, RW_COLS), 0.0, 0.5),
        "rw_mu_next": unif((L, RW_COLS), 0.0, 0.5),
        "rw_w0": unif((L, 2, RW_WIDTH), -6.5, -1.5),
        "rw_w_up": nrm((L, 2, W_LORA, RW_WIDTH), 0.1 * W_LORA ** -0.5),
        "rw_a0": nrm((L, 2, RW_WIDTH), 0.1),
        "rw_a_up": nrm((L, 2, A_LORA, RW_WIDTH), 0.1 * A_LORA ** -0.5),
        "rw_g_up": nrm((L, G_LORA, RW_WIDTH), G_LORA ** -0.5),
        "rw_v0": 1.0 + nrm((L - 1, RW_WIDTH), 0.1),
        "rw_v_down": nrm((L - 1, RW_WIDTH, V_LORA), RW_WIDTH ** -0.5),
        "rw_v_up": nrm((L - 1, V_LORA, RW_WIDTH), 0.1 * V_LORA ** -0.5),
        "rw_k_k": 0.85 + nrm((L, RW_WIDTH), 0.02),
        "rw_k_a": 1.0 + nrm((L, RW_WIDTH), 0.02),
        "rw_r_k": nrm((L, RW_HEADS, RW_HEAD), 0.1),
        "rw_ln_w": gain((L, RW_WIDTH)),
        "rw_ln_b": nrm((L, RW_WIDTH), 0.02),
        "w_rw_out": nrm((L, RW_WIDTH, D), RW_WIDTH ** -0.5),
        "at_q_norm": gain((L, AT_HEAD)),
        "at_k_norm": gain((L, AT_HEAD)),
        "w_at_out": nrm((L, AT_Q, D), AT_Q ** -0.5),
        "w_o": nrm((L, D, D), D ** -0.5),
        "x_wq": nrm((L, D, X_WIDTH), D ** -0.5),
        "x_wkv": nrm((L, D, 2 * X_WIDTH), D ** -0.5),
        "x_wo": nrm((L, X_WIDTH, D), X_WIDTH ** -0.5),
        "ffn_wg": nrm((L, D, D_FF), D ** -0.5),
        "ffn_wu": nrm((L, D, D_FF), D ** -0.5),
        "ffn_wd": nrm((L, D_FF, D), D_FF ** -0.5),
    }


def reference(x, mem, n_mix_pre, n_mix_post, n_x_pre, n_x_post, n_ffn_pre, n_ffn_post,
              n_mem, w_in, rw_mu_prev, rw_mu_next, rw_w0, rw_w_up, rw_a0, rw_a_up,
              rw_g_up, rw_v0, rw_v_down, rw_v_up, rw_k_k, rw_k_a, rw_r_k, rw_ln_w,
              rw_ln_b, w_rw_out, at_q_norm, at_k_norm, w_at_out, w_o, x_wq, x_wkv,
              x_wo, ffn_wg, ffn_wu, ffn_wd):
    v_first = None
    for l in range(DEPTH):
        h = rmsnorm(x, n_mix_pre[l])
        z = h @ w_in[l]
        z_rw = z[..., :RW_COLS]
        z_at = z[..., RW_COLS:RW_COLS + AT_COLS]
        z_g = z[..., RW_COLS + AT_COLS:]
        v_mix = None if l == 0 else (rw_v0[l - 1], rw_v_down[l - 1], rw_v_up[l - 1])
        y_rw, v_first = rwkv7_branch(z_rw, rw_mu_prev[l], rw_mu_next[l], rw_w0[l], rw_w_up[l],
                                     rw_a0[l], rw_a_up[l], rw_g_up[l], rw_k_k[l], rw_k_a[l],
                                     rw_r_k[l], rw_ln_w[l], rw_ln_b[l], v_first, v_mix)
        y_at = axial_gqa_branch(z_at, at_q_norm[l], at_k_norm[l])
        gate = jax.nn.sigmoid(z_g.astype(jnp.float32)).astype(x.dtype)
        merged = (gate[..., :D_MODEL] * (y_rw @ w_rw_out[l])
                  + gate[..., D_MODEL:] * (y_at @ w_at_out[l]))
        x = x + rmsnorm(merged @ w_o[l], n_mix_post[l])
        h = rmsnorm(x, n_x_pre[l])
        c = memory_cross_attention(h, rmsnorm(mem, n_mem[l]), x_wq[l], x_wkv[l], x_wo[l])
        x = x + rmsnorm(c, n_x_post[l])
        h = rmsnorm(x, n_ffn_pre[l])
        x = x + rmsnorm(swiglu(h, ffn_wg[l], ffn_wu[l], ffn_wd[l]), n_ffn_post[l])
    return x
```

```python
import functools

import jax
import jax.numpy as jnp
from jax import lax
from jax.experimental import pallas as pl
from jax.experimental.pallas import tpu as pltpu

F32 = jnp.float32
BF16 = jnp.bfloat16

D_MODEL = 2048
DEPTH = 4
GRID_W = 64
EPS = 1e-6
RW_HEAD = 64
RW_WIDTH = 1024
W_LORA = 64
A_LORA = 64
G_LORA = 160
RW_GN_EPS = 64e-5
AT_HEADS = 16
AT_KV_HEADS = 4
AT_HEAD = 64
AT_Q = AT_HEADS * AT_HEAD
AT_KV = AT_KV_HEADS * AT_HEAD
ROPE_THETA = 10000.0
X_HEADS = 4
X_HEAD = 128
X_WIDTH = X_HEADS * X_HEAD
D_FF = 5632

LANES = 128
SUBLANES = 8
CHUNK = 64
SUB = 16
G_PAD = 256

COL_G1 = 0
COL_G2 = 2048
COL_R = 4096
COL_K = 5120
COL_V = 6144
COL_Q = 7168
COL_LORA = 8192
COL_ATKV = 8704
IN_PACKED = 9216

VMEM_LIMIT = 56 * 1024 * 1024


def _cparams(sem):
    return pltpu.CompilerParams(dimension_semantics=sem, vmem_limit_bytes=VMEM_LIMIT)


def _dot(a, b):
    return jnp.dot(a.astype(BF16), b.astype(BF16), preferred_element_type=F32)


def _dot_nt(a, b):
    return lax.dot_general(a.astype(BF16), b.astype(BF16), (((1,), (1,)), ((), ())),
                           preferred_element_type=F32)


def _split2(x):
    hi = x.astype(BF16)
    lo = (x - hi.astype(F32)).astype(BF16)
    return hi, lo


def _split3(x):
    hi = x.astype(BF16)
    r1 = x - hi.astype(F32)
    mid = r1.astype(BF16)
    lo = (r1 - mid.astype(F32)).astype(BF16)
    return hi, mid, lo


def _sigmoid(x):
    return 1.0 / (1.0 + jnp.exp(-x))


def _softplus(x):
    return jnp.maximum(x, 0.0) + jnp.log(1.0 + jnp.exp(-jnp.abs(x)))


def _head_ones():
    r = lax.broadcasted_iota(jnp.int32, (LANES, LANES), 0) // RW_HEAD
    c = lax.broadcasted_iota(jnp.int32, (LANES, LANES), 1) // RW_HEAD
    return jnp.where(r == c, 1.0, 0.0).astype(BF16)


def _head_sum(x, ones):
    outs = []
    for s in range(x.shape[1] // LANES):
        hi, lo = _split2(x[:, s * LANES:(s + 1) * LANES])
        outs.append(jnp.dot(hi, ones, preferred_element_type=F32)
                    + jnp.dot(lo, ones, preferred_element_type=F32))
    return outs[0] if len(outs) == 1 else jnp.concatenate(outs, axis=1)


def _rms(x, gain):
    ms = jnp.mean(x * x, axis=-1, keepdims=True)
    return x * lax.rsqrt(ms + EPS) * gain


def _norm_proj_kernel(x_ref, g_ref, w_ref, o_ref, h_ref):
    @pl.when(pl.program_id(1) == 0)
    def _():
        h_ref[...] = _rms(x_ref[...], g_ref[...]).astype(BF16)

    o_ref[...] = jnp.dot(h_ref[...], w_ref[...], preferred_element_type=F32).astype(o_ref.dtype)


def _norm_proj(x, gain, w, *, tm, tn, out_dtype):
    m, d = x.shape
    n = w.shape[1]
    return pl.pallas_call(
        _norm_proj_kernel,
        out_shape=jax.ShapeDtypeStruct((m, n), out_dtype),
        grid=(m // tm, n // tn),
        in_specs=[pl.BlockSpec((tm, d), lambda i, j: (i, 0)),
                  pl.BlockSpec((1, d), lambda i, j: (0, 0)),
                  pl.BlockSpec((d, tn), lambda i, j: (0, j))],
        out_specs=pl.BlockSpec((tm, tn), lambda i, j: (i, j)),
        scratch_shapes=[pltpu.VMEM((tm, d), BF16)],
        compiler_params=_cparams(("parallel", "arbitrary")),
        name="norm_proj",
    )(x, gain, w)


def _shift(z, prev_row, next_row, mu_p, mu_n):
    rows = z.shape[0]
    ridx = lax.broadcasted_iota(jnp.int32, z.shape, 0)
    zp = jnp.where(ridx == 0, prev_row, pltpu.roll(z, 1, 0))
    zn = jnp.where(ridx == rows - 1, next_row, pltpu.roll(z, rows - 1, 0))
    return z + mu_p * (zp - z) + mu_n * (zn - z)


def _rwprep_kernel(has_vmix, *refs):
    (zr, zk, zv, zl, pr, pk, pv, pL, nr, nk, nv, nL, mup, mun, w0, wup, a0, aup, gup,
     kk_w, ka_w, rk_w) = refs[:22]
    pos = 22
    if has_vmix:
        vfirst, v0, vdown, vup = refs[pos:pos + 4]
        pos += 4
    r_o, v_o, kk_o, kd_o, lw_o, ba_o, g_o, bon_o = refs[pos:pos + 8]

    i = pl.program_id(0)
    last = pl.num_programs(0) - 1
    keep_p = jnp.where(i == 0, 0.0, 1.0)
    keep_n = jnp.where(i == last, 0.0, 1.0)

    def shifted(z_ref, p_ref, n_ref, lo, hi):
        p_row = p_ref[SUBLANES - 1:SUBLANES, :] * keep_p
        n_row = n_ref[0:1, :] * keep_n
        return _shift(z_ref[...], p_row, n_row, mup[:, lo:hi], mun[:, lo:hi])

    r = shifted(zr, pr, nr, 0, 1024)
    k = shifted(zk, pk, nk, 1024, 2048)
    v = shifted(zv, pv, nv, 2048, 3072)
    lora = shifted(zl, pL, nL, 3072, 3584)

    if has_vmix:
        mix = _sigmoid(v0[...] + _dot(_dot(v, vdown[...]), vup[...]))
        v = v + (vfirst[...] - v) * mix

    u = w0[...] + _dot(jnp.tanh(lora[:, 0:128]), wup[...])
    w_log = -_softplus(-u) - 0.5
    lw = -jnp.exp(w_log)
    a = _sigmoid(a0[...] + _dot(lora[:, 128:256], aup[...]))
    g = _dot(_sigmoid(lora[:, 256:512]), gup[...])

    ones = _head_ones()
    kk = k * kk_w[...]
    norm = jnp.sqrt(_head_sum(kk * kk, ones))
    kk = kk / jnp.maximum(norm, 1e-12)
    ka = ka_w[...]
    a_f = a[:, :RW_WIDTH]
    a_b = a[:, RW_WIDTH:]
    kd_f = k * (1.0 + (a_f - 1.0) * ka)
    kd_b = k * (1.0 + (a_b - 1.0) * ka)
    bonus = _head_sum(r * rk_w[...] * (kd_f + kd_b), ones) * v

    r_o[...] = r
    v_o[...] = v
    kk_o[...] = kk
    kd_o[0] = kd_f
    kd_o[1] = kd_b
    lw_o[0] = lw[:, :RW_WIDTH]
    lw_o[1] = lw[:, RW_WIDTH:]
    ba_o[0] = kk * a_f
    ba_o[1] = kk * a_b
    g_o[...] = g
    bon_o[...] = bonus


def _rwprep(z, mup, mun, w0, wup, a0, aup, gup, kk_w, ka_w, rk_w, vmix, *, tm):
    s = z.shape[0]
    nb8 = s // SUBLANES
    tb = tm // SUBLANES
    has_vmix = vmix is not None

    def main(width, cb):
        return pl.BlockSpec((tm, width), lambda i: (i, cb))

    def prev(width, cb):
        return pl.BlockSpec((SUBLANES, width), lambda i: (jnp.maximum(i * tb - 1, 0), cb))

    def nxt(width, cb):
        return pl.BlockSpec((SUBLANES, width), lambda i: (jnp.minimum((i + 1) * tb, nb8 - 1), cb))

    def full(arr):
        nd = arr.ndim
        return pl.BlockSpec(arr.shape, lambda i: (0,) * nd)

    cols = [(1024, COL_R // 1024), (1024, COL_K // 1024), (1024, COL_V // 1024), (512, COL_LORA // 512)]
    in_specs = ([main(w, c) for w, c in cols] + [prev(w, c) for w, c in cols]
                + [nxt(w, c) for w, c in cols])
    args = [z] * 12
    consts = [mup, mun, w0, wup, a0, aup, gup, kk_w, ka_w, rk_w]
    in_specs += [full(c) for c in consts]
    args += consts
    if has_vmix:
        vfirst, v0, vdown, vup = vmix
        in_specs += [pl.BlockSpec((tm, RW_WIDTH), lambda i: (i, 0)), full(v0), full(vdown), full(vup)]
        args += [vfirst, v0, vdown, vup]

    one = jax.ShapeDtypeStruct((s, RW_WIDTH), F32)
    two = jax.ShapeDtypeStruct((2, s, RW_WIDTH), F32)
    spec1 = pl.BlockSpec((tm, RW_WIDTH), lambda i: (i, 0))
    spec2 = pl.BlockSpec((2, tm, RW_WIDTH), lambda i: (0, i, 0))
    return pl.pallas_call(
        functools.partial(_rwprep_kernel, has_vmix),
        out_shape=[one, one, one, two, two, two, one, one],
        grid=(s // tm,),
        in_specs=in_specs,
        out_specs=[spec1, spec1, spec1, spec2, spec2, spec2, spec1, spec1],
        compiler_params=_cparams(("parallel",)),
        name="rwkv_prep",
    )(*args)


def _rwrec_kernel(r_ref, v_ref, kk_ref, kd_ref, lw_ref, ba_ref, y_ref, s_ref):
    d = pl.program_id(0)
    T = CHUNK

    @pl.when(pl.program_id(1) == 0)
    def _():
        s_ref[...] = jnp.zeros_like(s_ref)

    sgn = 1 - 2 * d
    row = lax.broadcasted_iota(jnp.int32, (T, T), 0)
    col = lax.broadcasted_iota(jnp.int32, (T, T), 1)
    tri = jnp.where((row - col) * sgn >= 0, 1.0, 0.0).astype(BF16)

    lw = lw_ref[...]
    hi, mid, lo = _split3(lw)
    cin = (jnp.dot(tri, hi, preferred_element_type=F32)
           + jnp.dot(tri, mid, preferred_element_type=F32)
           + jnp.dot(tri, lo, preferred_element_type=F32))
    ctot = jnp.sum(lw, axis=0, keepdims=True)
    e_in = jnp.exp(cin)
    e_ex = jnp.exp(cin - lw)
    e_neg = jnp.exp(-cin)
    e_end = jnp.exp(ctot - cin)
    e_tot = jnp.exp(ctot)

    kd = kd_ref[...]
    ba = ba_ref[...]
    r_t = r_ref[...] * e_in
    a_t = -kk_ref[...] * e_ex
    b_t = ba * e_neg
    k_t = kd * e_neg
    b_e = ba * e_end
    k_e = kd * e_end
    v = v_ref[...]

    R2 = lax.broadcasted_iota(jnp.int32, (2 * T, 2 * T), 0)
    C2 = lax.broadcasted_iota(jnp.int32, (2 * T, 2 * T), 1)
    same = (R2 // T) == (C2 // T)
    diff = ((R2 % T) - (C2 % T)) * sgn
    mask_a = same & (diff > 0)
    mask_y = same & (diff >= 0)
    mask_d = same & (((R2 % T) // SUB) == ((C2 % T) // SUB))
    eye = R2 == C2
    head0 = lax.broadcasted_iota(jnp.int32, (T, LANES), 1) < RW_HEAD

    def stack_heads(x):
        return jnp.concatenate([jnp.where(head0, x, 0.0), jnp.where(head0, 0.0, x)], axis=0)

    def stack_dup(x):
        return jnp.concatenate([x, x], axis=0)

    for sl in range(RW_WIDTH // LANES):
        cs = slice(sl * LANES, (sl + 1) * LANES)
        a2 = stack_heads(a_t[:, cs]).astype(BF16)
        r2 = stack_heads(r_t[:, cs]).astype(BF16)
        v2 = stack_heads(v[:, cs]).astype(BF16)
        b2 = stack_dup(b_t[:, cs]).astype(BF16)
        k2 = stack_dup(k_t[:, cs]).astype(BF16)
        be2 = stack_heads(b_e[:, cs]).astype(BF16)
        ke2 = stack_heads(k_e[:, cs]).astype(BF16)

        n_ab = jnp.where(mask_a, _dot_nt(a2, b2), 0.0)
        n_ak = jnp.where(mask_a, _dot_nt(a2, k2), 0.0)
        m_rb = jnp.where(mask_y, _dot_nt(r2, b2), 0.0)
        m_rk = jnp.where(mask_y, _dot_nt(r2, k2), 0.0)

        st = s_ref[sl]
        st_b = st.astype(BF16)
        x = _dot_nt(a2, st_b) + _dot(n_ak, v2)
        p = jnp.where(mask_d, n_ab, 0.0)
        q = jnp.where(eye, 1.0, p)
        for _ in range(3):
            p = _dot(p, p)
            q = q + _dot(q, p)
        m = _dot(q, jnp.where(mask_d, 0.0, n_ab))
        x = _dot(q, x)
        x = x + _dot(m, x)
        x = x + _dot(_dot(m, m), x)
        y2 = _dot_nt(r2, st_b) + _dot(jnp.concatenate([m_rb, m_rk], axis=1),
                                      jnp.concatenate([x.astype(BF16), v2], axis=0))
        y_ref[:, cs] = y2[:T] + y2[T:]
        upd = _dot(jnp.concatenate([x.T, v2.astype(F32).T], axis=1),
                   jnp.concatenate([be2, ke2], axis=0))
        s_ref[sl] = st * e_tot[:, cs] + upd


def _rwrec(r, v, kk, kd, lw, ba):
    s = r.shape[0]
    nc = s // CHUNK

    def cidx(d, c):
        return c + d * (nc - 1 - 2 * c)

    spec1 = pl.BlockSpec((CHUNK, RW_WIDTH), lambda d, c: (cidx(d, c), 0))
    spec2 = pl.BlockSpec((None, CHUNK, RW_WIDTH), lambda d, c: (d, cidx(d, c), 0))
    return pl.pallas_call(
        _rwrec_kernel,
        out_shape=jax.ShapeDtypeStruct((2, s, RW_WIDTH), F32),
        grid=(2, nc),
        in_specs=[spec1, spec1, spec1, spec2, spec2, spec2],
        out_specs=spec2,
        scratch_shapes=[pltpu.VMEM((RW_WIDTH // LANES, LANES, LANES), F32)],
        compiler_params=_cparams(("arbitrary", "arbitrary")),
        name="rwkv_recurrence",
    )(r, v, kk, kd, lw, ba)


def _atprep_kernel(zq, zkv, cos_ref, sin_ref, qn, kn, q_o, k_o, v_o):
    ones = _head_ones()
    cos = cos_ref[...]
    sin = sin_ref[...]
    lane = lax.broadcasted_iota(jnp.int32, cos.shape, 1)
    even = ((lane // 16) % 2) == 0

    def norm_rope(t, gain):
        ms = _head_sum(t * t, ones) * (1.0 / AT_HEAD)
        tn = t * lax.rsqrt(ms + EPS) * gain
        swapped = jnp.where(even, pltpu.roll(tn, LANES - 16, 1), pltpu.roll(tn, 16, 1))
        return tn * cos + swapped * sin

    for sl in range(AT_Q // LANES):
        out = norm_rope(zq[:, sl * LANES:(sl + 1) * LANES], qn[...]) * (AT_HEAD ** -0.5)
        q_o[2 * sl] = out[:, :AT_HEAD].astype(BF16)
        q_o[2 * sl + 1] = out[:, AT_HEAD:].astype(BF16)
    for sl in range(AT_KV // LANES):
        out = norm_rope(zkv[:, sl * LANES:(sl + 1) * LANES], kn[...])
        k_o[2 * sl] = out[:, :AT_HEAD].astype(BF16)
        k_o[2 * sl + 1] = out[:, AT_HEAD:].astype(BF16)
        vv = zkv[:, AT_KV + sl * LANES:AT_KV + (sl + 1) * LANES]
        v_o[2 * sl] = vv[:, :AT_HEAD].astype(BF16)
        v_o[2 * sl + 1] = vv[:, AT_HEAD:].astype(BF16)


def _atprep(z, cos, sin, qn, kn, *, tm):
    s = z.shape[0]
    return pl.pallas_call(
        _atprep_kernel,
        out_shape=[jax.ShapeDtypeStruct((AT_HEADS, s, AT_HEAD), BF16),
                   jax.ShapeDtypeStruct((AT_KV_HEADS, s, AT_HEAD), BF16),
                   jax.ShapeDtypeStruct((AT_KV_HEADS, s, AT_HEAD), BF16)],
        grid=(s // tm,),
        in_specs=[pl.BlockSpec((tm, AT_Q), lambda i: (i, COL_Q // AT_Q)),
                  pl.BlockSpec((tm, 2 * AT_KV), lambda i: (i, COL_ATKV // (2 * AT_KV))),
                  pl.BlockSpec((tm, LANES), lambda i: (i, 0)),
                  pl.BlockSpec((tm, LANES), lambda i: (i, 0)),
                  pl.BlockSpec((1, LANES), lambda i: (0, 0)),
                  pl.BlockSpec((1, LANES), lambda i: (0, 0))],
        out_specs=[pl.BlockSpec((AT_HEADS, tm, AT_HEAD), lambda i: (0, i, 0)),
                   pl.BlockSpec((AT_KV_HEADS, tm, AT_HEAD), lambda i: (0, i, 0)),
                   pl.BlockSpec((AT_KV_HEADS, tm, AT_HEAD), lambda i: (0, i, 0))],
        compiler_params=_cparams(("parallel",)),
        name="attn_prep",
    )(z, z, cos, sin, qn, kn)


def _flash_kernel(q_ref, k_ref, v_ref, o_ref, m_ref, l_ref, acc_ref):
    ki = pl.program_id(2)
    grp, tq, hd = q_ref.shape

    @pl.when(ki == 0)
    def _():
        m_ref[...] = jnp.full_like(m_ref, -jnp.inf)
        l_ref[...] = jnp.zeros_like(l_ref)
        acc_ref[...] = jnp.zeros_like(acc_ref)

    q = q_ref[...].reshape(grp * tq, hd)
    s = lax.dot_general(q, k_ref[...], (((1,), (1,)), ((), ())), preferred_element_type=F32)
    m_prev = m_ref[...]
    m_new = jnp.maximum(m_prev, jnp.max(s, axis=-1, keepdims=True))
    alpha = jnp.exp(m_prev - m_new)
    p = jnp.exp(s - m_new)
    l_ref[...] = alpha * l_ref[...] + jnp.sum(p, axis=-1, keepdims=True)
    acc_ref[...] = alpha * acc_ref[...] + jnp.dot(p.astype(BF16), v_ref[...], preferred_element_type=F32)
    m_ref[...] = m_new

    @pl.when(ki == pl.num_programs(2) - 1)
    def _():
        o = acc_ref[...] / l_ref[...]
        for g in range(grp):
            o_ref[:, g * hd:(g + 1) * hd] = o[g * tq:(g + 1) * tq].astype(o_ref.dtype)


def _flash(q, k, v, *, tq, tk):
    s = q.shape[1]
    grp = AT_HEADS // AT_KV_HEADS
    return pl.pallas_call(
        _flash_kernel,
        out_shape=jax.ShapeDtypeStruct((s, AT_Q), BF16),
        grid=(AT_KV_HEADS, s // tq, s // tk),
        in_specs=[pl.BlockSpec((grp, tq, AT_HEAD), lambda h, i, j: (h, i, 0)),
                  pl.BlockSpec((None, tk, AT_HEAD), lambda h, i, j: (h, j, 0)),
                  pl.BlockSpec((None, tk, AT_HEAD), lambda h, i, j: (h, j, 0))],
        out_specs=pl.BlockSpec((tq, grp * AT_HEAD), lambda h, i, j: (i, h)),
        scratch_shapes=[pltpu.VMEM((grp * tq, 1), F32), pltpu.VMEM((grp * tq, 1), F32),
                        pltpu.VMEM((grp * tq, AT_HEAD), F32)],
        compiler_params=_cparams(("parallel", "parallel", "arbitrary")),
        name="flash_attention",
    )(q, k, v)


def _mix_out_kernel(y_ref, g_ref, bon_ref, yat_ref, zg1_ref, zg2_ref, x_ref, lnw, lnb,
                    wrw, wat, wo, npost, o_ref):
    ones = _head_ones()
    y = y_ref[0] + y_ref[1]
    inv = 1.0 / RW_HEAD
    mu = _head_sum(y, ones) * inv
    dlt = y - mu
    var = _head_sum(dlt * dlt, ones) * inv
    yn = dlt * lax.rsqrt(var + RW_GN_EPS) * lnw[...] + lnb[...]
    y_rw = (yn + bon_ref[...]) * g_ref[...]
    pa = _dot(y_rw, wrw[...])
    pb = jnp.dot(yat_ref[...], wat[...], preferred_element_type=F32)
    merged = _sigmoid(zg1_ref[...]) * pa + _sigmoid(zg2_ref[...]) * pb
    out = _dot(merged, wo[...])
    o_ref[...] = x_ref[...] + _rms(out, npost[...])


def _mix_out(y, g, bon, yat, z, x, lnw, lnb, wrw, wat, wo, npost, *, tm):
    s = x.shape[0]

    def const(arr):
        nd = arr.ndim
        return pl.BlockSpec(arr.shape, lambda i: (0,) * nd, pipeline_mode=pl.Buffered(1))

    row1k = pl.BlockSpec((tm, RW_WIDTH), lambda i: (i, 0))
    return pl.pallas_call(
        _mix_out_kernel,
        out_shape=jax.ShapeDtypeStruct((s, D_MODEL), F32),
        grid=(s // tm,),
        in_specs=[pl.BlockSpec((2, tm, RW_WIDTH), lambda i: (0, i, 0)), row1k, row1k, row1k,
                  pl.BlockSpec((tm, D_MODEL), lambda i: (i, COL_G1 // D_MODEL)),
                  pl.BlockSpec((tm, D_MODEL), lambda i: (i, COL_G2 // D_MODEL)),
                  pl.BlockSpec((tm, D_MODEL), lambda i: (i, 0)),
                  const(lnw), const(lnb), const(wrw), const(wat), const(wo), const(npost)],
        out_specs=pl.BlockSpec((tm, D_MODEL), lambda i: (i, 0)),
        compiler_params=_cparams(("parallel",)),
        name="mix_out",
    )(y, g, bon, yat, z, z, x, lnw, lnb, wrw, wat, wo, npost)


def _xattn_kernel(x_ref, kv_ref, npre, wq, wo, npost, o_ref):
    x = x_ref[...]
    h = _rms(x, npre[...])
    q = _dot(h, wq[...]) * (X_HEAD ** -0.5)
    outs = []
    for hd in range(X_HEADS):
        kh = kv_ref[:, hd * X_HEAD:(hd + 1) * X_HEAD]
        vh = kv_ref[:, X_WIDTH + hd * X_HEAD:X_WIDTH + (hd + 1) * X_HEAD]
        sc = _dot_nt(q[:, hd * X_HEAD:(hd + 1) * X_HEAD], kh)
        sc = sc - jnp.max(sc, axis=-1, keepdims=True)
        p = jnp.exp(sc)
        p = p / jnp.sum(p, axis=-1, keepdims=True)
        outs.append(_dot(p, vh))
    o = jnp.concatenate(outs, axis=1)
    c = _dot(o, wo[...])
    o_ref[...] = x + _rms(c, npost[...])


def _xattn(x, kv, npre, wq, wo, npost, *, tm):
    s = x.shape[0]

    def const(arr):
        nd = arr.ndim
        return pl.BlockSpec(arr.shape, lambda i: (0,) * nd)

    return pl.pallas_call(
        _xattn_kernel,
        out_shape=jax.ShapeDtypeStruct((s, D_MODEL), F32),
        grid=(s // tm,),
        in_specs=[pl.BlockSpec((tm, D_MODEL), lambda i: (i, 0)), const(kv), const(npre),
                  const(wq), const(wo), const(npost)],
        out_specs=pl.BlockSpec((tm, D_MODEL), lambda i: (i, 0)),
        compiler_params=_cparams(("parallel",)),
        name="cross_attention",
    )(x, kv, npre, wq, wo, npost)


def _ffn_kernel(x_ref, npre, wg, wu, wd, npost, o_ref, h_ref, acc_ref):
    j = pl.program_id(1)

    @pl.when(j == 0)
    def _():
        h_ref[...] = _rms(x_ref[...], npre[...]).astype(BF16)
        acc_ref[...] = jnp.zeros_like(acc_ref)

    h = h_ref[...]
    gt = jnp.dot(h, wg[...], preferred_element_type=F32)
    ut = jnp.dot(h, wu[...], preferred_element_type=F32)
    act = gt * _sigmoid(gt) * ut
    acc_ref[...] += _dot(act, wd[...])

    @pl.when(j == pl.num_programs(1) - 1)
    def _():
        o_ref[...] = x_ref[...] + _rms(acc_ref[...], npost[...])


def _ffn(x, npre, wg, wu, wd, npost, *, tm, tf):
    s = x.shape[0]
    f = wg.shape[1]
    return pl.pallas_call(
        _ffn_kernel,
        out_shape=jax.ShapeDtypeStruct((s, D_MODEL), F32),
        grid=(s // tm, f // tf),
        in_specs=[pl.BlockSpec((tm, D_MODEL), lambda i, j: (i, 0)),
                  pl.BlockSpec((1, D_MODEL), lambda i, j: (0, 0)),
                  pl.BlockSpec((D_MODEL, tf), lambda i, j: (0, j)),
                  pl.BlockSpec((D_MODEL, tf), lambda i, j: (0, j)),
                  pl.BlockSpec((tf, D_MODEL), lambda i, j: (j, 0)),
                  pl.BlockSpec((1, D_MODEL), lambda i, j: (0, 0))],
        out_specs=pl.BlockSpec((tm, D_MODEL), lambda i, j: (i, 0)),
        scratch_shapes=[pltpu.VMEM((tm, D_MODEL), BF16), pltpu.VMEM((tm, D_MODEL), F32)],
        compiler_params=_cparams(("parallel", "arbitrary")),
        name="swiglu",
    )(x, npre, wg, wu, wd, npost)


def _pack_w_in(w):
    o_lora = 3 * RW_WIDTH
    o_gd = o_lora + 2 * W_LORA + 2 * A_LORA
    o_at = o_gd + G_LORA
    o_gate = o_at + AT_Q + 2 * AT_KV
    d = w.shape[0]
    parts = [w[:, o_gate:o_gate + 2 * D_MODEL],
             w[:, 0:3 * RW_WIDTH],
             w[:, o_at:o_at + AT_Q],
             w[:, o_lora:o_at], jnp.zeros((d, G_PAD - G_LORA), w.dtype),
             w[:, o_at + AT_Q:o_gate]]
    return jnp.concatenate(parts, axis=1).astype(BF16)


def _pack_mu(mu):
    return jnp.concatenate([mu, jnp.zeros((G_PAD - G_LORA,), mu.dtype)])[None, :]


def _block_diag_up(up):
    z = jnp.zeros_like(up[0])
    return jnp.concatenate([jnp.concatenate([up[0], z], axis=1),
                            jnp.concatenate([z, up[1]], axis=1)], axis=0).astype(BF16)


def _rope_tables(s):
    rows = s // GRID_W
    row = jnp.repeat(jnp.arange(rows), GRID_W).astype(F32)
    col = jnp.tile(jnp.arange(GRID_W), rows).astype(F32)
    n_freq = AT_HEAD // 4
    inv = ROPE_THETA ** (-jnp.arange(n_freq, dtype=F32) / n_freq)
    ar = row[:, None] * inv
    ac = col[:, None] * inv
    cos = jnp.concatenate([jnp.cos(ar), jnp.cos(ar), jnp.cos(ac), jnp.cos(ac)], axis=1)
    sin = jnp.concatenate([-jnp.sin(ar), jnp.sin(ar), -jnp.sin(ac), jnp.sin(ac)], axis=1)
    return jnp.tile(cos, (1, 2)), jnp.tile(sin, (1, 2))


def kernel(x, mem, n_mix_pre, n_mix_post, n_x_pre, n_x_post, n_ffn_pre, n_ffn_post, n_mem, w_in, rw_mu_prev, rw_mu_next, rw_w0, rw_w_up, rw_a0, rw_a_up, rw_g_up, rw_v0, rw_v_down, rw_v_up, rw_k_k, rw_k_a, rw_r_k, rw_ln_w, rw_ln_b, w_rw_out, at_q_norm, at_k_norm, w_at_out, w_o, x_wq, x_wkv, x_wo, ffn_wg, ffn_wu, ffn_wd):
    b, s, d = x.shape
    assert b == 1 and d == D_MODEL and s % 512 == 0
    xs = x[0]
    mems = mem[0]
    cos, sin = _rope_tables(s)
    row = lambda t: t[None, :]
    v_first = None
    for l in range(DEPTH):
        z = _norm_proj(xs, row(n_mix_pre[l]), _pack_w_in(w_in[l]), tm=512, tn=1024, out_dtype=F32)

        vmix = None
        if l > 0:
            vmix = (v_first, row(rw_v0[l - 1]), rw_v_down[l - 1].astype(BF16), rw_v_up[l - 1].astype(BF16))
        gup = jnp.concatenate([rw_g_up[l], jnp.zeros((G_PAD - G_LORA, RW_WIDTH), F32)], axis=0).astype(BF16)
        r, v, kk, kd, lw, ba, g, bon = _rwprep(
            z, _pack_mu(rw_mu_prev[l]), _pack_mu(rw_mu_next[l]),
            rw_w0[l].reshape(1, 2 * RW_WIDTH), _block_diag_up(rw_w_up[l]),
            rw_a0[l].reshape(1, 2 * RW_WIDTH), _block_diag_up(rw_a_up[l]), gup,
            row(rw_k_k[l]), row(rw_k_a[l]), rw_r_k[l].reshape(1, RW_WIDTH), vmix, tm=256)
        if l == 0:
            v_first = v
        y = _rwrec(r, v, kk, kd, lw, ba)

        qn = jnp.tile(at_q_norm[l], 2)[None, :]
        kn = jnp.tile(at_k_norm[l], 2)[None, :]
        q_hm, k_hm, v_hm = _atprep(z, cos, sin, qn, kn, tm=512)
        y_at = _flash(q_hm, k_hm, v_hm, tq=256, tk=512)

        xs = _mix_out(y, g, bon, y_at, z, xs, row(rw_ln_w[l]), row(rw_ln_b[l]),
                      w_rw_out[l].astype(BF16), w_at_out[l].astype(BF16), w_o[l].astype(BF16),
                      row(n_mix_post[l]), tm=256)

        kv = _norm_proj(mems, row(n_mem[l]), x_wkv[l].astype(BF16), tm=mems.shape[0], tn=512, out_dtype=BF16)
        xs = _xattn(xs, kv, row(n_x_pre[l]), x_wq[l].astype(BF16), x_wo[l].astype(BF16),
                    row(n_x_post[l]), tm=512)

        xs = _ffn(xs, row(n_ffn_pre[l]), ffn_wg[l].astype(BF16), ffn_wu[l].astype(BF16),
                  ffn_wd[l].astype(BF16), row(n_ffn_post[l]), tm=512, tf=512)
    return xs[None]
```

```python
import functools

import jax
import jax.numpy as jnp
from jax import lax
from jax.experimental import pallas as pl
from jax.experimental.pallas import tpu as pltpu

F32 = jnp.float32
BF16 = jnp.bfloat16

D_MODEL = 2048
DEPTH = 4
GRID_W = 64
EPS = 1e-6
RW_HEAD = 64
RW_WIDTH = 1024
W_LORA = 64
A_LORA = 64
G_LORA = 160
RW_GN_EPS = 64e-5
AT_HEADS = 16
AT_KV_HEADS = 4
AT_HEAD = 64
AT_Q = AT_HEADS * AT_HEAD
AT_KV = AT_KV_HEADS * AT_HEAD
ROPE_THETA = 10000.0
X_HEADS = 4
X_HEAD = 128
X_WIDTH = X_HEADS * X_HEAD
D_FF = 5632

LANES = 128
SUBLANES = 8
CHUNK = 64
SUB = 16
G_PAD = 256

COL_G1 = 0
COL_G2 = 2048
COL_R = 4096
COL_K = 5120
COL_V = 6144
COL_Q = 7168
COL_LORA = 8192
COL_ATKV = 8704
IN_PACKED = 9216

VMEM_LIMIT = 56 * 1024 * 1024


def _cparams(sem):
    return pltpu.CompilerParams(dimension_semantics=sem, vmem_limit_bytes=VMEM_LIMIT)


def _dot(a, b):
    return jnp.dot(a.astype(BF16), b.astype(BF16), preferred_element_type=F32)


def _dot_nt(a, b):
    return lax.dot_general(a.astype(BF16), b.astype(BF16), (((1,), (1,)), ((), ())),
                           preferred_element_type=F32)


def _split2(x):
    hi = x.astype(BF16)
    lo = (x - hi.astype(F32)).astype(BF16)
    return hi, lo


def _split3(x):
    hi = x.astype(BF16)
    r1 = x - hi.astype(F32)
    mid = r1.astype(BF16)
    lo = (r1 - mid.astype(F32)).astype(BF16)
    return hi, mid, lo


def _sigmoid(x):
    return 1.0 / (1.0 + jnp.exp(-x))


def _softplus(x):
    return jnp.maximum(x, 0.0) + jnp.log(1.0 + jnp.exp(-jnp.abs(x)))


def _head_ones():
    r = lax.broadcasted_iota(jnp.int32, (LANES, LANES), 0) // RW_HEAD
    c = lax.broadcasted_iota(jnp.int32, (LANES, LANES), 1) // RW_HEAD
    return jnp.where(r == c, 1.0, 0.0).astype(BF16)


def _head_sum(x, ones):
    outs = []
    for s in range(x.shape[1] // LANES):
        hi, lo = _split2(x[:, s * LANES:(s + 1) * LANES])
        outs.append(jnp.dot(hi, ones, preferred_element_type=F32)
                    + jnp.dot(lo, ones, preferred_element_type=F32))
    return outs[0] if len(outs) == 1 else jnp.concatenate(outs, axis=1)


def _rms(x, gain):
    ms = jnp.mean(x * x, axis=-1, keepdims=True)
    return x * lax.rsqrt(ms + EPS) * gain


def _norm_proj_kernel(x_ref, g_ref, w_ref, o_ref, h_ref):
    @pl.when(pl.program_id(1) == 0)
    def _():
        h_ref[...] = _rms(x_ref[...], g_ref[...]).astype(BF16)

    o_ref[...] = jnp.dot(h_ref[...], w_ref[...], preferred_element_type=F32).astype(o_ref.dtype)


def _norm_proj(x, gain, w, *, tm, tn, out_dtype):
    m, d = x.shape
    n = w.shape[1]
    return pl.pallas_call(
        _norm_proj_kernel,
        out_shape=jax.ShapeDtypeStruct((m, n), out_dtype),
        grid=(m // tm, n // tn),
        in_specs=[pl.BlockSpec((tm, d), lambda i, j: (i, 0)),
                  pl.BlockSpec((1, d), lambda i, j: (0, 0)),
                  pl.BlockSpec((d, tn), lambda i, j: (0, j))],
        out_specs=pl.BlockSpec((tm, tn), lambda i, j: (i, j)),
        scratch_shapes=[pltpu.VMEM((tm, d), BF16)],
        compiler_params=_cparams(("parallel", "arbitrary")),
        name="norm_proj",
    )(x, gain, w)


def _shift(z, prev_row, next_row, mu_p, mu_n):
    rows = z.shape[0]
    ridx = lax.broadcasted_iota(jnp.int32, z.shape, 0)
    zp = jnp.where(ridx == 0, prev_row, pltpu.roll(z, 1, 0))
    zn = jnp.where(ridx == rows - 1, next_row, pltpu.roll(z, rows - 1, 0))
    return z + mu_p * (zp - z) + mu_n * (zn - z)


def _rwprep_kernel(has_vmix, *refs):
    (zr, zk, zv, zl, pr, pk, pv, pL, nr, nk, nv, nL, mup, mun, w0, wup, a0, aup, gup,
     kk_w, ka_w, rk_w) = refs[:22]
    pos = 22
    if has_vmix:
        vfirst, v0, vdown, vup = refs[pos:pos + 4]
        pos += 4
    r_o, v_o, kk_o, kd_o, lw_o, ba_o, g_o, bon_o = refs[pos:pos + 8]

    i = pl.program_id(0)
    last = pl.num_programs(0) - 1
    keep_p = jnp.where(i == 0, 0.0, 1.0)
    keep_n = jnp.where(i == last, 0.0, 1.0)

    def shifted(z_ref, p_ref, n_ref, lo, hi):
        p_row = p_ref[SUBLANES - 1:SUBLANES, :] * keep_p
        n_row = n_ref[0:1, :] * keep_n
        return _shift(z_ref[...], p_row, n_row, mup[:, lo:hi], mun[:, lo:hi])

    r = shifted(zr, pr, nr, 0, 1024)
    k = shifted(zk, pk, nk, 1024, 2048)
    v = shifted(zv, pv, nv, 2048, 3072)
    lora = shifted(zl, pL, nL, 3072, 3584)

    if has_vmix:
        mix = _sigmoid(v0[...] + _dot(_dot(v, vdown[...]), vup[...]))
        v = v + (vfirst[...] - v) * mix

    u = w0[...] + _dot(jnp.tanh(lora[:, 0:128]), wup[...])
    w_log = -_softplus(-u) - 0.5
    lw = -jnp.exp(w_log)
    a = _sigmoid(a0[...] + _dot(lora[:, 128:256], aup[...]))
    g = _dot(_sigmoid(lora[:, 256:512]), gup[...])

    ones = _head_ones()
    kk = k * kk_w[...]
    norm = jnp.sqrt(_head_sum(kk * kk, ones))
    kk = kk / jnp.maximum(norm, 1e-12)
    ka = ka_w[...]
    a_f = a[:, :RW_WIDTH]
    a_b = a[:, RW_WIDTH:]
    kd_f = k * (1.0 + (a_f - 1.0) * ka)
    kd_b = k * (1.0 + (a_b - 1.0) * ka)
    bonus = _head_sum(r * rk_w[...] * (kd_f + kd_b), ones) * v

    r_o[...] = r
    v_o[...] = v
    kk_o[...] = kk
    kd_o[0] = kd_f
    kd_o[1] = kd_b
    lw_o[0] = lw[:, :RW_WIDTH]
    lw_o[1] = lw[:, RW_WIDTH:]
    ba_o[0] = kk * a_f
    ba_o[1] = kk * a_b
    g_o[...] = g
    bon_o[...] = bonus


def _rwprep(z, mup, mun, w0, wup, a0, aup, gup, kk_w, ka_w, rk_w, vmix, *, tm):
    s = z.shape[0]
    nb8 = s // SUBLANES
    tb = tm // SUBLANES
    has_vmix = vmix is not None

    def main(width, cb):
        return pl.BlockSpec((tm, width), lambda i: (i, cb))

    def prev(width, cb):
        return pl.BlockSpec((SUBLANES, width), lambda i: (jnp.maximum(i * tb - 1, 0), cb))

    def nxt(width, cb):
        return pl.BlockSpec((SUBLANES, width), lambda i: (jnp.minimum((i + 1) * tb, nb8 - 1), cb))

    def full(arr):
        nd = arr.ndim
        return pl.BlockSpec(arr.shape, lambda i: (0,) * nd)

    cols = [(1024, COL_R // 1024), (1024, COL_K // 1024), (1024, COL_V // 1024), (512, COL_LORA // 512)]
    in_specs = ([main(w, c) for w, c in cols] + [prev(w, c) for w, c in cols]
                + [nxt(w, c) for w, c in cols])
    args = [z] * 12
    consts = [mup, mun, w0, wup, a0, aup, gup, kk_w, ka_w, rk_w]
    in_specs += [full(c) for c in consts]
    args += consts
    if has_vmix:
        vfirst, v0, vdown, vup = vmix
        in_specs += [pl.BlockSpec((tm, RW_WIDTH), lambda i: (i, 0)), full(v0), full(vdown), full(vup)]
        args += [vfirst, v0, vdown, vup]

    one = jax.ShapeDtypeStruct((s, RW_WIDTH), F32)
    two = jax.ShapeDtypeStruct((2, s, RW_WIDTH), F32)
    spec1 = pl.BlockSpec((tm, RW_WIDTH), lambda i: (i, 0))
    spec2 = pl.BlockSpec((2, tm, RW_WIDTH), lambda i: (0, i, 0))
    return pl.pallas_call(
        functools.partial(_rwprep_kernel, has_vmix),
        out_shape=[one, one, one, two, two, two, one, one],
        grid=(s // tm,),
        in_specs=in_specs,
        out_specs=[spec1, spec1, spec1, spec2, spec2, spec2, spec1, spec1],
        compiler_params=_cparams(("parallel",)),
        name="rwkv_prep",
    )(*args)


def _rwrec_kernel(r_ref, v_ref, kk_ref, kd_ref, lw_ref, ba_ref, y_ref, s_ref):
    d = pl.program_id(0)
    T = CHUNK

    @pl.when(pl.program_id(1) == 0)
    def _():
        s_ref[...] = jnp.zeros_like(s_ref)

    sgn = 1 - 2 * d
    row = lax.broadcasted_iota(jnp.int32, (T, T), 0)
    col = lax.broadcasted_iota(jnp.int32, (T, T), 1)
    tri = jnp.where((row - col) * sgn >= 0, 1.0, 0.0).astype(BF16)

    lw = lw_ref[...]
    hi, mid, lo = _split3(lw)
    cin = (jnp.dot(tri, hi, preferred_element_type=F32)
           + jnp.dot(tri, mid, preferred_element_type=F32)
           + jnp.dot(tri, lo, preferred_element_type=F32))
    ctot = jnp.sum(lw, axis=0, keepdims=True)
    e_in = jnp.exp(cin)
    e_ex = jnp.exp(cin - lw)
    e_neg = jnp.exp(-cin)
    e_end = jnp.exp(ctot - cin)
    e_tot = jnp.exp(ctot)

    kd = kd_ref[...]
    ba = ba_ref[...]
    r_t = r_ref[...] * e_in
    a_t = -kk_ref[...] * e_ex
    b_t = ba * e_neg
    k_t = kd * e_neg
    b_e = ba * e_end
    k_e = kd * e_end
    v = v_ref[...]

    R2 = lax.broadcasted_iota(jnp.int32, (2 * T, 2 * T), 0)
    C2 = lax.broadcasted_iota(jnp.int32, (2 * T, 2 * T), 1)
    same = (R2 // T) == (C2 // T)
    diff = ((R2 % T) - (C2 % T)) * sgn
    mask_a = same & (diff > 0)
    mask_y = same & (diff >= 0)
    mask_d = same & (((R2 % T) // SUB) == ((C2 % T) // SUB))
    eye = R2 == C2
    head0 = lax.broadcasted_iota(jnp.int32, (T, LANES), 1) < RW_HEAD

    def stack_heads(x):
        return jnp.concatenate([jnp.where(head0, x, 0.0), jnp.where(head0, 0.0, x)], axis=0)

    def stack_dup(x):
        return jnp.concatenate([x, x], axis=0)

    for sl in range(RW_WIDTH // LANES):
        cs = slice(sl * LANES, (sl + 1) * LANES)
        a2 = stack_heads(a_t[:, cs]).astype(BF16)
        r2 = stack_heads(r_t[:, cs]).astype(BF16)
        v2 = stack_heads(v[:, cs]).astype(BF16)
        b2 = stack_dup(b_t[:, cs]).astype(BF16)
        k2 = stack_dup(k_t[:, cs]).astype(BF16)
        be2 = stack_heads(b_e[:, cs]).astype(BF16)
        ke2 = stack_heads(k_e[:, cs]).astype(BF16)

        n_ab = jnp.where(mask_a, _dot_nt(a2, b2), 0.0)
        n_ak = jnp.where(mask_a, _dot_nt(a2, k2), 0.0)
        m_rb = jnp.where(mask_y, _dot_nt(r2, b2), 0.0)
        m_rk = jnp.where(mask_y, _dot_nt(r2, k2), 0.0)

        st = s_ref[sl]
        st_b = st.astype(BF16)
        x = _dot_nt(a2, st_b) + _dot(n_ak, v2)
        p = jnp.where(mask_d, n_ab, 0.0)
        q = jnp.where(eye, 1.0, p)
        for _ in range(3):
            p = _dot(p, p)
            q = q + _dot(q, p)
        m = _dot(q, jnp.where(mask_d, 0.0, n_ab))
        x = _dot(q, x)
        x = x + _dot(m, x)
        x = x + _dot(_dot(m, m), x)
        y2 = _dot_nt(r2, st_b) + _dot(jnp.concatenate([m_rb, m_rk], axis=1),
                                      jnp.concatenate([x.astype(BF16), v2], axis=0))
        y_ref[:, cs] = y2[:T] + y2[T:]
        upd = _dot(jnp.concatenate([x.T, v2.astype(F32).T], axis=1),
                   jnp.concatenate([be2, ke2], axis=0))
        s_ref[sl] = st * e_tot[:, cs] + upd


def _rwrec(r, v, kk, kd, lw, ba):
    s = r.shape[0]
    nc = s // CHUNK

    def cidx(d, c):
        return c + d * (nc - 1 - 2 * c)

    spec1 = pl.BlockSpec((CHUNK, RW_WIDTH), lambda d, c: (cidx(d, c), 0))
    spec2 = pl.BlockSpec((None, CHUNK, RW_WIDTH), lambda d, c: (d, cidx(d, c), 0))
    return pl.pallas_call(
        _rwrec_kernel,
        out_shape=jax.ShapeDtypeStruct((2, s, RW_WIDTH), F32),
        grid=(2, nc),
        in_specs=[spec1, spec1, spec1, spec2, spec2, spec2],
        out_specs=spec2,
        scratch_shapes=[pltpu.VMEM((RW_WIDTH // LANES, LANES, LANES), F32)],
        compiler_params=_cparams(("arbitrary", "arbitrary")),
        name="rwkv_recurrence",
    )(r, v, kk, kd, lw, ba)


def _atprep_kernel(zq, zkv, cos_ref, sin_ref, qn, kn, qt_o, k_o, vt_o):
    ones = _head_ones()
    cos = cos_ref[...]
    sin = sin_ref[...]
    lane = lax.broadcasted_iota(jnp.int32, cos.shape, 1)
    even = ((lane // 16) % 2) == 0

    def norm_rope(t, gain):
        ms = _head_sum(t * t, ones) * (1.0 / AT_HEAD)
        tn = t * lax.rsqrt(ms + EPS) * gain
        swapped = jnp.where(even, pltpu.roll(tn, LANES - 16, 1), pltpu.roll(tn, 16, 1))
        return tn * cos + swapped * sin

    for sl in range(AT_Q // LANES):
        out = norm_rope(zq[:, sl * LANES:(sl + 1) * LANES], qn[...]) * (AT_HEAD ** -0.5)
        out_t = out.T.astype(BF16)
        qt_o[2 * sl] = out_t[:AT_HEAD]
        qt_o[2 * sl + 1] = out_t[AT_HEAD:]
    for sl in range(AT_KV // LANES):
        out = norm_rope(zkv[:, sl * LANES:(sl + 1) * LANES], kn[...])
        k_o[2 * sl] = out[:, :AT_HEAD].astype(BF16)
        k_o[2 * sl + 1] = out[:, AT_HEAD:].astype(BF16)
        vt = zkv[:, AT_KV + sl * LANES:AT_KV + (sl + 1) * LANES].T.astype(BF16)
        vt_o[2 * sl] = vt[:AT_HEAD]
        vt_o[2 * sl + 1] = vt[AT_HEAD:]


def _atprep(z, cos, sin, qn, kn, *, tm):
    s = z.shape[0]
    nb = s // tm
    return pl.pallas_call(
        _atprep_kernel,
        out_shape=[jax.ShapeDtypeStruct((AT_HEADS, AT_HEAD, s), BF16),
                   jax.ShapeDtypeStruct((AT_KV_HEADS, nb, tm, AT_HEAD), BF16),
                   jax.ShapeDtypeStruct((AT_KV_HEADS, nb, AT_HEAD, tm), BF16)],
        grid=(nb,),
        in_specs=[pl.BlockSpec((tm, AT_Q), lambda i: (i, COL_Q // AT_Q)),
                  pl.BlockSpec((tm, 2 * AT_KV), lambda i: (i, COL_ATKV // (2 * AT_KV))),
                  pl.BlockSpec((tm, LANES), lambda i: (i, 0)),
                  pl.BlockSpec((tm, LANES), lambda i: (i, 0)),
                  pl.BlockSpec((1, LANES), lambda i: (0, 0)),
                  pl.BlockSpec((1, LANES), lambda i: (0, 0))],
        out_specs=[pl.BlockSpec((AT_HEADS, AT_HEAD, tm), lambda i: (0, 0, i)),
                   pl.BlockSpec((AT_KV_HEADS, None, tm, AT_HEAD), lambda i: (0, i, 0, 0)),
                   pl.BlockSpec((AT_KV_HEADS, None, AT_HEAD, tm), lambda i: (0, i, 0, 0))],
        compiler_params=_cparams(("parallel",)),
        name="attn_prep",
    )(z, z, cos, sin, qn, kn)


def _flash_kernel(qt_ref, k_ref, vt_ref, o_ref, m_ref, l_ref, acc_ref):
    grp = qt_ref.shape[0]
    nkb = k_ref.shape[0]
    m_ref[...] = jnp.full_like(m_ref, -jnp.inf)
    l_ref[...] = jnp.zeros_like(l_ref)
    acc_ref[...] = jnp.zeros_like(acc_ref)

    def body(kb, carry):
        k = k_ref[kb]
        vt = vt_ref[kb]
        sts = [jnp.dot(k, qt_ref[g], preferred_element_type=F32) for g in range(grp)]
        pts, alphas = [], []
        for g in range(grp):
            m_prev = m_ref[g]
            m_new = jnp.maximum(m_prev, jnp.max(sts[g], axis=0, keepdims=True))
            alpha = jnp.exp(m_prev - m_new)
            pt = jnp.exp(sts[g] - m_new)
            l_ref[g] = alpha * l_ref[g] + jnp.sum(pt, axis=0, keepdims=True)
            m_ref[g] = m_new
            pts.append(pt.astype(BF16))
            alphas.append(alpha)
        for g in range(grp):
            acc_ref[g] = alphas[g] * acc_ref[g] + jnp.dot(vt, pts[g], preferred_element_type=F32)
        return carry

    lax.fori_loop(0, nkb, body, 0)
    for pair in range(grp // 2):
        o2 = jnp.concatenate([acc_ref[2 * pair] / l_ref[2 * pair],
                              acc_ref[2 * pair + 1] / l_ref[2 * pair + 1]], axis=0)
        o_ref[:, pair * LANES:(pair + 1) * LANES] = o2.T.astype(o_ref.dtype)


def _flash(qt, k, vt, *, tq):
    s = qt.shape[2]
    _, nkb, tk, _ = k.shape
    grp = AT_HEADS // AT_KV_HEADS
    return pl.pallas_call(
        _flash_kernel,
        out_shape=jax.ShapeDtypeStruct((s, AT_Q), BF16),
        grid=(AT_KV_HEADS, s // tq),
        in_specs=[pl.BlockSpec((grp, AT_HEAD, tq), lambda h, i: (h, 0, i)),
                  pl.BlockSpec((None, nkb, tk, AT_HEAD), lambda h, i: (h, 0, 0, 0)),
                  pl.BlockSpec((None, nkb, AT_HEAD, tk), lambda h, i: (h, 0, 0, 0))],
        out_specs=pl.BlockSpec((tq, grp * AT_HEAD), lambda h, i: (i, h)),
        scratch_shapes=[pltpu.VMEM((grp, 1, tq), F32), pltpu.VMEM((grp, 1, tq), F32),
                        pltpu.VMEM((grp, AT_HEAD, tq), F32)],
        compiler_params=_cparams(("parallel", "parallel")),
        name="flash_attention",
    )(qt, k, vt)


def _mix_out_kernel(y_ref, g_ref, bon_ref, yat_ref, zg1_ref, zg2_ref, x_ref, lnw, lnb,
                    wrw, wat, wo, npost, o_ref):
    ones = _head_ones()
    y = y_ref[0] + y_ref[1]
    inv = 1.0 / RW_HEAD
    mu = _head_sum(y, ones) * inv
    dlt = y - mu
    var = _head_sum(dlt * dlt, ones) * inv
    yn = dlt * lax.rsqrt(var + RW_GN_EPS) * lnw[...] + lnb[...]
    y_rw = (yn + bon_ref[...]) * g_ref[...]
    pa = _dot(y_rw, wrw[...])
    pb = jnp.dot(yat_ref[...], wat[...], preferred_element_type=F32)
    merged = _sigmoid(zg1_ref[...]) * pa + _sigmoid(zg2_ref[...]) * pb
    out = _dot(merged, wo[...])
    o_ref[...] = x_ref[...] + _rms(out, npost[...])


def _mix_out(y, g, bon, yat, z, x, lnw, lnb, wrw, wat, wo, npost, *, tm):
    s = x.shape[0]

    def const(arr):
        nd = arr.ndim
        return pl.BlockSpec(arr.shape, lambda i: (0,) * nd, pipeline_mode=pl.Buffered(1))

    row1k = pl.BlockSpec((tm, RW_WIDTH), lambda i: (i, 0))
    return pl.pallas_call(
        _mix_out_kernel,
        out_shape=jax.ShapeDtypeStruct((s, D_MODEL), F32),
        grid=(s // tm,),
        in_specs=[pl.BlockSpec((2, tm, RW_WIDTH), lambda i: (0, i, 0)), row1k, row1k, row1k,
                  pl.BlockSpec((tm, D_MODEL), lambda i: (i, COL_G1 // D_MODEL)),
                  pl.BlockSpec((tm, D_MODEL), lambda i: (i, COL_G2 // D_MODEL)),
                  pl.BlockSpec((tm, D_MODEL), lambda i: (i, 0)),
                  const(lnw), const(lnb), const(wrw), const(wat), const(wo), const(npost)],
        out_specs=pl.BlockSpec((tm, D_MODEL), lambda i: (i, 0)),
        compiler_params=_cparams(("parallel",)),
        name="mix_out",
    )(y, g, bon, yat, z, z, x, lnw, lnb, wrw, wat, wo, npost)


def _xattn_kernel(x_ref, kv_ref, npre, wq, wo, npost, o_ref):
    x = x_ref[...]
    h = _rms(x, npre[...])
    q = _dot(h, wq[...]) * (X_HEAD ** -0.5)
    outs = []
    for hd in range(X_HEADS):
        kh = kv_ref[:, hd * X_HEAD:(hd + 1) * X_HEAD]
        vh = kv_ref[:, X_WIDTH + hd * X_HEAD:X_WIDTH + (hd + 1) * X_HEAD]
        sc = _dot_nt(q[:, hd * X_HEAD:(hd + 1) * X_HEAD], kh)
        sc = sc - jnp.max(sc, axis=-1, keepdims=True)
        p = jnp.exp(sc)
        p = p / jnp.sum(p, axis=-1, keepdims=True)
        outs.append(_dot(p, vh))
    o = jnp.concatenate(outs, axis=1)
    c = _dot(o, wo[...])
    o_ref[...] = x + _rms(c, npost[...])


def _xattn(x, kv, npre, wq, wo, npost, *, tm):
    s = x.shape[0]

    def const(arr):
        nd = arr.ndim
        return pl.BlockSpec(arr.shape, lambda i: (0,) * nd)

    return pl.pallas_call(
        _xattn_kernel,
        out_shape=jax.ShapeDtypeStruct((s, D_MODEL), F32),
        grid=(s // tm,),
        in_specs=[pl.BlockSpec((tm, D_MODEL), lambda i: (i, 0)), const(kv), const(npre),
                  const(wq), const(wo), const(npost)],
        out_specs=pl.BlockSpec((tm, D_MODEL), lambda i: (i, 0)),
        compiler_params=_cparams(("parallel",)),
        name="cross_attention",
    )(x, kv, npre, wq, wo, npost)


def _ffn_kernel(x_ref, npre, wg, wu, wd, npost, o_ref, h_ref, acc_ref):
    j = pl.program_id(1)

    @pl.when(j == 0)
    def _():
        h_ref[...] = _rms(x_ref[...], npre[...]).astype(BF16)
        acc_ref[...] = jnp.zeros_like(acc_ref)

    h = h_ref[...]
    gt = jnp.dot(h, wg[...], preferred_element_type=F32)
    ut = jnp.dot(h, wu[...], preferred_element_type=F32)
    act = gt * _sigmoid(gt) * ut
    acc_ref[...] += _dot(act, wd[...])

    @pl.when(j == pl.num_programs(1) - 1)
    def _():
        o_ref[...] = x_ref[...] + _rms(acc_ref[...], npost[...])


def _ffn(x, npre, wg, wu, wd, npost, *, tm, tf):
    s = x.shape[0]
    f = wg.shape[1]
    return pl.pallas_call(
        _ffn_kernel,
        out_shape=jax.ShapeDtypeStruct((s, D_MODEL), F32),
        grid=(s // tm, f // tf),
        in_specs=[pl.BlockSpec((tm, D_MODEL), lambda i, j: (i, 0)),
                  pl.BlockSpec((1, D_MODEL), lambda i, j: (0, 0)),
                  pl.BlockSpec((D_MODEL, tf), lambda i, j: (0, j)),
                  pl.BlockSpec((D_MODEL, tf), lambda i, j: (0, j)),
                  pl.BlockSpec((tf, D_MODEL), lambda i, j: (j, 0)),
                  pl.BlockSpec((1, D_MODEL), lambda i, j: (0, 0))],
        out_specs=pl.BlockSpec((tm, D_MODEL), lambda i, j: (i, 0)),
        scratch_shapes=[pltpu.VMEM((tm, D_MODEL), BF16), pltpu.VMEM((tm, D_MODEL), F32)],
        compiler_params=_cparams(("parallel", "arbitrary")),
        name="swiglu",
    )(x, npre, wg, wu, wd, npost)


def _pack_w_in(w):
    o_lora = 3 * RW_WIDTH
    o_gd = o_lora + 2 * W_LORA + 2 * A_LORA
    o_at = o_gd + G_LORA
    o_gate = o_at + AT_Q + 2 * AT_KV
    d = w.shape[0]
    parts = [w[:, o_gate:o_gate + 2 * D_MODEL],
             w[:, 0:3 * RW_WIDTH],
             w[:, o_at:o_at + AT_Q],
             w[:, o_lora:o_at], jnp.zeros((d, G_PAD - G_LORA), w.dtype),
             w[:, o_at + AT_Q:o_gate]]
    return jnp.concatenate(parts, axis=1).astype(BF16)


def _pack_mu(mu):
    return jnp.concatenate([mu, jnp.zeros((G_PAD - G_LORA,), mu.dtype)])[None, :]


def _block_diag_up(up):
    z = jnp.zeros_like(up[0])
    return jnp.concatenate([jnp.concatenate([up[0], z], axis=1),
                            jnp.concatenate([z, up[1]], axis=1)], axis=0).astype(BF16)


def _rope_tables(s):
    rows = s // GRID_W
    row = jnp.repeat(jnp.arange(rows), GRID_W).astype(F32)
    col = jnp.tile(jnp.arange(GRID_W), rows).astype(F32)
    n_freq = AT_HEAD // 4
    inv = ROPE_THETA ** (-jnp.arange(n_freq, dtype=F32) / n_freq)
    ar = row[:, None] * inv
    ac = col[:, None] * inv
    cos = jnp.concatenate([jnp.cos(ar), jnp.cos(ar), jnp.cos(ac), jnp.cos(ac)], axis=1)
    sin = jnp.concatenate([-jnp.sin(ar), jnp.sin(ar), -jnp.sin(ac), jnp.sin(ac)], axis=1)
    return jnp.tile(cos, (1, 2)), jnp.tile(sin, (1, 2))


def kernel(x, mem, n_mix_pre, n_mix_post, n_x_pre, n_x_post, n_ffn_pre, n_ffn_post, n_mem, w_in, rw_mu_prev, rw_mu_next, rw_w0, rw_w_up, rw_a0, rw_a_up, rw_g_up, rw_v0, rw_v_down, rw_v_up, rw_k_k, rw_k_a, rw_r_k, rw_ln_w, rw_ln_b, w_rw_out, at_q_norm, at_k_norm, w_at_out, w_o, x_wq, x_wkv, x_wo, ffn_wg, ffn_wu, ffn_wd):
    b, s, d = x.shape
    assert b == 1 and d == D_MODEL and s % 512 == 0
    xs = x[0]
    mems = mem[0]
    cos, sin = _rope_tables(s)
    row = lambda t: t[None, :]
    v_first = None
    for l in range(DEPTH):
        z = _norm_proj(xs, row(n_mix_pre[l]), _pack_w_in(w_in[l]), tm=512, tn=1024, out_dtype=F32)

        vmix = None
        if l > 0:
            vmix = (v_first, row(rw_v0[l - 1]), rw_v_down[l - 1].astype(BF16), rw_v_up[l - 1].astype(BF16))
        gup = jnp.concatenate([rw_g_up[l], jnp.zeros((G_PAD - G_LORA, RW_WIDTH), F32)], axis=0).astype(BF16)
        r, v, kk, kd, lw, ba, g, bon = _rwprep(
            z, _pack_mu(rw_mu_prev[l]), _pack_mu(rw_mu_next[l]),
            rw_w0[l].reshape(1, 2 * RW_WIDTH), _block_diag_up(rw_w_up[l]),
            rw_a0[l].reshape(1, 2 * RW_WIDTH), _block_diag_up(rw_a_up[l]), gup,
            row(rw_k_k[l]), row(rw_k_a[l]), rw_r_k[l].reshape(1, RW_WIDTH), vmix, tm=256)
        if l == 0:
            v_first = v
        y = _rwrec(r, v, kk, kd, lw, ba)

        qn = jnp.tile(at_q_norm[l], 2)[None, :]
        kn = jnp.tile(at_k_norm[l], 2)[None, :]
        q_t, k_blk, v_t = _atprep(z, cos, sin, qn, kn, tm=512)
        y_at = _flash(q_t, k_blk, v_t, tq=256)

        xs = _mix_out(y, g, bon, y_at, z, xs, row(rw_ln_w[l]), row(rw_ln_b[l]),
                      w_rw_out[l].astype(BF16), w_at_out[l].astype(BF16), w_o[l].astype(BF16),
                      row(n_mix_post[l]), tm=256)

        kv = _norm_proj(mems, row(n_mem[l]), x_wkv[l].astype(BF16), tm=mems.shape[0], tn=512, out_dtype=BF16)
        xs = _xattn(xs, kv, row(n_x_pre[l]), x_wq[l].astype(BF16), x_wo[l].astype(BF16),
                    row(n_x_post[l]), tm=512)

        xs = _ffn(xs, row(n_ffn_pre[l]), ffn_wg[l].astype(BF16), ffn_wu[l].astype(BF16),
                  ffn_wd[l].astype(BF16), row(n_ffn_post[l]), tm=512, tf=512)
    return xs[None]
```

```python
import functools

import jax
import jax.numpy as jnp
from jax import lax
from jax.experimental import pallas as pl
from jax.experimental.pallas import tpu as pltpu

F32 = jnp.float32
BF16 = jnp.bfloat16

D_MODEL = 2048
DEPTH = 4
GRID_W = 64
EPS = 1e-6
RW_HEAD = 64
RW_WIDTH = 1024
W_LORA = 64
A_LORA = 64
G_LORA = 160
RW_GN_EPS = 64e-5
AT_HEADS = 16
AT_KV_HEADS = 4
AT_HEAD = 64
AT_Q = AT_HEADS * AT_HEAD
AT_KV = AT_KV_HEADS * AT_HEAD
ROPE_THETA = 10000.0
X_HEADS = 4
X_HEAD = 128
X_WIDTH = X_HEADS * X_HEAD
D_FF = 5632

LANES = 128
SUBLANES = 8
CHUNK = 64
SUB = 16
G_PAD = 256

COL_G1 = 0
COL_G2 = 2048
COL_R = 4096
COL_K = 5120
COL_V = 6144
COL_Q = 7168
COL_LORA = 8192
COL_ATKV = 8704
IN_PACKED = 9216

VMEM_LIMIT = 56 * 1024 * 1024


def _cparams(sem):
    return pltpu.CompilerParams(dimension_semantics=sem, vmem_limit_bytes=VMEM_LIMIT)


def _dot(a, b):
    return jnp.dot(a.astype(BF16), b.astype(BF16), preferred_element_type=F32)


def _dot_nt(a, b):
    return lax.dot_general(a.astype(BF16), b.astype(BF16), (((1,), (1,)), ((), ())),
                           preferred_element_type=F32)


def _split2(x):
    hi = x.astype(BF16)
    lo = (x - hi.astype(F32)).astype(BF16)
    return hi, lo


def _split3(x):
    hi = x.astype(BF16)
    r1 = x - hi.astype(F32)
    mid = r1.astype(BF16)
    lo = (r1 - mid.astype(F32)).astype(BF16)
    return hi, mid, lo


def _sigmoid(x):
    return 1.0 / (1.0 + jnp.exp(-x))


def _softplus(x):
    return jnp.maximum(x, 0.0) + jnp.log(1.0 + jnp.exp(-jnp.abs(x)))


def _head_ones():
    r = lax.broadcasted_iota(jnp.int32, (LANES, LANES), 0) // RW_HEAD
    c = lax.broadcasted_iota(jnp.int32, (LANES, LANES), 1) // RW_HEAD
    return jnp.where(r == c, 1.0, 0.0).astype(BF16)


def _head_sum(x, ones):
    outs = []
    for s in range(x.shape[1] // LANES):
        hi, lo = _split2(x[:, s * LANES:(s + 1) * LANES])
        outs.append(jnp.dot(hi, ones, preferred_element_type=F32)
                    + jnp.dot(lo, ones, preferred_element_type=F32))
    return outs[0] if len(outs) == 1 else jnp.concatenate(outs, axis=1)


def _rms(x, gain):
    ms = jnp.mean(x * x, axis=-1, keepdims=True)
    return x * lax.rsqrt(ms + EPS) * gain


def _norm_proj_kernel(x_ref, g_ref, w_ref, o_ref, h_ref):
    @pl.when(pl.program_id(1) == 0)
    def _():
        h_ref[...] = _rms(x_ref[...], g_ref[...]).astype(BF16)

    o_ref[...] = jnp.dot(h_ref[...], w_ref[...], preferred_element_type=F32).astype(o_ref.dtype)


def _norm_proj(x, gain, w, *, tm, tn, out_dtype):
    m, d = x.shape
    n = w.shape[1]
    return pl.pallas_call(
        _norm_proj_kernel,
        out_shape=jax.ShapeDtypeStruct((m, n), out_dtype),
        grid=(m // tm, n // tn),
        in_specs=[pl.BlockSpec((tm, d), lambda i, j: (i, 0)),
                  pl.BlockSpec((1, d), lambda i, j: (0, 0)),
                  pl.BlockSpec((d, tn), lambda i, j: (0, j))],
        out_specs=pl.BlockSpec((tm, tn), lambda i, j: (i, j)),
        scratch_shapes=[pltpu.VMEM((tm, d), BF16)],
        compiler_params=_cparams(("parallel", "arbitrary")),
        name="norm_proj",
    )(x, gain, w)


def _shift(z, prev_row, next_row, mu_p, mu_n):
    rows = z.shape[0]
    ridx = lax.broadcasted_iota(jnp.int32, z.shape, 0)
    zp = jnp.where(ridx == 0, prev_row, pltpu.roll(z, 1, 0))
    zn = jnp.where(ridx == rows - 1, next_row, pltpu.roll(z, rows - 1, 0))
    return z + mu_p * (zp - z) + mu_n * (zn - z)


def _rwprep_kernel(has_vmix, *refs):
    (zr, zk, zv, zl, pr, pk, pv, pL, nr, nk, nv, nL, mup, mun, w0, wup, a0, aup, gup,
     kk_w, ka_w, rk_w) = refs[:22]
    pos = 22
    if has_vmix:
        vfirst, v0, vdown, vup = refs[pos:pos + 4]
        pos += 4
    r_o, v_o, kk_o, kd_o, lw_o, ba_o, g_o, bon_o = refs[pos:pos + 8]

    i = pl.program_id(0)
    last = pl.num_programs(0) - 1
    keep_p = jnp.where(i == 0, 0.0, 1.0)
    keep_n = jnp.where(i == last, 0.0, 1.0)

    def shifted(z_ref, p_ref, n_ref, lo, hi):
        p_row = p_ref[SUBLANES - 1:SUBLANES, :] * keep_p
        n_row = n_ref[0:1, :] * keep_n
        return _shift(z_ref[...], p_row, n_row, mup[:, lo:hi], mun[:, lo:hi])

    r = shifted(zr, pr, nr, 0, 1024)
    k = shifted(zk, pk, nk, 1024, 2048)
    v = shifted(zv, pv, nv, 2048, 3072)
    lora = shifted(zl, pL, nL, 3072, 3584)

    if has_vmix:
        mix = _sigmoid(v0[...] + _dot(_dot(v, vdown[...]), vup[...]))
        v = v + (vfirst[...] - v) * mix

    u = w0[...] + _dot(jnp.tanh(lora[:, 0:128]), wup[...])
    w_log = -_softplus(-u) - 0.5
    lw = -jnp.exp(w_log)
    a = _sigmoid(a0[...] + _dot(lora[:, 128:256], aup[...]))
    g = _dot(_sigmoid(lora[:, 256:512]), gup[...])

    ones = _head_ones()
    kk = k * kk_w[...]
    norm = jnp.sqrt(_head_sum(kk * kk, ones))
    kk = kk / jnp.maximum(norm, 1e-12)
    ka = ka_w[...]
    a_f = a[:, :RW_WIDTH]
    a_b = a[:, RW_WIDTH:]
    kd_f = k * (1.0 + (a_f - 1.0) * ka)
    kd_b = k * (1.0 + (a_b - 1.0) * ka)
    bonus = _head_sum(r * rk_w[...] * (kd_f + kd_b), ones) * v

    r_o[...] = r
    v_o[...] = v
    kk_o[...] = kk
    kd_o[0] = kd_f
    kd_o[1] = kd_b
    lw_o[0] = lw[:, :RW_WIDTH]
    lw_o[1] = lw[:, RW_WIDTH:]
    ba_o[0] = kk * a_f
    ba_o[1] = kk * a_b
    g_o[...] = g
    bon_o[...] = bonus


def _rwprep(z, mup, mun, w0, wup, a0, aup, gup, kk_w, ka_w, rk_w, vmix, *, tm):
    s = z.shape[0]
    nb8 = s // SUBLANES
    tb = tm // SUBLANES
    has_vmix = vmix is not None

    def main(width, cb):
        return pl.BlockSpec((tm, width), lambda i: (i, cb))

    def prev(width, cb):
        return pl.BlockSpec((SUBLANES, width), lambda i: (jnp.maximum(i * tb - 1, 0), cb))

    def nxt(width, cb):
        return pl.BlockSpec((SUBLANES, width), lambda i: (jnp.minimum((i + 1) * tb, nb8 - 1), cb))

    def full(arr):
        nd = arr.ndim
        return pl.BlockSpec(arr.shape, lambda i: (0,) * nd)

    cols = [(1024, COL_R // 1024), (1024, COL_K // 1024), (1024, COL_V // 1024), (512, COL_LORA // 512)]
    in_specs = ([main(w, c) for w, c in cols] + [prev(w, c) for w, c in cols]
                + [nxt(w, c) for w, c in cols])
    args = [z] * 12
    consts = [mup, mun, w0, wup, a0, aup, gup, kk_w, ka_w, rk_w]
    in_specs += [full(c) for c in consts]
    args += consts
    if has_vmix:
        vfirst, v0, vdown, vup = vmix
        in_specs += [pl.BlockSpec((tm, RW_WIDTH), lambda i: (i, 0)), full(v0), full(vdown), full(vup)]
        args += [vfirst, v0, vdown, vup]

    one = jax.ShapeDtypeStruct((s, RW_WIDTH), F32)
    two = jax.ShapeDtypeStruct((2, s, RW_WIDTH), F32)
    spec1 = pl.BlockSpec((tm, RW_WIDTH), lambda i: (i, 0))
    spec2 = pl.BlockSpec((2, tm, RW_WIDTH), lambda i: (0, i, 0))
    return pl.pallas_call(
        functools.partial(_rwprep_kernel, has_vmix),
        out_shape=[one, one, one, two, two, two, one, one],
        grid=(s // tm,),
        in_specs=in_specs,
        out_specs=[spec1, spec1, spec1, spec2, spec2, spec2, spec1, spec1],
        compiler_params=_cparams(("parallel",)),
        name="rwkv_prep",
    )(*args)


def _rwrec_kernel(r_ref, v_ref, kk_ref, kd_ref, lw_ref, ba_ref, y_ref, s_ref):
    d = pl.program_id(0)
    T = CHUNK

    @pl.when(pl.program_id(1) == 0)
    def _():
        s_ref[...] = jnp.zeros_like(s_ref)

    sgn = 1 - 2 * d
    row = lax.broadcasted_iota(jnp.int32, (T, T), 0)
    col = lax.broadcasted_iota(jnp.int32, (T, T), 1)
    tri = jnp.where((row - col) * sgn >= 0, 1.0, 0.0).astype(BF16)

    lw = lw_ref[...]
    hi, mid, lo = _split3(lw)
    cin = (jnp.dot(tri, hi, preferred_element_type=F32)
           + jnp.dot(tri, mid, preferred_element_type=F32)
           + jnp.dot(tri, lo, preferred_element_type=F32))
    ctot = jnp.sum(lw, axis=0, keepdims=True)
    e_in = jnp.exp(cin)
    e_ex = jnp.exp(cin - lw)
    e_neg = jnp.exp(-cin)
    e_end = jnp.exp(ctot - cin)
    e_tot = jnp.exp(ctot)

    kd = kd_ref[...]
    ba = ba_ref[...]
    r_t = r_ref[...] * e_in
    a_t = -kk_ref[...] * e_ex
    b_t = ba * e_neg
    k_t = kd * e_neg
    b_e = ba * e_end
    k_e = kd * e_end
    v = v_ref[...]

    R2 = lax.broadcasted_iota(jnp.int32, (2 * T, 2 * T), 0)
    C2 = lax.broadcasted_iota(jnp.int32, (2 * T, 2 * T), 1)
    same = (R2 // T) == (C2 // T)
    diff = ((R2 % T) - (C2 % T)) * sgn
    mask_a = same & (diff > 0)
    mask_y = same & (diff >= 0)
    mask_d = same & (((R2 % T) // SUB) == ((C2 % T) // SUB))
    eye = R2 == C2
    head0 = lax.broadcasted_iota(jnp.int32, (T, LANES), 1) < RW_HEAD

    def stack_heads(x):
        return jnp.concatenate([jnp.where(head0, x, 0.0), jnp.where(head0, 0.0, x)], axis=0)

    def stack_dup(x):
        return jnp.concatenate([x, x], axis=0)

    slabs = range(RW_WIDTH // LANES)
    cols = [slice(sl * LANES, (sl + 1) * LANES) for sl in slabs]
    T2 = 2 * T
    ar2 = [jnp.concatenate([stack_heads(a_t[:, cs]), stack_heads(r_t[:, cs])], axis=0).astype(BF16)
           for cs in cols]
    bk2 = [jnp.concatenate([stack_dup(b_t[:, cs]), stack_dup(k_t[:, cs])], axis=0).astype(BF16)
           for cs in cols]
    v2 = [stack_heads(v[:, cs]).astype(BF16) for cs in cols]
    bke2 = [jnp.concatenate([stack_heads(b_e[:, cs]), stack_heads(k_e[:, cs])], axis=0).astype(BF16)
            for cs in cols]

    sc = [_dot_nt(ar2[s], bk2[s]) for s in slabs]
    n_ab = [jnp.where(mask_a, sc[s][:T2, :T2], 0.0) for s in slabs]
    n_ak = [jnp.where(mask_a, sc[s][:T2, T2:], 0.0) for s in slabs]
    m_rbk = [jnp.concatenate([jnp.where(mask_y, sc[s][T2:, :T2], 0.0),
                              jnp.where(mask_y, sc[s][T2:, T2:], 0.0)], axis=1).astype(BF16)
             for s in slabs]

    p = [jnp.where(mask_d, n_ab[s], 0.0) for s in slabs]
    q = [jnp.where(eye, 1.0, p[s]) for s in slabs]
    for _ in range(3):
        p = [_dot(p[s], p[s]) for s in slabs]
        q = [q[s] + _dot(q[s], p[s]) for s in slabs]
    m = [_dot(q[s], jnp.where(mask_d, 0.0, n_ab[s])) for s in slabs]
    m2 = [_dot(m[s], m[s]) for s in slabs]
    akv = [_dot(n_ak[s], v2[s]) for s in slabs]

    st = [s_ref[s] for s in slabs]
    ars = [_dot_nt(ar2[s], st[s]) for s in slabs]
    x = [_dot(q[s], ars[s][:T2] + akv[s]) for s in slabs]
    x = [x[s] + _dot(m[s], x[s]) for s in slabs]
    x = [x[s] + _dot(m2[s], x[s]) for s in slabs]
    for s in slabs:
        y2 = ars[s][T2:] + _dot(m_rbk[s], jnp.concatenate([x[s].astype(BF16), v2[s]], axis=0))
        y_ref[:, cols[s]] = y2[:T] + y2[T:]
    for s in slabs:
        upd = _dot(jnp.concatenate([x[s].T, v2[s].astype(F32).T], axis=1), bke2[s])
        s_ref[s] = st[s] * e_tot[:, cols[s]] + upd


def _rwrec(r, v, kk, kd, lw, ba):
    s = r.shape[0]
    nc = s // CHUNK

    def cidx(d, c):
        return c + d * (nc - 1 - 2 * c)

    spec1 = pl.BlockSpec((CHUNK, RW_WIDTH), lambda d, c: (cidx(d, c), 0))
    spec2 = pl.BlockSpec((None, CHUNK, RW_WIDTH), lambda d, c: (d, cidx(d, c), 0))
    return pl.pallas_call(
        _rwrec_kernel,
        out_shape=jax.ShapeDtypeStruct((2, s, RW_WIDTH), F32),
        grid=(2, nc),
        in_specs=[spec1, spec1, spec1, spec2, spec2, spec2],
        out_specs=spec2,
        scratch_shapes=[pltpu.VMEM((RW_WIDTH // LANES, LANES, LANES), F32)],
        compiler_params=_cparams(("arbitrary", "arbitrary")),
        name="rwkv_recurrence",
    )(r, v, kk, kd, lw, ba)


def _atprep_kernel(zq, zkv, cos_ref, sin_ref, qn, kn, qt_o, k_o, vt_o):
    ones = _head_ones()
    cos = cos_ref[...]
    sin = sin_ref[...]
    lane = lax.broadcasted_iota(jnp.int32, cos.shape, 1)
    even = ((lane // 16) % 2) == 0

    def norm_rope(t, gain):
        ms = _head_sum(t * t, ones) * (1.0 / AT_HEAD)
        tn = t * lax.rsqrt(ms + EPS) * gain
        swapped = jnp.where(even, pltpu.roll(tn, LANES - 16, 1), pltpu.roll(tn, 16, 1))
        return tn * cos + swapped * sin

    for sl in range(AT_Q // LANES):
        out = norm_rope(zq[:, sl * LANES:(sl + 1) * LANES], qn[...]) * (AT_HEAD ** -0.5)
        out_t = out.T.astype(BF16)
        qt_o[2 * sl] = out_t[:AT_HEAD]
        qt_o[2 * sl + 1] = out_t[AT_HEAD:]
    for sl in range(AT_KV // LANES):
        out = norm_rope(zkv[:, sl * LANES:(sl + 1) * LANES], kn[...])
        k_o[2 * sl] = out[:, :AT_HEAD].astype(BF16)
        k_o[2 * sl + 1] = out[:, AT_HEAD:].astype(BF16)
        vt = zkv[:, AT_KV + sl * LANES:AT_KV + (sl + 1) * LANES].T.astype(BF16)
        vt_o[2 * sl] = vt[:AT_HEAD]
        vt_o[2 * sl + 1] = vt[AT_HEAD:]


def _atprep(z, cos, sin, qn, kn, *, tm):
    s = z.shape[0]
    nb = s // tm
    return pl.pallas_call(
        _atprep_kernel,
        out_shape=[jax.ShapeDtypeStruct((AT_HEADS, AT_HEAD, s), BF16),
                   jax.ShapeDtypeStruct((AT_KV_HEADS, nb, tm, AT_HEAD), BF16),
                   jax.ShapeDtypeStruct((AT_KV_HEADS, nb, AT_HEAD, tm), BF16)],
        grid=(nb,),
        in_specs=[pl.BlockSpec((tm, AT_Q), lambda i: (i, COL_Q // AT_Q)),
                  pl.BlockSpec((tm, 2 * AT_KV), lambda i: (i, COL_ATKV // (2 * AT_KV))),
                  pl.BlockSpec((tm, LANES), lambda i: (i, 0)),
                  pl.BlockSpec((tm, LANES), lambda i: (i, 0)),
                  pl.BlockSpec((1, LANES), lambda i: (0, 0)),
                  pl.BlockSpec((1, LANES), lambda i: (0, 0))],
        out_specs=[pl.BlockSpec((AT_HEADS, AT_HEAD, tm), lambda i: (0, 0, i)),
                   pl.BlockSpec((AT_KV_HEADS, None, tm, AT_HEAD), lambda i: (0, i, 0, 0)),
                   pl.BlockSpec((AT_KV_HEADS, None, AT_HEAD, tm), lambda i: (0, i, 0, 0))],
        compiler_params=_cparams(("parallel",)),
        name="attn_prep",
    )(z, z, cos, sin, qn, kn)


def _flash_kernel(qt_ref, k_ref, vt_ref, o_ref, m_ref, l_ref, acc_ref):
    grp = qt_ref.shape[0]
    nkb = k_ref.shape[0]
    m_ref[...] = jnp.full_like(m_ref, -jnp.inf)
    l_ref[...] = jnp.zeros_like(l_ref)
    acc_ref[...] = jnp.zeros_like(acc_ref)

    def body(kb, carry):
        k = k_ref[kb]
        vt = vt_ref[kb]
        sts = [jnp.dot(k, qt_ref[g], preferred_element_type=F32) for g in range(grp)]
        pts, alphas = [], []
        for g in range(grp):
            m_prev = m_ref[g]
            m_new = jnp.maximum(m_prev, jnp.max(sts[g], axis=0, keepdims=True))
            alpha = jnp.exp(m_prev - m_new)
            pt = jnp.exp(sts[g] - m_new)
            l_ref[g] = alpha * l_ref[g] + jnp.sum(pt, axis=0, keepdims=True)
            m_ref[g] = m_new
            pts.append(pt.astype(BF16))
            alphas.append(alpha)
        for g in range(grp):
            acc_ref[g] = alphas[g] * acc_ref[g] + jnp.dot(vt, pts[g], preferred_element_type=F32)
        return carry

    lax.fori_loop(0, nkb, body, 0)
    for pair in range(grp // 2):
        o2 = jnp.concatenate([acc_ref[2 * pair] / l_ref[2 * pair],
                              acc_ref[2 * pair + 1] / l_ref[2 * pair + 1]], axis=0)
        o_ref[:, pair * LANES:(pair + 1) * LANES] = o2.T.astype(o_ref.dtype)


def _flash(qt, k, vt, *, tq):
    s = qt.shape[2]
    _, nkb, tk, _ = k.shape
    grp = AT_HEADS // AT_KV_HEADS
    return pl.pallas_call(
        _flash_kernel,
        out_shape=jax.ShapeDtypeStruct((s, AT_Q), BF16),
        grid=(AT_KV_HEADS, s // tq),
        in_specs=[pl.BlockSpec((grp, AT_HEAD, tq), lambda h, i: (h, 0, i)),
                  pl.BlockSpec((None, nkb, tk, AT_HEAD), lambda h, i: (h, 0, 0, 0)),
                  pl.BlockSpec((None, nkb, AT_HEAD, tk), lambda h, i: (h, 0, 0, 0))],
        out_specs=pl.BlockSpec((tq, grp * AT_HEAD), lambda h, i: (i, h)),
        scratch_shapes=[pltpu.VMEM((grp, 1, tq), F32), pltpu.VMEM((grp, 1, tq), F32),
                        pltpu.VMEM((grp, AT_HEAD, tq), F32)],
        compiler_params=_cparams(("parallel", "parallel")),
        name="flash_attention",
    )(qt, k, vt)


def _mix_out_kernel(y_ref, g_ref, bon_ref, yat_ref, zg1_ref, zg2_ref, x_ref, lnw, lnb,
                    wrw, wat, wo, npost, o_ref):
    ones = _head_ones()
    y = y_ref[0] + y_ref[1]
    inv = 1.0 / RW_HEAD
    mu = _head_sum(y, ones) * inv
    dlt = y - mu
    var = _head_sum(dlt * dlt, ones) * inv
    yn = dlt * lax.rsqrt(var + RW_GN_EPS) * lnw[...] + lnb[...]
    y_rw = (yn + bon_ref[...]) * g_ref[...]
    pa = _dot(y_rw, wrw[...])
    pb = jnp.dot(yat_ref[...], wat[...], preferred_element_type=F32)
    merged = _sigmoid(zg1_ref[...]) * pa + _sigmoid(zg2_ref[...]) * pb
    out = _dot(merged, wo[...])
    o_ref[...] = x_ref[...] + _rms(out, npost[...])


def _mix_out(y, g, bon, yat, z, x, lnw, lnb, wrw, wat, wo, npost, *, tm):
    s = x.shape[0]

    def const(arr):
        nd = arr.ndim
        return pl.BlockSpec(arr.shape, lambda i: (0,) * nd, pipeline_mode=pl.Buffered(1))

    row1k = pl.BlockSpec((tm, RW_WIDTH), lambda i: (i, 0))
    return pl.pallas_call(
        _mix_out_kernel,
        out_shape=jax.ShapeDtypeStruct((s, D_MODEL), F32),
        grid=(s // tm,),
        in_specs=[pl.BlockSpec((2, tm, RW_WIDTH), lambda i: (0, i, 0)), row1k, row1k, row1k,
                  pl.BlockSpec((tm, D_MODEL), lambda i: (i, COL_G1 // D_MODEL)),
                  pl.BlockSpec((tm, D_MODEL), lambda i: (i, COL_G2 // D_MODEL)),
                  pl.BlockSpec((tm, D_MODEL), lambda i: (i, 0)),
                  const(lnw), const(lnb), const(wrw), const(wat), const(wo), const(npost)],
        out_specs=pl.BlockSpec((tm, D_MODEL), lambda i: (i, 0)),
        compiler_params=_cparams(("parallel",)),
        name="mix_out",
    )(y, g, bon, yat, z, z, x, lnw, lnb, wrw, wat, wo, npost)


def _xattn_kernel(x_ref, kv_ref, npre, wq, wo, npost, o_ref):
    x = x_ref[...]
    h = _rms(x, npre[...])
    q = _dot(h, wq[...]) * (X_HEAD ** -0.5)
    outs = []
    for hd in range(X_HEADS):
        kh = kv_ref[:, hd * X_HEAD:(hd + 1) * X_HEAD]
        vh = kv_ref[:, X_WIDTH + hd * X_HEAD:X_WIDTH + (hd + 1) * X_HEAD]
        sc = _dot_nt(q[:, hd * X_HEAD:(hd + 1) * X_HEAD], kh)
        sc = sc - jnp.max(sc, axis=-1, keepdims=True)
        p = jnp.exp(sc)
        p = p / jnp.sum(p, axis=-1, keepdims=True)
        outs.append(_dot(p, vh))
    o = jnp.concatenate(outs, axis=1)
    c = _dot(o, wo[...])
    o_ref[...] = x + _rms(c, npost[...])


def _xattn(x, kv, npre, wq, wo, npost, *, tm):
    s = x.shape[0]

    def const(arr):
        nd = arr.ndim
        return pl.BlockSpec(arr.shape, lambda i: (0,) * nd)

    return pl.pallas_call(
        _xattn_kernel,
        out_shape=jax.ShapeDtypeStruct((s, D_MODEL), F32),
        grid=(s // tm,),
        in_specs=[pl.BlockSpec((tm, D_MODEL), lambda i: (i, 0)), const(kv), const(npre),
                  const(wq), const(wo), const(npost)],
        out_specs=pl.BlockSpec((tm, D_MODEL), lambda i: (i, 0)),
        compiler_params=_cparams(("parallel",)),
        name="cross_attention",
    )(x, kv, npre, wq, wo, npost)


def _ffn_kernel(x_ref, npre, wg, wu, wd, npost, o_ref, h_ref, acc_ref):
    j = pl.program_id(1)

    @pl.when(j == 0)
    def _():
        h_ref[...] = _rms(x_ref[...], npre[...]).astype(BF16)
        acc_ref[...] = jnp.zeros_like(acc_ref)

    h = h_ref[...]
    gt = jnp.dot(h, wg[...], preferred_element_type=F32)
    ut = jnp.dot(h, wu[...], preferred_element_type=F32)
    act = gt * _sigmoid(gt) * ut
    acc_ref[...] += _dot(act, wd[...])

    @pl.when(j == pl.num_programs(1) - 1)
    def _():
        o_ref[...] = x_ref[...] + _rms(acc_ref[...], npost[...])


def _ffn(x, npre, wg, wu, wd, npost, *, tm, tf):
    s = x.shape[0]
    f = wg.shape[1]
    return pl.pallas_call(
        _ffn_kernel,
        out_shape=jax.ShapeDtypeStruct((s, D_MODEL), F32),
        grid=(s // tm, f // tf),
        in_specs=[pl.BlockSpec((tm, D_MODEL), lambda i, j: (i, 0)),
                  pl.BlockSpec((1, D_MODEL), lambda i, j: (0, 0)),
                  pl.BlockSpec((D_MODEL, tf), lambda i, j: (0, j)),
                  pl.BlockSpec((D_MODEL, tf), lambda i, j: (0, j)),
                  pl.BlockSpec((tf, D_MODEL), lambda i, j: (j, 0)),
                  pl.BlockSpec((1, D_MODEL), lambda i, j: (0, 0))],
        out_specs=pl.BlockSpec((tm, D_MODEL), lambda i, j: (i, 0)),
        scratch_shapes=[pltpu.VMEM((tm, D_MODEL), BF16), pltpu.VMEM((tm, D_MODEL), F32)],
        compiler_params=_cparams(("parallel", "arbitrary")),
        name="swiglu",
    )(x, npre, wg, wu, wd, npost)


def _pack_w_in(w):
    o_lora = 3 * RW_WIDTH
    o_gd = o_lora + 2 * W_LORA + 2 * A_LORA
    o_at = o_gd + G_LORA
    o_gate = o_at + AT_Q + 2 * AT_KV
    d = w.shape[0]
    parts = [w[:, o_gate:o_gate + 2 * D_MODEL],
             w[:, 0:3 * RW_WIDTH],
             w[:, o_at:o_at + AT_Q],
             w[:, o_lora:o_at], jnp.zeros((d, G_PAD - G_LORA), w.dtype),
             w[:, o_at + AT_Q:o_gate]]
    return jnp.concatenate(parts, axis=1).astype(BF16)


def _pack_mu(mu):
    return jnp.concatenate([mu, jnp.zeros((G_PAD - G_LORA,), mu.dtype)])[None, :]


def _block_diag_up(up):
    z = jnp.zeros_like(up[0])
    return jnp.concatenate([jnp.concatenate([up[0], z], axis=1),
                            jnp.concatenate([z, up[1]], axis=1)], axis=0).astype(BF16)


def _rope_tables(s):
    rows = s // GRID_W
    row = jnp.repeat(jnp.arange(rows), GRID_W).astype(F32)
    col = jnp.tile(jnp.arange(GRID_W), rows).astype(F32)
    n_freq = AT_HEAD // 4
    inv = ROPE_THETA ** (-jnp.arange(n_freq, dtype=F32) / n_freq)
    ar = row[:, None] * inv
    ac = col[:, None] * inv
    cos = jnp.concatenate([jnp.cos(ar), jnp.cos(ar), jnp.cos(ac), jnp.cos(ac)], axis=1)
    sin = jnp.concatenate([-jnp.sin(ar), jnp.sin(ar), -jnp.sin(ac), jnp.sin(ac)], axis=1)
    return jnp.tile(cos, (1, 2)), jnp.tile(sin, (1, 2))


def kernel(x, mem, n_mix_pre, n_mix_post, n_x_pre, n_x_post, n_ffn_pre, n_ffn_post, n_mem, w_in, rw_mu_prev, rw_mu_next, rw_w0, rw_w_up, rw_a0, rw_a_up, rw_g_up, rw_v0, rw_v_down, rw_v_up, rw_k_k, rw_k_a, rw_r_k, rw_ln_w, rw_ln_b, w_rw_out, at_q_norm, at_k_norm, w_at_out, w_o, x_wq, x_wkv, x_wo, ffn_wg, ffn_wu, ffn_wd):
    b, s, d = x.shape
    assert b == 1 and d == D_MODEL and s % 512 == 0
    xs = x[0]
    mems = mem[0]
    cos, sin = _rope_tables(s)
    row = lambda t: t[None, :]
    v_first = None
    for l in range(DEPTH):
        z = _norm_proj(xs, row(n_mix_pre[l]), _pack_w_in(w_in[l]), tm=512, tn=1024, out_dtype=F32)

        vmix = None
        if l > 0:
            vmix = (v_first, row(rw_v0[l - 1]), rw_v_down[l - 1].astype(BF16), rw_v_up[l - 1].astype(BF16))
        gup = jnp.concatenate([rw_g_up[l], jnp.zeros((G_PAD - G_LORA, RW_WIDTH), F32)], axis=0).astype(BF16)
        r, v, kk, kd, lw, ba, g, bon = _rwprep(
            z, _pack_mu(rw_mu_prev[l]), _pack_mu(rw_mu_next[l]),
            rw_w0[l].reshape(1, 2 * RW_WIDTH), _block_diag_up(rw_w_up[l]),
            rw_a0[l].reshape(1, 2 * RW_WIDTH), _block_diag_up(rw_a_up[l]), gup,
            row(rw_k_k[l]), row(rw_k_a[l]), rw_r_k[l].reshape(1, RW_WIDTH), vmix, tm=256)
        if l == 0:
            v_first = v
        y = _rwrec(r, v, kk, kd, lw, ba)

        qn = jnp.tile(at_q_norm[l], 2)[None, :]
        kn = jnp.tile(at_k_norm[l], 2)[None, :]
        q_t, k_blk, v_t = _atprep(z, cos, sin, qn, kn, tm=512)
        y_at = _flash(q_t, k_blk, v_t, tq=256)

        xs = _mix_out(y, g, bon, y_at, z, xs, row(rw_ln_w[l]), row(rw_ln_b[l]),
                      w_rw_out[l].astype(BF16), w_at_out[l].astype(BF16), w_o[l].astype(BF16),
                      row(n_mix_post[l]), tm=256)

        kv = _norm_proj(mems, row(n_mem[l]), x_wkv[l].astype(BF16), tm=mems.shape[0], tn=512, out_dtype=BF16)
        xs = _xattn(xs, kv, row(n_x_pre[l]), x_wq[l].astype(BF16), x_wo[l].astype(BF16),
                    row(n_x_post[l]), tm=512)

        xs = _ffn(xs, row(n_ffn_pre[l]), ffn_wg[l].astype(BF16), ffn_wu[l].astype(BF16),
                  ffn_wd[l].astype(BF16), row(n_ffn_post[l]), tm=512, tf=512)
    return xs[None]
```

```python
import functools

import jax
import jax.numpy as jnp
from jax import lax
from jax.experimental import pallas as pl
from jax.experimental.pallas import tpu as pltpu

F32 = jnp.float32
BF16 = jnp.bfloat16

D_MODEL = 2048
DEPTH = 4
GRID_W = 64
EPS = 1e-6
RW_HEAD = 64
RW_WIDTH = 1024
W_LORA = 64
A_LORA = 64
G_LORA = 160
RW_GN_EPS = 64e-5
AT_HEADS = 16
AT_KV_HEADS = 4
AT_HEAD = 64
AT_Q = AT_HEADS * AT_HEAD
AT_KV = AT_KV_HEADS * AT_HEAD
ROPE_THETA = 10000.0
X_HEADS = 4
X_HEAD = 128
X_WIDTH = X_HEADS * X_HEAD
D_FF = 5632

LANES = 128
SUBLANES = 8
CHUNK = 64
SUB = 16
LOG2E = 1.4426950408889634
V_AUG = AT_HEAD + 16
G_PAD = 256

COL_G1 = 0
COL_G2 = 2048
COL_R = 4096
COL_K = 5120
COL_V = 6144
COL_Q = 7168
COL_LORA = 8192
COL_ATKV = 8704
IN_PACKED = 9216

VMEM_LIMIT = 56 * 1024 * 1024


def _cparams(sem):
    return pltpu.CompilerParams(dimension_semantics=sem, vmem_limit_bytes=VMEM_LIMIT)


def _dot(a, b):
    return jnp.dot(a.astype(BF16), b.astype(BF16), preferred_element_type=F32)


def _dot_nt(a, b):
    return lax.dot_general(a.astype(BF16), b.astype(BF16), (((1,), (1,)), ((), ())),
                           preferred_element_type=F32)


def _split2(x):
    hi = x.astype(BF16)
    lo = (x - hi.astype(F32)).astype(BF16)
    return hi, lo


def _split3(x):
    hi = x.astype(BF16)
    r1 = x - hi.astype(F32)
    mid = r1.astype(BF16)
    lo = (r1 - mid.astype(F32)).astype(BF16)
    return hi, mid, lo


def _sigmoid(x):
    return 1.0 / (1.0 + jnp.exp(-x))


def _softplus(x):
    return jnp.maximum(x, 0.0) + jnp.log(1.0 + jnp.exp(-jnp.abs(x)))


def _head_ones():
    r = lax.broadcasted_iota(jnp.int32, (LANES, LANES), 0) // RW_HEAD
    c = lax.broadcasted_iota(jnp.int32, (LANES, LANES), 1) // RW_HEAD
    return jnp.where(r == c, 1.0, 0.0).astype(BF16)


def _head_sum(x, ones):
    outs = []
    for s in range(x.shape[1] // LANES):
        hi, lo = _split2(x[:, s * LANES:(s + 1) * LANES])
        outs.append(jnp.dot(hi, ones, preferred_element_type=F32)
                    + jnp.dot(lo, ones, preferred_element_type=F32))
    return outs[0] if len(outs) == 1 else jnp.concatenate(outs, axis=1)


def _rms(x, gain):
    ms = jnp.mean(x * x, axis=-1, keepdims=True)
    return x * lax.rsqrt(ms + EPS) * gain


def _norm_proj_kernel(x_ref, g_ref, w_ref, o_ref, h_ref):
    @pl.when(pl.program_id(1) == 0)
    def _():
        h_ref[...] = _rms(x_ref[...], g_ref[...]).astype(BF16)

    o_ref[...] = jnp.dot(h_ref[...], w_ref[...], preferred_element_type=F32).astype(o_ref.dtype)


def _norm_proj(x, gain, w, *, tm, tn, out_dtype):
    m, d = x.shape
    n = w.shape[1]
    return pl.pallas_call(
        _norm_proj_kernel,
        out_shape=jax.ShapeDtypeStruct((m, n), out_dtype),
        grid=(m // tm, n // tn),
        in_specs=[pl.BlockSpec((tm, d), lambda i, j: (i, 0)),
                  pl.BlockSpec((1, d), lambda i, j: (0, 0)),
                  pl.BlockSpec((d, tn), lambda i, j: (0, j))],
        out_specs=pl.BlockSpec((tm, tn), lambda i, j: (i, j)),
        scratch_shapes=[pltpu.VMEM((tm, d), BF16)],
        compiler_params=_cparams(("parallel", "arbitrary")),
        name="norm_proj",
    )(x, gain, w)


def _shift(z, prev_row, next_row, mu_p, mu_n):
    rows = z.shape[0]
    ridx = lax.broadcasted_iota(jnp.int32, z.shape, 0)
    zp = jnp.where(ridx == 0, prev_row, pltpu.roll(z, 1, 0))
    zn = jnp.where(ridx == rows - 1, next_row, pltpu.roll(z, rows - 1, 0))
    return z + mu_p * (zp - z) + mu_n * (zn - z)


def _rwprep_kernel(has_vmix, *refs):
    (zr, zk, zv, zl, pr, pk, pv, pL, nr, nk, nv, nL, mup, mun, w0, wup, a0, aup, gup,
     kk_w, ka_w, rk_w) = refs[:22]
    pos = 22
    if has_vmix:
        vfirst, v0, vdown, vup = refs[pos:pos + 4]
        pos += 4
    r_o, v_o, kk_o, kd_o, lw_o, ba_o, g_o, bon_o = refs[pos:pos + 8]

    i = pl.program_id(0)
    last = pl.num_programs(0) - 1
    keep_p = jnp.where(i == 0, 0.0, 1.0)
    keep_n = jnp.where(i == last, 0.0, 1.0)

    def shifted(z_ref, p_ref, n_ref, lo, hi):
        p_row = p_ref[SUBLANES - 1:SUBLANES, :] * keep_p
        n_row = n_ref[0:1, :] * keep_n
        return _shift(z_ref[...], p_row, n_row, mup[:, lo:hi], mun[:, lo:hi])

    r = shifted(zr, pr, nr, 0, 1024)
    k = shifted(zk, pk, nk, 1024, 2048)
    v = shifted(zv, pv, nv, 2048, 3072)
    lora = shifted(zl, pL, nL, 3072, 3584)

    if has_vmix:
        mix = _sigmoid(v0[...] + _dot(_dot(v, vdown[...]), vup[...]))
        v = v + (vfirst[...] - v) * mix

    u = w0[...] + _dot(jnp.tanh(lora[:, 0:128]), wup[...])
    w_log = -_softplus(-u) - 0.5
    lw = -jnp.exp(w_log)
    a = _sigmoid(a0[...] + _dot(lora[:, 128:256], aup[...]))
    g = _dot(_sigmoid(lora[:, 256:512]), gup[...])

    ones = _head_ones()
    kk = k * kk_w[...]
    norm = jnp.sqrt(_head_sum(kk * kk, ones))
    kk = kk / jnp.maximum(norm, 1e-12)
    ka = ka_w[...]
    a_f = a[:, :RW_WIDTH]
    a_b = a[:, RW_WIDTH:]
    kd_f = k * (1.0 + (a_f - 1.0) * ka)
    kd_b = k * (1.0 + (a_b - 1.0) * ka)
    bonus = _head_sum(r * rk_w[...] * (kd_f + kd_b), ones) * v

    r_o[...] = r
    v_o[...] = v
    kk_o[...] = kk
    kd_o[0] = kd_f
    kd_o[1] = kd_b
    lw_o[0] = lw[:, :RW_WIDTH]
    lw_o[1] = lw[:, RW_WIDTH:]
    ba_o[0] = kk * a_f
    ba_o[1] = kk * a_b
    g_o[...] = g
    bon_o[...] = bonus


def _rwprep(z, mup, mun, w0, wup, a0, aup, gup, kk_w, ka_w, rk_w, vmix, *, tm):
    s = z.shape[0]
    nb8 = s // SUBLANES
    tb = tm // SUBLANES
    has_vmix = vmix is not None

    def main(width, cb):
        return pl.BlockSpec((tm, width), lambda i: (i, cb))

    def prev(width, cb):
        return pl.BlockSpec((SUBLANES, width), lambda i: (jnp.maximum(i * tb - 1, 0), cb))

    def nxt(width, cb):
        return pl.BlockSpec((SUBLANES, width), lambda i: (jnp.minimum((i + 1) * tb, nb8 - 1), cb))

    def full(arr):
        nd = arr.ndim
        return pl.BlockSpec(arr.shape, lambda i: (0,) * nd)

    cols = [(1024, COL_R // 1024), (1024, COL_K // 1024), (1024, COL_V // 1024), (512, COL_LORA // 512)]
    in_specs = ([main(w, c) for w, c in cols] + [prev(w, c) for w, c in cols]
                + [nxt(w, c) for w, c in cols])
    args = [z] * 12
    consts = [mup, mun, w0, wup, a0, aup, gup, kk_w, ka_w, rk_w]
    in_specs += [full(c) for c in consts]
    args += consts
    if has_vmix:
        vfirst, v0, vdown, vup = vmix
        in_specs += [pl.BlockSpec((tm, RW_WIDTH), lambda i: (i, 0)), full(v0), full(vdown), full(vup)]
        args += [vfirst, v0, vdown, vup]

    one = jax.ShapeDtypeStruct((s, RW_WIDTH), F32)
    two = jax.ShapeDtypeStruct((2, s, RW_WIDTH), F32)
    spec1 = pl.BlockSpec((tm, RW_WIDTH), lambda i: (i, 0))
    spec2 = pl.BlockSpec((2, tm, RW_WIDTH), lambda i: (0, i, 0))
    return pl.pallas_call(
        functools.partial(_rwprep_kernel, has_vmix),
        out_shape=[one, one, one, two, two, two, one, one],
        grid=(s // tm,),
        in_specs=in_specs,
        out_specs=[spec1, spec1, spec1, spec2, spec2, spec2, spec1, spec1],
        compiler_params=_cparams(("parallel",)),
        name="rwkv_prep",
    )(*args)


def _rwrec_kernel(r_ref, v_ref, kk_ref, kd_ref, lw_ref, ba_ref, y_ref, s_ref):
    d = pl.program_id(0)
    T = CHUNK

    @pl.when(pl.program_id(1) == 0)
    def _():
        s_ref[...] = jnp.zeros_like(s_ref)

    sgn = 1 - 2 * d
    row = lax.broadcasted_iota(jnp.int32, (T, T), 0)
    col = lax.broadcasted_iota(jnp.int32, (T, T), 1)
    tri = jnp.where((row - col) * sgn >= 0, 1.0, 0.0).astype(BF16)

    lw = lw_ref[...]
    hi, mid, lo = _split3(lw)
    cin = (jnp.dot(tri, hi, preferred_element_type=F32)
           + jnp.dot(tri, mid, preferred_element_type=F32)
           + jnp.dot(tri, lo, preferred_element_type=F32))
    ctot = jnp.sum(lw, axis=0, keepdims=True)
    e_in = jnp.exp(cin)
    e_ex = jnp.exp(cin - lw)
    e_neg = jnp.exp(-cin)
    e_end = jnp.exp(ctot - cin)
    e_tot = jnp.exp(ctot)

    kd = kd_ref[...]
    ba = ba_ref[...]
    r_t = r_ref[...] * e_in
    a_t = -kk_ref[...] * e_ex
    b_t = ba * e_neg
    k_t = kd * e_neg
    b_e = ba * e_end
    k_e = kd * e_end
    v = v_ref[...]

    R2 = lax.broadcasted_iota(jnp.int32, (2 * T, 2 * T), 0)
    C2 = lax.broadcasted_iota(jnp.int32, (2 * T, 2 * T), 1)
    same = (R2 // T) == (C2 // T)
    diff = ((R2 % T) - (C2 % T)) * sgn
    mask_a = same & (diff > 0)
    mask_y = same & (diff >= 0)
    mask_d = same & (((R2 % T) // SUB) == ((C2 % T) // SUB))
    eye = R2 == C2
    head0 = lax.broadcasted_iota(jnp.int32, (T, LANES), 1) < RW_HEAD

    def stack_heads(x):
        return jnp.concatenate([jnp.where(head0, x, 0.0), jnp.where(head0, 0.0, x)], axis=0)

    def stack_dup(x):
        return jnp.concatenate([x, x], axis=0)

    slabs = range(RW_WIDTH // LANES)
    cols = [slice(sl * LANES, (sl + 1) * LANES) for sl in slabs]
    T2 = 2 * T
    ar2 = [jnp.concatenate([stack_heads(a_t[:, cs]), stack_heads(r_t[:, cs])], axis=0).astype(BF16)
           for cs in cols]
    bk2 = [jnp.concatenate([stack_dup(b_t[:, cs]), stack_dup(k_t[:, cs])], axis=0).astype(BF16)
           for cs in cols]
    v2 = [stack_heads(v[:, cs]).astype(BF16) for cs in cols]
    bke2 = [jnp.concatenate([stack_heads(b_e[:, cs]), stack_heads(k_e[:, cs])], axis=0).astype(BF16)
            for cs in cols]

    sc = [_dot_nt(ar2[s], bk2[s]) for s in slabs]
    n_ab = [jnp.where(mask_a, sc[s][:T2, :T2], 0.0) for s in slabs]
    n_ak = [jnp.where(mask_a, sc[s][:T2, T2:], 0.0) for s in slabs]
    m_rbk = [jnp.concatenate([jnp.where(mask_y, sc[s][T2:, :T2], 0.0),
                              jnp.where(mask_y, sc[s][T2:, T2:], 0.0)], axis=1).astype(BF16)
             for s in slabs]

    p = [jnp.where(mask_d, n_ab[s], 0.0) for s in slabs]
    q = [jnp.where(eye, 1.0, p[s]) for s in slabs]
    for _ in range(3):
        p = [_dot(p[s], p[s]) for s in slabs]
        q = [q[s] + _dot(q[s], p[s]) for s in slabs]
    m = [_dot(q[s], jnp.where(mask_d, 0.0, n_ab[s])) for s in slabs]
    m2 = [_dot(m[s], m[s]) for s in slabs]
    akv = [_dot(n_ak[s], v2[s]) for s in slabs]

    st = [s_ref[s] for s in slabs]
    ars = [_dot_nt(ar2[s], st[s]) for s in slabs]
    x = [_dot(q[s], ars[s][:T2] + akv[s]) for s in slabs]
    x = [x[s] + _dot(m[s], x[s]) for s in slabs]
    x = [x[s] + _dot(m2[s], x[s]) for s in slabs]
    for s in slabs:
        y2 = ars[s][T2:] + _dot(m_rbk[s], jnp.concatenate([x[s].astype(BF16), v2[s]], axis=0))
        y_ref[:, cols[s]] = y2[:T] + y2[T:]
    for s in slabs:
        upd = _dot(jnp.concatenate([x[s].T, v2[s].astype(F32).T], axis=1), bke2[s])
        s_ref[s] = st[s] * e_tot[:, cols[s]] + upd


def _rwrec(r, v, kk, kd, lw, ba):
    s = r.shape[0]
    nc = s // CHUNK

    def cidx(d, c):
        return c + d * (nc - 1 - 2 * c)

    spec1 = pl.BlockSpec((CHUNK, RW_WIDTH), lambda d, c: (cidx(d, c), 0))
    spec2 = pl.BlockSpec((None, CHUNK, RW_WIDTH), lambda d, c: (d, cidx(d, c), 0))
    return pl.pallas_call(
        _rwrec_kernel,
        out_shape=jax.ShapeDtypeStruct((2, s, RW_WIDTH), F32),
        grid=(2, nc),
        in_specs=[spec1, spec1, spec1, spec2, spec2, spec2],
        out_specs=spec2,
        scratch_shapes=[pltpu.VMEM((RW_WIDTH // LANES, LANES, LANES), F32)],
        compiler_params=_cparams(("arbitrary", "arbitrary")),
        name="rwkv_recurrence",
    )(r, v, kk, kd, lw, ba)


def _atprep_kernel(zq, zkv, cos_ref, sin_ref, qn, kn, qt_o, k_o, vt_o):
    ones = _head_ones()
    cos = cos_ref[...]
    sin = sin_ref[...]
    lane = lax.broadcasted_iota(jnp.int32, cos.shape, 1)
    even = ((lane // 16) % 2) == 0

    def norm_rope(t, gain):
        ms = _head_sum(t * t, ones) * (1.0 / AT_HEAD)
        tn = t * lax.rsqrt(ms + EPS) * gain
        swapped = jnp.where(even, pltpu.roll(tn, LANES - 16, 1), pltpu.roll(tn, 16, 1))
        return tn * cos + swapped * sin

    q_scale = (AT_HEAD ** -0.5) * LOG2E
    tm = zq.shape[0]
    ones_rows = jnp.where(lax.broadcasted_iota(jnp.int32, (V_AUG - AT_HEAD, tm), 0) == 0, 1.0, 0.0).astype(BF16)
    for sl in range(AT_Q // LANES):
        out = norm_rope(zq[:, sl * LANES:(sl + 1) * LANES], qn[...]) * q_scale
        out_t = out.T.astype(BF16)
        qt_o[2 * sl] = out_t[:AT_HEAD]
        qt_o[2 * sl + 1] = out_t[AT_HEAD:]
    for sl in range(AT_KV // LANES):
        out = norm_rope(zkv[:, sl * LANES:(sl + 1) * LANES], kn[...])
        k_o[2 * sl] = out[:, :AT_HEAD].astype(BF16)
        k_o[2 * sl + 1] = out[:, AT_HEAD:].astype(BF16)
        vt = zkv[:, AT_KV + sl * LANES:AT_KV + (sl + 1) * LANES].T.astype(BF16)
        vt_o[2 * sl] = jnp.concatenate([vt[:AT_HEAD], ones_rows], axis=0)
        vt_o[2 * sl + 1] = jnp.concatenate([vt[AT_HEAD:], ones_rows], axis=0)


def _atprep(z, cos, sin, qn, kn, *, tm):
    s = z.shape[0]
    nb = s // tm
    return pl.pallas_call(
        _atprep_kernel,
        out_shape=[jax.ShapeDtypeStruct((AT_HEADS, AT_HEAD, s), BF16),
                   jax.ShapeDtypeStruct((AT_KV_HEADS, nb, tm, AT_HEAD), BF16),
                   jax.ShapeDtypeStruct((AT_KV_HEADS, nb, V_AUG, tm), BF16)],
        grid=(nb,),
        in_specs=[pl.BlockSpec((tm, AT_Q), lambda i: (i, COL_Q // AT_Q)),
                  pl.BlockSpec((tm, 2 * AT_KV), lambda i: (i, COL_ATKV // (2 * AT_KV))),
                  pl.BlockSpec((tm, LANES), lambda i: (i, 0)),
                  pl.BlockSpec((tm, LANES), lambda i: (i, 0)),
                  pl.BlockSpec((1, LANES), lambda i: (0, 0)),
                  pl.BlockSpec((1, LANES), lambda i: (0, 0))],
        out_specs=[pl.BlockSpec((AT_HEADS, AT_HEAD, tm), lambda i: (0, 0, i)),
                   pl.BlockSpec((AT_KV_HEADS, None, tm, AT_HEAD), lambda i: (0, i, 0, 0)),
                   pl.BlockSpec((AT_KV_HEADS, None, V_AUG, tm), lambda i: (0, i, 0, 0))],
        compiler_params=_cparams(("parallel",)),
        name="attn_prep",
    )(z, z, cos, sin, qn, kn)


def _flash_kernel(qt_ref, k_ref, vt_ref, o_ref, m_ref, acc_ref, sa_ref, sb_ref):
    grp = qt_ref.shape[0]
    nkb = k_ref.shape[0]
    m_ref[...] = jnp.full_like(m_ref, -jnp.inf)
    acc_ref[...] = jnp.zeros_like(acc_ref)

    def step(kb_cur, cur_ref, kb_next, next_ref):
        k_next = k_ref[kb_next]
        vt = vt_ref[kb_cur]
        for g in range(grp):
            next_ref[g] = jnp.dot(k_next, qt_ref[g], preferred_element_type=F32)
            st = cur_ref[g]
            m_prev = m_ref[g]
            m_new = jnp.maximum(m_prev, jnp.max(st, axis=0, keepdims=True))
            alpha = jnp.exp2(m_prev - m_new)
            pt = jnp.exp2(st - m_new).astype(BF16)
            m_ref[g] = m_new
            acc_ref[g] = alpha * acc_ref[g] + jnp.dot(vt, pt, preferred_element_type=F32)

    k0 = k_ref[0]
    for g in range(grp):
        sa_ref[g] = jnp.dot(k0, qt_ref[g], preferred_element_type=F32)

    def body(j, carry):
        kb = 2 * j
        step(kb, sa_ref, kb + 1, sb_ref)
        step(kb + 1, sb_ref, jnp.minimum(kb + 2, nkb - 1), sa_ref)
        return carry

    lax.fori_loop(0, nkb // 2, body, 0)
    for pair in range(grp // 2):
        a0 = acc_ref[2 * pair]
        a1 = acc_ref[2 * pair + 1]
        o2 = jnp.concatenate([a0[:AT_HEAD] / a0[AT_HEAD:AT_HEAD + 1],
                              a1[:AT_HEAD] / a1[AT_HEAD:AT_HEAD + 1]], axis=0)
        o_ref[:, pair * LANES:(pair + 1) * LANES] = o2.T.astype(o_ref.dtype)


def _flash(qt, k, vt, *, tq):
    s = qt.shape[2]
    _, nkb, tk, _ = k.shape
    grp = AT_HEADS // AT_KV_HEADS
    return pl.pallas_call(
        _flash_kernel,
        out_shape=jax.ShapeDtypeStruct((s, AT_Q), BF16),
        grid=(AT_KV_HEADS, s // tq),
        in_specs=[pl.BlockSpec((grp, AT_HEAD, tq), lambda h, i: (h, 0, i)),
                  pl.BlockSpec((None, nkb, tk, AT_HEAD), lambda h, i: (h, 0, 0, 0)),
                  pl.BlockSpec((None, nkb, V_AUG, tk), lambda h, i: (h, 0, 0, 0))],
        out_specs=pl.BlockSpec((tq, grp * AT_HEAD), lambda h, i: (i, h)),
        scratch_shapes=[pltpu.VMEM((grp, 1, tq), F32), pltpu.VMEM((grp, V_AUG, tq), F32),
                        pltpu.VMEM((grp, tk, tq), F32), pltpu.VMEM((grp, tk, tq), F32)],
        compiler_params=_cparams(("parallel", "parallel")),
        name="flash_attention",
    )(qt, k, vt)


def _mix_out_kernel(y_ref, g_ref, bon_ref, yat_ref, zg1_ref, zg2_ref, x_ref, lnw, lnb,
                    wrw, wat, wo, npost, o_ref):
    ones = _head_ones()
    y = y_ref[0] + y_ref[1]
    inv = 1.0 / RW_HEAD
    mu = _head_sum(y, ones) * inv
    dlt = y - mu
    var = _head_sum(dlt * dlt, ones) * inv
    yn = dlt * lax.rsqrt(var + RW_GN_EPS) * lnw[...] + lnb[...]
    y_rw = (yn + bon_ref[...]) * g_ref[...]
    pa = _dot(y_rw, wrw[...])
    pb = jnp.dot(yat_ref[...], wat[...], preferred_element_type=F32)
    merged = _sigmoid(zg1_ref[...]) * pa + _sigmoid(zg2_ref[...]) * pb
    out = _dot(merged, wo[...])
    o_ref[...] = x_ref[...] + _rms(out, npost[...])


def _mix_out(y, g, bon, yat, z, x, lnw, lnb, wrw, wat, wo, npost, *, tm):
    s = x.shape[0]

    def const(arr):
        nd = arr.ndim
        return pl.BlockSpec(arr.shape, lambda i: (0,) * nd, pipeline_mode=pl.Buffered(1))

    row1k = pl.BlockSpec((tm, RW_WIDTH), lambda i: (i, 0))
    return pl.pallas_call(
        _mix_out_kernel,
        out_shape=jax.ShapeDtypeStruct((s, D_MODEL), F32),
        grid=(s // tm,),
        in_specs=[pl.BlockSpec((2, tm, RW_WIDTH), lambda i: (0, i, 0)), row1k, row1k, row1k,
                  pl.BlockSpec((tm, D_MODEL), lambda i: (i, COL_G1 // D_MODEL)),
                  pl.BlockSpec((tm, D_MODEL), lambda i: (i, COL_G2 // D_MODEL)),
                  pl.BlockSpec((tm, D_MODEL), lambda i: (i, 0)),
                  const(lnw), const(lnb), const(wrw), const(wat), const(wo), const(npost)],
        out_specs=pl.BlockSpec((tm, D_MODEL), lambda i: (i, 0)),
        compiler_params=_cparams(("parallel",)),
        name="mix_out",
    )(y, g, bon, yat, z, z, x, lnw, lnb, wrw, wat, wo, npost)


def _xattn_kernel(x_ref, kv_ref, npre, wq, wo, npost, o_ref):
    x = x_ref[...]
    h = _rms(x, npre[...])
    q = _dot(h, wq[...]) * (X_HEAD ** -0.5)
    outs = []
    for hd in range(X_HEADS):
        kh = kv_ref[:, hd * X_HEAD:(hd + 1) * X_HEAD]
        vh = kv_ref[:, X_WIDTH + hd * X_HEAD:X_WIDTH + (hd + 1) * X_HEAD]
        sc = _dot_nt(q[:, hd * X_HEAD:(hd + 1) * X_HEAD], kh)
        sc = sc - jnp.max(sc, axis=-1, keepdims=True)
        p = jnp.exp(sc)
        p = p / jnp.sum(p, axis=-1, keepdims=True)
        outs.append(_dot(p, vh))
    o = jnp.concatenate(outs, axis=1)
    c = _dot(o, wo[...])
    o_ref[...] = x + _rms(c, npost[...])


def _xattn(x, kv, npre, wq, wo, npost, *, tm):
    s = x.shape[0]

    def const(arr):
        nd = arr.ndim
        return pl.BlockSpec(arr.shape, lambda i: (0,) * nd)

    return pl.pallas_call(
        _xattn_kernel,
        out_shape=jax.ShapeDtypeStruct((s, D_MODEL), F32),
        grid=(s // tm,),
        in_specs=[pl.BlockSpec((tm, D_MODEL), lambda i: (i, 0)), const(kv), const(npre),
                  const(wq), const(wo), const(npost)],
        out_specs=pl.BlockSpec((tm, D_MODEL), lambda i: (i, 0)),
        compiler_params=_cparams(("parallel",)),
        name="cross_attention",
    )(x, kv, npre, wq, wo, npost)


def _ffn_kernel(x_ref, npre, wg, wu, wd, npost, o_ref, h_ref, acc_ref):
    j = pl.program_id(1)

    @pl.when(j == 0)
    def _():
        h_ref[...] = _rms(x_ref[...], npre[...]).astype(BF16)
        acc_ref[...] = jnp.zeros_like(acc_ref)

    h = h_ref[...]
    gt = jnp.dot(h, wg[...], preferred_element_type=F32)
    ut = jnp.dot(h, wu[...], preferred_element_type=F32)
    act = gt * _sigmoid(gt) * ut
    acc_ref[...] += _dot(act, wd[...])

    @pl.when(j == pl.num_programs(1) - 1)
    def _():
        o_ref[...] = x_ref[...] + _rms(acc_ref[...], npost[...])


def _ffn(x, npre, wg, wu, wd, npost, *, tm, tf):
    s = x.shape[0]
    f = wg.shape[1]
    return pl.pallas_call(
        _ffn_kernel,
        out_shape=jax.ShapeDtypeStruct((s, D_MODEL), F32),
        grid=(s // tm, f // tf),
        in_specs=[pl.BlockSpec((tm, D_MODEL), lambda i, j: (i, 0)),
                  pl.BlockSpec((1, D_MODEL), lambda i, j: (0, 0)),
                  pl.BlockSpec((D_MODEL, tf), lambda i, j: (0, j)),
                  pl.BlockSpec((D_MODEL, tf), lambda i, j: (0, j)),
                  pl.BlockSpec((tf, D_MODEL), lambda i, j: (j, 0)),
                  pl.BlockSpec((1, D_MODEL), lambda i, j: (0, 0))],
        out_specs=pl.BlockSpec((tm, D_MODEL), lambda i, j: (i, 0)),
        scratch_shapes=[pltpu.VMEM((tm, D_MODEL), BF16), pltpu.VMEM((tm, D_MODEL), F32)],
        compiler_params=_cparams(("parallel", "arbitrary")),
        name="swiglu",
    )(x, npre, wg, wu, wd, npost)


def _pack_w_in(w):
    o_lora = 3 * RW_WIDTH
    o_gd = o_lora + 2 * W_LORA + 2 * A_LORA
    o_at = o_gd + G_LORA
    o_gate = o_at + AT_Q + 2 * AT_KV
    d = w.shape[0]
    parts = [w[:, o_gate:o_gate + 2 * D_MODEL],
             w[:, 0:3 * RW_WIDTH],
             w[:, o_at:o_at + AT_Q],
             w[:, o_lora:o_at], jnp.zeros((d, G_PAD - G_LORA), w.dtype),
             w[:, o_at + AT_Q:o_gate]]
    return jnp.concatenate(parts, axis=1).astype(BF16)


def _pack_mu(mu):
    return jnp.concatenate([mu, jnp.zeros((G_PAD - G_LORA,), mu.dtype)])[None, :]


def _block_diag_up(up):
    z = jnp.zeros_like(up[0])
    return jnp.concatenate([jnp.concatenate([up[0], z], axis=1),
                            jnp.concatenate([z, up[1]], axis=1)], axis=0).astype(BF16)


def _rope_tables(s):
    rows = s // GRID_W
    row = jnp.repeat(jnp.arange(rows), GRID_W).astype(F32)
    col = jnp.tile(jnp.arange(GRID_W), rows).astype(F32)
    n_freq = AT_HEAD // 4
    inv = ROPE_THETA ** (-jnp.arange(n_freq, dtype=F32) / n_freq)
    ar = row[:, None] * inv
    ac = col[:, None] * inv
    cos = jnp.concatenate([jnp.cos(ar), jnp.cos(ar), jnp.cos(ac), jnp.cos(ac)], axis=1)
    sin = jnp.concatenate([-jnp.sin(ar), jnp.sin(ar), -jnp.sin(ac), jnp.sin(ac)], axis=1)
    return jnp.tile(cos, (1, 2)), jnp.tile(sin, (1, 2))


def kernel(x, mem, n_mix_pre, n_mix_post, n_x_pre, n_x_post, n_ffn_pre, n_ffn_post, n_mem, w_in, rw_mu_prev, rw_mu_next, rw_w0, rw_w_up, rw_a0, rw_a_up, rw_g_up, rw_v0, rw_v_down, rw_v_up, rw_k_k, rw_k_a, rw_r_k, rw_ln_w, rw_ln_b, w_rw_out, at_q_norm, at_k_norm, w_at_out, w_o, x_wq, x_wkv, x_wo, ffn_wg, ffn_wu, ffn_wd):
    b, s, d = x.shape
    assert b == 1 and d == D_MODEL and s % 512 == 0
    xs = x[0]
    mems = mem[0]
    cos, sin = _rope_tables(s)
    row = lambda t: t[None, :]
    v_first = None
    for l in range(DEPTH):
        z = _norm_proj(xs, row(n_mix_pre[l]), _pack_w_in(w_in[l]), tm=512, tn=1024, out_dtype=F32)

        vmix = None
        if l > 0:
            vmix = (v_first, row(rw_v0[l - 1]), rw_v_down[l - 1].astype(BF16), rw_v_up[l - 1].astype(BF16))
        gup = jnp.concatenate([rw_g_up[l], jnp.zeros((G_PAD - G_LORA, RW_WIDTH), F32)], axis=0).astype(BF16)
        r, v, kk, kd, lw, ba, g, bon = _rwprep(
            z, _pack_mu(rw_mu_prev[l]), _pack_mu(rw_mu_next[l]),
            rw_w0[l].reshape(1, 2 * RW_WIDTH), _block_diag_up(rw_w_up[l]),
            rw_a0[l].reshape(1, 2 * RW_WIDTH), _block_diag_up(rw_a_up[l]), gup,
            row(rw_k_k[l]), row(rw_k_a[l]), rw_r_k[l].reshape(1, RW_WIDTH), vmix, tm=256)
        if l == 0:
            v_first = v
        y = _rwrec(r, v, kk, kd, lw, ba)

        qn = jnp.tile(at_q_norm[l], 2)[None, :]
        kn = jnp.tile(at_k_norm[l], 2)[None, :]
        q_t, k_blk, v_t = _atprep(z, cos, sin, qn, kn, tm=512)
        y_at = _flash(q_t, k_blk, v_t, tq=256)

        xs = _mix_out(y, g, bon, y_at, z, xs, row(rw_ln_w[l]), row(rw_ln_b[l]),
                      w_rw_out[l].astype(BF16), w_at_out[l].astype(BF16), w_o[l].astype(BF16),
                      row(n_mix_post[l]), tm=256)

        kv = _norm_proj(mems, row(n_mem[l]), x_wkv[l].astype(BF16), tm=mems.shape[0], tn=512, out_dtype=BF16)
        xs = _xattn(xs, kv, row(n_x_pre[l]), x_wq[l].astype(BF16), x_wo[l].astype(BF16),
                    row(n_x_post[l]), tm=512)

        xs = _ffn(xs, row(n_ffn_pre[l]), ffn_wg[l].astype(BF16), ffn_wu[l].astype(BF16),
                  ffn_wd[l].astype(BF16), row(n_ffn_post[l]), tm=512, tf=512)
    return xs[None]
```

```python
import functools

import jax
import jax.numpy as jnp
from jax import lax
from jax.experimental import pallas as pl
from jax.experimental.pallas import tpu as pltpu

F32 = jnp.float32
BF16 = jnp.bfloat16

D_MODEL = 2048
DEPTH = 4
GRID_W = 64
EPS = 1e-6
RW_HEAD = 64
RW_WIDTH = 1024
W_LORA = 64
A_LORA = 64
G_LORA = 160
RW_GN_EPS = 64e-5
AT_HEADS = 16
AT_KV_HEADS = 4
AT_HEAD = 64
AT_Q = AT_HEADS * AT_HEAD
AT_KV = AT_KV_HEADS * AT_HEAD
ROPE_THETA = 10000.0
X_HEADS = 4
X_HEAD = 128
X_WIDTH = X_HEADS * X_HEAD
D_FF = 5632

LANES = 128
SUBLANES = 8
CHUNK = 64
SUB = 16
LOG2E = 1.4426950408889634
V_AUG = AT_HEAD + 16
G_PAD = 256

COL_G1 = 0
COL_G2 = 2048
COL_R = 4096
COL_K = 5120
COL_V = 6144
COL_Q = 7168
COL_LORA = 8192
COL_ATKV = 8704
IN_PACKED = 9216

VMEM_LIMIT = 56 * 1024 * 1024


def _cparams(sem):
    return pltpu.CompilerParams(dimension_semantics=sem, vmem_limit_bytes=VMEM_LIMIT)


def _dot(a, b):
    return jnp.dot(a.astype(BF16), b.astype(BF16), preferred_element_type=F32)


def _dot_nt(a, b):
    return lax.dot_general(a.astype(BF16), b.astype(BF16), (((1,), (1,)), ((), ())),
                           preferred_element_type=F32)


def _split2(x):
    hi = x.astype(BF16)
    lo = (x - hi.astype(F32)).astype(BF16)
    return hi, lo


def _split3(x):
    hi = x.astype(BF16)
    r1 = x - hi.astype(F32)
    mid = r1.astype(BF16)
    lo = (r1 - mid.astype(F32)).astype(BF16)
    return hi, mid, lo


def _sigmoid(x):
    return 1.0 / (1.0 + jnp.exp(-x))


def _softplus(x):
    return jnp.maximum(x, 0.0) + jnp.log(1.0 + jnp.exp(-jnp.abs(x)))


def _head_ones():
    r = lax.broadcasted_iota(jnp.int32, (LANES, LANES), 0) // RW_HEAD
    c = lax.broadcasted_iota(jnp.int32, (LANES, LANES), 1) // RW_HEAD
    return jnp.where(r == c, 1.0, 0.0).astype(BF16)


def _head_sum(x, ones):
    outs = []
    for s in range(x.shape[1] // LANES):
        hi, lo = _split2(x[:, s * LANES:(s + 1) * LANES])
        outs.append(jnp.dot(hi, ones, preferred_element_type=F32)
                    + jnp.dot(lo, ones, preferred_element_type=F32))
    return outs[0] if len(outs) == 1 else jnp.concatenate(outs, axis=1)


def _rms(x, gain):
    ms = jnp.mean(x * x, axis=-1, keepdims=True)
    return x * lax.rsqrt(ms + EPS) * gain


def _norm_proj_kernel(x_ref, g_ref, w_ref, o_ref, h_ref):
    @pl.when(pl.program_id(1) == 0)
    def _():
        h_ref[...] = _rms(x_ref[...], g_ref[...]).astype(BF16)

    o_ref[...] = jnp.dot(h_ref[...], w_ref[...], preferred_element_type=F32).astype(o_ref.dtype)


def _norm_proj(x, gain, w, layer, *, tm, tn, out_dtype):
    m, d = x.shape
    n = w.shape[2]
    return pl.pallas_call(
        _norm_proj_kernel,
        out_shape=jax.ShapeDtypeStruct((m, n), out_dtype),
        grid=(m // tm, n // tn),
        in_specs=[pl.BlockSpec((tm, d), lambda i, j: (i, 0)),
                  pl.BlockSpec((1, d), lambda i, j: (0, 0)),
                  pl.BlockSpec((None, d, tn), lambda i, j: (layer, 0, j))],
        out_specs=pl.BlockSpec((tm, tn), lambda i, j: (i, j)),
        scratch_shapes=[pltpu.VMEM((tm, d), BF16)],
        compiler_params=_cparams(("parallel", "arbitrary")),
        name="norm_proj",
    )(x, gain, w)


def _shift(z, prev_row, next_row, mu_p, mu_n):
    rows = z.shape[0]
    ridx = lax.broadcasted_iota(jnp.int32, z.shape, 0)
    zp = jnp.where(ridx == 0, prev_row, pltpu.roll(z, 1, 0))
    zn = jnp.where(ridx == rows - 1, next_row, pltpu.roll(z, rows - 1, 0))
    return z + mu_p * (zp - z) + mu_n * (zn - z)


def _rwprep_kernel(has_vmix, *refs):
    (zr, zk, zv, zl, pr, pk, pv, pL, nr, nk, nv, nL, mup, mun, w0, wup, a0, aup, gup,
     kk_w, ka_w, rk_w) = refs[:22]
    pos = 22
    if has_vmix:
        vfirst, v0, vdown, vup = refs[pos:pos + 4]
        pos += 4
    r_o, v_o, kk_o, kd_o, lw_o, ba_o, g_o, bon_o = refs[pos:pos + 8]

    i = pl.program_id(0)
    last = pl.num_programs(0) - 1
    keep_p = jnp.where(i == 0, 0.0, 1.0)
    keep_n = jnp.where(i == last, 0.0, 1.0)

    def shifted(z_ref, p_ref, n_ref, lo, hi):
        p_row = p_ref[SUBLANES - 1:SUBLANES, :] * keep_p
        n_row = n_ref[0:1, :] * keep_n
        return _shift(z_ref[...], p_row, n_row, mup[:, lo:hi], mun[:, lo:hi])

    r = shifted(zr, pr, nr, 0, 1024)
    k = shifted(zk, pk, nk, 1024, 2048)
    v = shifted(zv, pv, nv, 2048, 3072)
    lora = shifted(zl, pL, nL, 3072, 3584)

    if has_vmix:
        mix = _sigmoid(v0[...] + _dot(_dot(v, vdown[...]), vup[...]))
        v = v + (vfirst[...] - v) * mix

    u = w0[...] + _dot(jnp.tanh(lora[:, 0:128]), wup[...])
    w_log = -_softplus(-u) - 0.5
    lw = -jnp.exp(w_log)
    a = _sigmoid(a0[...] + _dot(lora[:, 128:256], aup[...]))
    g = _dot(_sigmoid(lora[:, 256:512]), gup[...])

    ones = _head_ones()
    kk = k * kk_w[...]
    norm = jnp.sqrt(_head_sum(kk * kk, ones))
    kk = kk / jnp.maximum(norm, 1e-12)
    ka = ka_w[...]
    a_f = a[:, :RW_WIDTH]
    a_b = a[:, RW_WIDTH:]
    kd_f = k * (1.0 + (a_f - 1.0) * ka)
    kd_b = k * (1.0 + (a_b - 1.0) * ka)
    bonus = _head_sum(r * rk_w[...] * (kd_f + kd_b), ones) * v

    r_o[...] = r
    v_o[...] = v
    kk_o[...] = kk
    kd_o[0] = kd_f
    kd_o[1] = kd_b
    lw_o[0] = lw[:, :RW_WIDTH]
    lw_o[1] = lw[:, RW_WIDTH:]
    ba_o[0] = kk * a_f
    ba_o[1] = kk * a_b
    g_o[...] = g
    bon_o[...] = bonus


def _rwprep(z, mup, mun, w0, wup, a0, aup, gup, kk_w, ka_w, rk_w, vmix, *, tm):
    s = z.shape[0]
    nb8 = s // SUBLANES
    tb = tm // SUBLANES
    has_vmix = vmix is not None

    def main(width, cb):
        return pl.BlockSpec((tm, width), lambda i: (i, cb))

    def prev(width, cb):
        return pl.BlockSpec((SUBLANES, width), lambda i: (jnp.maximum(i * tb - 1, 0), cb))

    def nxt(width, cb):
        return pl.BlockSpec((SUBLANES, width), lambda i: (jnp.minimum((i + 1) * tb, nb8 - 1), cb))

    def full(arr):
        nd = arr.ndim
        return pl.BlockSpec(arr.shape, lambda i: (0,) * nd)

    cols = [(1024, COL_R // 1024), (1024, COL_K // 1024), (1024, COL_V // 1024), (512, COL_LORA // 512)]
    in_specs = ([main(w, c) for w, c in cols] + [prev(w, c) for w, c in cols]
                + [nxt(w, c) for w, c in cols])
    args = [z] * 12
    consts = [mup, mun, w0, wup, a0, aup, gup, kk_w, ka_w, rk_w]
    in_specs += [full(c) for c in consts]
    args += consts
    if has_vmix:
        vfirst, v0, vdown, vup = vmix
        in_specs += [pl.BlockSpec((tm, RW_WIDTH), lambda i: (i, 0)), full(v0), full(vdown), full(vup)]
        args += [vfirst, v0, vdown, vup]

    one = jax.ShapeDtypeStruct((s, RW_WIDTH), F32)
    two = jax.ShapeDtypeStruct((2, s, RW_WIDTH), F32)
    spec1 = pl.BlockSpec((tm, RW_WIDTH), lambda i: (i, 0))
    spec2 = pl.BlockSpec((2, tm, RW_WIDTH), lambda i: (0, i, 0))
    return pl.pallas_call(
        functools.partial(_rwprep_kernel, has_vmix),
        out_shape=[one, one, one, two, two, two, one, one],
        grid=(s // tm,),
        in_specs=in_specs,
        out_specs=[spec1, spec1, spec1, spec2, spec2, spec2, spec1, spec1],
        compiler_params=_cparams(("parallel",)),
        name="rwkv_prep",
    )(*args)


def _rwrec_kernel(r_ref, v_ref, kk_ref, kd_ref, lw_ref, ba_ref, y_ref, s_ref):
    d = pl.program_id(0)
    T = CHUNK

    @pl.when(pl.program_id(1) == 0)
    def _():
        s_ref[...] = jnp.zeros_like(s_ref)

    sgn = 1 - 2 * d
    row = lax.broadcasted_iota(jnp.int32, (T, T), 0)
    col = lax.broadcasted_iota(jnp.int32, (T, T), 1)
    tri = jnp.where((row - col) * sgn >= 0, 1.0, 0.0).astype(BF16)

    lw = lw_ref[...]
    hi, mid, lo = _split3(lw)
    cin = (jnp.dot(tri, hi, preferred_element_type=F32)
           + jnp.dot(tri, mid, preferred_element_type=F32)
           + jnp.dot(tri, lo, preferred_element_type=F32))
    ctot = jnp.sum(lw, axis=0, keepdims=True)
    e_in = jnp.exp(cin)
    e_ex = jnp.exp(cin - lw)
    e_neg = jnp.exp(-cin)
    e_end = jnp.exp(ctot - cin)
    e_tot = jnp.exp(ctot)

    kd = kd_ref[...]
    ba = ba_ref[...]
    r_t = r_ref[...] * e_in
    a_t = -kk_ref[...] * e_ex
    b_t = ba * e_neg
    k_t = kd * e_neg
    b_e = ba * e_end
    k_e = kd * e_end
    v = v_ref[...]

    R2 = lax.broadcasted_iota(jnp.int32, (2 * T, 2 * T), 0)
    C2 = lax.broadcasted_iota(jnp.int32, (2 * T, 2 * T), 1)
    same = (R2 // T) == (C2 // T)
    diff = ((R2 % T) - (C2 % T)) * sgn
    mask_a = same & (diff > 0)
    mask_y = same & (diff >= 0)
    mask_d = same & (((R2 % T) // SUB) == ((C2 % T) // SUB))
    eye = R2 == C2
    head0 = lax.broadcasted_iota(jnp.int32, (T, LANES), 1) < RW_HEAD

    def stack_heads(x):
        return jnp.concatenate([jnp.where(head0, x, 0.0), jnp.where(head0, 0.0, x)], axis=0)

    def stack_dup(x):
        return jnp.concatenate([x, x], axis=0)

    slabs = range(RW_WIDTH // LANES)
    cols = [slice(sl * LANES, (sl + 1) * LANES) for sl in slabs]
    T2 = 2 * T
    ar2 = [jnp.concatenate([stack_heads(a_t[:, cs]), stack_heads(r_t[:, cs])], axis=0).astype(BF16)
           for cs in cols]
    bk2 = [jnp.concatenate([stack_dup(b_t[:, cs]), stack_dup(k_t[:, cs])], axis=0).astype(BF16)
           for cs in cols]
    v2 = [stack_heads(v[:, cs]).astype(BF16) for cs in cols]
    bke2 = [jnp.concatenate([stack_heads(b_e[:, cs]), stack_heads(k_e[:, cs])], axis=0).astype(BF16)
            for cs in cols]

    sc = [_dot_nt(ar2[s], bk2[s]) for s in slabs]
    n_ab = [jnp.where(mask_a, sc[s][:T2, :T2], 0.0) for s in slabs]
    n_ak = [jnp.where(mask_a, sc[s][:T2, T2:], 0.0) for s in slabs]
    m_rbk = [jnp.concatenate([jnp.where(mask_y, sc[s][T2:, :T2], 0.0),
                              jnp.where(mask_y, sc[s][T2:, T2:], 0.0)], axis=1).astype(BF16)
             for s in slabs]

    def dot2(lhs, a, b):
        t = _dot(lhs, jnp.concatenate([a, b], axis=1))
        return t[:, :T2], t[:, T2:]

    p = [jnp.where(mask_d, n_ab[s], 0.0) for s in slabs]
    q = [jnp.where(eye, 1.0, p[s]) for s in slabs]
    p = [_dot(p[s], p[s]) for s in slabs]
    for _ in range(2):
        pq = [dot2(p[s], p[s], q[s]) for s in slabs]
        p = [pq[s][0] for s in slabs]
        q = [q[s] + pq[s][1] for s in slabs]
    q = [q[s] + _dot(p[s], q[s]) for s in slabs]
    akv = [_dot(n_ak[s], v2[s]) for s in slabs]

    st = [s_ref[s] for s in slabs]
    ars = [_dot_nt(ar2[s], st[s]) for s in slabs]
    mx = [dot2(q[s], jnp.where(mask_d, 0.0, n_ab[s]), ars[s][:T2] + akv[s]) for s in slabs]
    m = [mx[s][0] for s in slabs]
    x = [mx[s][1] for s in slabs]
    mx = [dot2(m[s], m[s], x[s]) for s in slabs]
    x = [x[s] + mx[s][1] for s in slabs]
    x = [x[s] + _dot(mx[s][0], x[s]) for s in slabs]
    for s in slabs:
        y2 = ars[s][T2:] + _dot(m_rbk[s], jnp.concatenate([x[s].astype(BF16), v2[s]], axis=0))
        y_ref[:, cols[s]] = y2[:T] + y2[T:]
    for s in slabs:
        upd = _dot(jnp.concatenate([x[s].T, v2[s].astype(F32).T], axis=1), bke2[s])
        s_ref[s] = st[s] * e_tot[:, cols[s]] + upd


def _rwrec(r, v, kk, kd, lw, ba):
    s = r.shape[0]
    nc = s // CHUNK

    def cidx(d, c):
        return c + d * (nc - 1 - 2 * c)

    spec1 = pl.BlockSpec((CHUNK, RW_WIDTH), lambda d, c: (cidx(d, c), 0))
    spec2 = pl.BlockSpec((None, CHUNK, RW_WIDTH), lambda d, c: (d, cidx(d, c), 0))
    return pl.pallas_call(
        _rwrec_kernel,
        out_shape=jax.ShapeDtypeStruct((2, s, RW_WIDTH), F32),
        grid=(2, nc),
        in_specs=[spec1, spec1, spec1, spec2, spec2, spec2],
        out_specs=spec2,
        scratch_shapes=[pltpu.VMEM((RW_WIDTH // LANES, LANES, LANES), F32)],
        compiler_params=_cparams(("arbitrary", "arbitrary")),
        name="rwkv_recurrence",
    )(r, v, kk, kd, lw, ba)


def _atprep_kernel(zq, zkv, cos_ref, sin_ref, qn, kn, qt_o, k_o, vt_o):
    ones = _head_ones()
    cos = cos_ref[...]
    sin = sin_ref[...]
    lane = lax.broadcasted_iota(jnp.int32, cos.shape, 1)
    even = ((lane // 16) % 2) == 0

    def norm_rope(t, gain):
        ms = _head_sum(t * t, ones) * (1.0 / AT_HEAD)
        tn = t * lax.rsqrt(ms + EPS) * gain
        swapped = jnp.where(even, pltpu.roll(tn, LANES - 16, 1), pltpu.roll(tn, 16, 1))
        return tn * cos + swapped * sin

    q_scale = (AT_HEAD ** -0.5) * LOG2E
    tm = zq.shape[0]
    ones_rows = jnp.where(lax.broadcasted_iota(jnp.int32, (V_AUG - AT_HEAD, tm), 0) == 0, 1.0, 0.0).astype(BF16)
    for sl in range(AT_Q // LANES):
        out = norm_rope(zq[:, sl * LANES:(sl + 1) * LANES], qn[...]) * q_scale
        out_t = out.T.astype(BF16)
        qt_o[2 * sl] = out_t[:AT_HEAD]
        qt_o[2 * sl + 1] = out_t[AT_HEAD:]
    for sl in range(AT_KV // LANES):
        out = norm_rope(zkv[:, sl * LANES:(sl + 1) * LANES], kn[...])
        k_o[2 * sl] = out[:, :AT_HEAD].astype(BF16)
        k_o[2 * sl + 1] = out[:, AT_HEAD:].astype(BF16)
        vt = zkv[:, AT_KV + sl * LANES:AT_KV + (sl + 1) * LANES].T.astype(BF16)
        vt_o[2 * sl] = jnp.concatenate([vt[:AT_HEAD], ones_rows], axis=0)
        vt_o[2 * sl + 1] = jnp.concatenate([vt[AT_HEAD:], ones_rows], axis=0)


def _atprep(z, cos, sin, qn, kn, *, tm):
    s = z.shape[0]
    nb = s // tm
    return pl.pallas_call(
        _atprep_kernel,
        out_shape=[jax.ShapeDtypeStruct((AT_HEADS, AT_HEAD, s), BF16),
                   jax.ShapeDtypeStruct((AT_KV_HEADS, nb, tm, AT_HEAD), BF16),
                   jax.ShapeDtypeStruct((AT_KV_HEADS, nb, V_AUG, tm), BF16)],
        grid=(nb,),
        in_specs=[pl.BlockSpec((tm, AT_Q), lambda i: (i, COL_Q // AT_Q)),
                  pl.BlockSpec((tm, 2 * AT_KV), lambda i: (i, COL_ATKV // (2 * AT_KV))),
                  pl.BlockSpec((tm, LANES), lambda i: (i, 0)),
                  pl.BlockSpec((tm, LANES), lambda i: (i, 0)),
                  pl.BlockSpec((1, LANES), lambda i: (0, 0)),
                  pl.BlockSpec((1, LANES), lambda i: (0, 0))],
        out_specs=[pl.BlockSpec((AT_HEADS, AT_HEAD, tm), lambda i: (0, 0, i)),
                   pl.BlockSpec((AT_KV_HEADS, None, tm, AT_HEAD), lambda i: (0, i, 0, 0)),
                   pl.BlockSpec((AT_KV_HEADS, None, V_AUG, tm), lambda i: (0, i, 0, 0))],
        compiler_params=_cparams(("parallel",)),
        name="attn_prep",
    )(z, z, cos, sin, qn, kn)


def _flash_kernel(qt_ref, k_ref, vt_ref, o_ref, m_ref, acc_ref, sa_ref, sb_ref):
    grp = qt_ref.shape[0]
    nkb = k_ref.shape[0]
    m_ref[...] = jnp.full_like(m_ref, -jnp.inf)
    acc_ref[...] = jnp.zeros_like(acc_ref)

    def step(kb_cur, cur_ref, kb_next, next_ref):
        k_next = k_ref[kb_next]
        vt = vt_ref[kb_cur]
        for g in range(grp):
            next_ref[g] = jnp.dot(k_next, qt_ref[g], preferred_element_type=F32)
            st = cur_ref[g]
            m_prev = m_ref[g]
            m_new = jnp.maximum(m_prev, jnp.max(st, axis=0, keepdims=True))
            alpha = jnp.exp2(m_prev - m_new)
            pt = jnp.exp2(st - m_new).astype(BF16)
            m_ref[g] = m_new
            acc_ref[g] = alpha * acc_ref[g] + jnp.dot(vt, pt, preferred_element_type=F32)

    k0 = k_ref[0]
    for g in range(grp):
        sa_ref[g] = jnp.dot(k0, qt_ref[g], preferred_element_type=F32)

    unroll = 8

    def body(j, carry):
        kb = unroll * j
        for u in range(0, unroll, 2):
            step(kb + u, sa_ref, kb + u + 1, sb_ref)
            step(kb + u + 1, sb_ref, jnp.minimum(kb + u + 2, nkb - 1), sa_ref)
        return carry

    lax.fori_loop(0, nkb // unroll, body, 0)
    for pair in range(grp // 2):
        a0 = acc_ref[2 * pair]
        a1 = acc_ref[2 * pair + 1]
        o2 = jnp.concatenate([a0[:AT_HEAD] / a0[AT_HEAD:AT_HEAD + 1],
                              a1[:AT_HEAD] / a1[AT_HEAD:AT_HEAD + 1]], axis=0)
        o_ref[:, pair * LANES:(pair + 1) * LANES] = o2.T.astype(o_ref.dtype)


def _flash(qt, k, vt, *, tq):
    s = qt.shape[2]
    _, nkb, tk, _ = k.shape
    grp = AT_HEADS // AT_KV_HEADS
    return pl.pallas_call(
        _flash_kernel,
        out_shape=jax.ShapeDtypeStruct((s, AT_Q), BF16),
        grid=(AT_KV_HEADS, s // tq),
        in_specs=[pl.BlockSpec((grp, AT_HEAD, tq), lambda h, i: (h, 0, i)),
                  pl.BlockSpec((None, nkb, tk, AT_HEAD), lambda h, i: (h, 0, 0, 0)),
                  pl.BlockSpec((None, nkb, V_AUG, tk), lambda h, i: (h, 0, 0, 0))],
        out_specs=pl.BlockSpec((tq, grp * AT_HEAD), lambda h, i: (i, h)),
        scratch_shapes=[pltpu.VMEM((grp, 1, tq), F32), pltpu.VMEM((grp, V_AUG, tq), F32),
                        pltpu.VMEM((grp, tk, tq), F32), pltpu.VMEM((grp, tk, tq), F32)],
        compiler_params=_cparams(("parallel", "parallel")),
        name="flash_attention",
    )(qt, k, vt)


def _mix_out_kernel(y_ref, g_ref, bon_ref, yat_ref, zg1_ref, zg2_ref, x_ref, lnw, lnb,
                    wrw, wat, wo, npost, o_ref):
    ones = _head_ones()
    y = y_ref[0] + y_ref[1]
    inv = 1.0 / RW_HEAD
    mu = _head_sum(y, ones) * inv
    dlt = y - mu
    var = _head_sum(dlt * dlt, ones) * inv
    yn = dlt * lax.rsqrt(var + RW_GN_EPS) * lnw[...] + lnb[...]
    y_rw = (yn + bon_ref[...]) * g_ref[...]
    pa = _dot(y_rw, wrw[...])
    pb = jnp.dot(yat_ref[...], wat[...], preferred_element_type=F32)
    merged = _sigmoid(zg1_ref[...]) * pa + _sigmoid(zg2_ref[...]) * pb
    out = _dot(merged, wo[...])
    o_ref[...] = x_ref[...] + _rms(out, npost[...])


def _layer_block(arr, layer, **kw):
    return pl.BlockSpec((None,) + arr.shape[1:], lambda *_: (layer, 0, 0), **kw)


def _mix_out(y, g, bon, yat, z, x, lnw, lnb, wrw, wat, wo, npost, layer, *, tm):
    s = x.shape[0]

    def const(arr):
        if arr.ndim == 3:
            return _layer_block(arr, layer, pipeline_mode=pl.Buffered(1))
        return pl.BlockSpec(arr.shape, lambda i: (0, 0), pipeline_mode=pl.Buffered(1))

    row1k = pl.BlockSpec((tm, RW_WIDTH), lambda i: (i, 0))
    return pl.pallas_call(
        _mix_out_kernel,
        out_shape=jax.ShapeDtypeStruct((s, D_MODEL), F32),
        grid=(s // tm,),
        in_specs=[pl.BlockSpec((2, tm, RW_WIDTH), lambda i: (0, i, 0)), row1k, row1k, row1k,
                  pl.BlockSpec((tm, D_MODEL), lambda i: (i, COL_G1 // D_MODEL)),
                  pl.BlockSpec((tm, D_MODEL), lambda i: (i, COL_G2 // D_MODEL)),
                  pl.BlockSpec((tm, D_MODEL), lambda i: (i, 0)),
                  const(lnw), const(lnb), const(wrw), const(wat), const(wo), const(npost)],
        out_specs=pl.BlockSpec((tm, D_MODEL), lambda i: (i, 0)),
        compiler_params=_cparams(("parallel",)),
        name="mix_out",
    )(y, g, bon, yat, z, z, x, lnw, lnb, wrw, wat, wo, npost)


def _xattn_kernel(x_ref, kv_ref, npre, wq, wo, npost, o_ref):
    x = x_ref[...]
    h = _rms(x, npre[...])
    q = _dot(h, wq[...]) * (X_HEAD ** -0.5)
    outs = []
    for hd in range(X_HEADS):
        kh = kv_ref[:, hd * X_HEAD:(hd + 1) * X_HEAD]
        vh = kv_ref[:, X_WIDTH + hd * X_HEAD:X_WIDTH + (hd + 1) * X_HEAD]
        sc = _dot_nt(q[:, hd * X_HEAD:(hd + 1) * X_HEAD], kh)
        sc = sc - jnp.max(sc, axis=-1, keepdims=True)
        p = jnp.exp(sc)
        p = p / jnp.sum(p, axis=-1, keepdims=True)
        outs.append(_dot(p, vh))
    o = jnp.concatenate(outs, axis=1)
    c = _dot(o, wo[...])
    o_ref[...] = x + _rms(c, npost[...])


def _xattn(x, kv, npre, wq, wo, npost, layer, *, tm):
    s = x.shape[0]

    def const(arr):
        if arr.ndim == 3:
            return _layer_block(arr, layer)
        return pl.BlockSpec(arr.shape, lambda i: (0, 0))

    return pl.pallas_call(
        _xattn_kernel,
        out_shape=jax.ShapeDtypeStruct((s, D_MODEL), F32),
        grid=(s // tm,),
        in_specs=[pl.BlockSpec((tm, D_MODEL), lambda i: (i, 0)), const(kv), const(npre),
                  const(wq), const(wo), const(npost)],
        out_specs=pl.BlockSpec((tm, D_MODEL), lambda i: (i, 0)),
        compiler_params=_cparams(("parallel",)),
        name="cross_attention",
    )(x, kv, npre, wq, wo, npost)


def _ffn_kernel(x_ref, npre, wg, wu, wd, npost, o_ref, h_ref, acc_ref):
    j = pl.program_id(1)

    @pl.when(j == 0)
    def _():
        h_ref[...] = _rms(x_ref[...], npre[...]).astype(BF16)
        acc_ref[...] = jnp.zeros_like(acc_ref)

    h = h_ref[...]
    gt = jnp.dot(h, wg[...], preferred_element_type=F32)
    ut = jnp.dot(h, wu[...], preferred_element_type=F32)
    act = gt * _sigmoid(gt) * ut
    acc_ref[...] += _dot(act, wd[...])

    @pl.when(j == pl.num_programs(1) - 1)
    def _():
        o_ref[...] = x_ref[...] + _rms(acc_ref[...], npost[...])


def _ffn(x, npre, wg, wu, wd, npost, layer, *, tm, tf):
    s = x.shape[0]
    f = wg.shape[2]
    return pl.pallas_call(
        _ffn_kernel,
        out_shape=jax.ShapeDtypeStruct((s, D_MODEL), F32),
        grid=(s // tm, f // tf),
        in_specs=[pl.BlockSpec((tm, D_MODEL), lambda i, j: (i, 0)),
                  pl.BlockSpec((1, D_MODEL), lambda i, j: (0, 0)),
                  pl.BlockSpec((None, D_MODEL, tf), lambda i, j: (layer, 0, j)),
                  pl.BlockSpec((None, D_MODEL, tf), lambda i, j: (layer, 0, j)),
                  pl.BlockSpec((None, tf, D_MODEL), lambda i, j: (layer, j, 0)),
                  pl.BlockSpec((1, D_MODEL), lambda i, j: (0, 0))],
        out_specs=pl.BlockSpec((tm, D_MODEL), lambda i, j: (i, 0)),
        scratch_shapes=[pltpu.VMEM((tm, D_MODEL), BF16), pltpu.VMEM((tm, D_MODEL), F32)],
        compiler_params=_cparams(("parallel", "arbitrary")),
        name="swiglu",
    )(x, npre, wg, wu, wd, npost)


def _pack_w_in(w):
    o_lora = 3 * RW_WIDTH
    o_gd = o_lora + 2 * W_LORA + 2 * A_LORA
    o_at = o_gd + G_LORA
    o_gate = o_at + AT_Q + 2 * AT_KV
    w = w.astype(BF16)
    parts = [w[..., o_gate:o_gate + 2 * D_MODEL],
             w[..., 0:3 * RW_WIDTH],
             w[..., o_at:o_at + AT_Q],
             w[..., o_lora:o_at], jnp.zeros(w.shape[:2] + (G_PAD - G_LORA,), BF16),
             w[..., o_at + AT_Q:o_gate]]
    return jnp.concatenate(parts, axis=2)


def _pack_mu(mu):
    return jnp.concatenate([mu, jnp.zeros((G_PAD - G_LORA,), mu.dtype)])[None, :]


def _block_diag_up(up):
    z = jnp.zeros_like(up[0])
    return jnp.concatenate([jnp.concatenate([up[0], z], axis=1),
                            jnp.concatenate([z, up[1]], axis=1)], axis=0).astype(BF16)


def _rope_tables(s):
    rows = s // GRID_W
    row = jnp.repeat(jnp.arange(rows), GRID_W).astype(F32)
    col = jnp.tile(jnp.arange(GRID_W), rows).astype(F32)
    n_freq = AT_HEAD // 4
    inv = ROPE_THETA ** (-jnp.arange(n_freq, dtype=F32) / n_freq)
    ar = row[:, None] * inv
    ac = col[:, None] * inv
    cos = jnp.concatenate([jnp.cos(ar), jnp.cos(ar), jnp.cos(ac), jnp.cos(ac)], axis=1)
    sin = jnp.concatenate([-jnp.sin(ar), jnp.sin(ar), -jnp.sin(ac), jnp.sin(ac)], axis=1)
    return jnp.tile(cos, (1, 2)), jnp.tile(sin, (1, 2))


def kernel(x, mem, n_mix_pre, n_mix_post, n_x_pre, n_x_post, n_ffn_pre, n_ffn_post, n_mem, w_in, rw_mu_prev, rw_mu_next, rw_w0, rw_w_up, rw_a0, rw_a_up, rw_g_up, rw_v0, rw_v_down, rw_v_up, rw_k_k, rw_k_a, rw_r_k, rw_ln_w, rw_ln_b, w_rw_out, at_q_norm, at_k_norm, w_at_out, w_o, x_wq, x_wkv, x_wo, ffn_wg, ffn_wu, ffn_wd):
    b, s, d = x.shape
    assert b == 1 and d == D_MODEL and s % 512 == 0
    xs = x[0]
    mems = mem[0]
    cos, sin = _rope_tables(s)
    row = lambda t: t[None, :]
    w_in_p = _pack_w_in(w_in)
    wrw_b, wat_b, wo_b = w_rw_out.astype(BF16), w_at_out.astype(BF16), w_o.astype(BF16)
    xwq_b, xwkv_b, xwo_b = x_wq.astype(BF16), x_wkv.astype(BF16), x_wo.astype(BF16)
    wg_b, wu_b, wd_b = ffn_wg.astype(BF16), ffn_wu.astype(BF16), ffn_wd.astype(BF16)
    v_first = None
    for l in range(DEPTH):
        z = _norm_proj(xs, row(n_mix_pre[l]), w_in_p, l, tm=512, tn=1024, out_dtype=F32)

        vmix = None
        if l > 0:
            vmix = (v_first, row(rw_v0[l - 1]), rw_v_down[l - 1].astype(BF16), rw_v_up[l - 1].astype(BF16))
        gup = jnp.concatenate([rw_g_up[l], jnp.zeros((G_PAD - G_LORA, RW_WIDTH), F32)], axis=0).astype(BF16)
        r, v, kk, kd, lw, ba, g, bon = _rwprep(
            z, _pack_mu(rw_mu_prev[l]), _pack_mu(rw_mu_next[l]),
            rw_w0[l].reshape(1, 2 * RW_WIDTH), _block_diag_up(rw_w_up[l]),
            rw_a0[l].reshape(1, 2 * RW_WIDTH), _block_diag_up(rw_a_up[l]), gup,
            row(rw_k_k[l]), row(rw_k_a[l]), rw_r_k[l].reshape(1, RW_WIDTH), vmix, tm=256)
        if l == 0:
            v_first = v
        y = _rwrec(r, v, kk, kd, lw, ba)

        qn = jnp.tile(at_q_norm[l], 2)[None, :]
        kn = jnp.tile(at_k_norm[l], 2)[None, :]
        q_t, k_blk, v_t = _atprep(z, cos, sin, qn, kn, tm=512)
        y_at = _flash(q_t, k_blk, v_t, tq=256)

        xs = _mix_out(y, g, bon, y_at, z, xs, row(rw_ln_w[l]), row(rw_ln_b[l]),
                      wrw_b, wat_b, wo_b, row(n_mix_post[l]), l, tm=256)

        kv = _norm_proj(mems, row(n_mem[l]), xwkv_b, l, tm=mems.shape[0], tn=512, out_dtype=BF16)
        xs = _xattn(xs, kv, row(n_x_pre[l]), xwq_b, xwo_b, row(n_x_post[l]), l, tm=512)

        xs = _ffn(xs, row(n_ffn_pre[l]), wg_b, wu_b, wd_b, row(n_ffn_post[l]), l, tm=512, tf=512)
    return xs[None]
```

```python
import functools

import jax
import jax.numpy as jnp
from jax import lax
from jax.experimental import pallas as pl
from jax.experimental.pallas import tpu as pltpu

F32 = jnp.float32
BF16 = jnp.bfloat16

D_MODEL = 2048
DEPTH = 4
GRID_W = 64
EPS = 1e-6
RW_HEAD = 64
RW_WIDTH = 1024
W_LORA = 64
A_LORA = 64
G_LORA = 160
RW_GN_EPS = 64e-5
AT_HEADS = 16
AT_KV_HEADS = 4
AT_HEAD = 64
AT_Q = AT_HEADS * AT_HEAD
AT_KV = AT_KV_HEADS * AT_HEAD
ROPE_THETA = 10000.0
X_HEADS = 4
X_HEAD = 128
X_WIDTH = X_HEADS * X_HEAD
D_FF = 5632

LANES = 128
SUBLANES = 8
CHUNK = 64
SUB = 16
LOG2E = 1.4426950408889634
V_AUG = AT_HEAD + 16
G_PAD = 256

COL_G1 = 0
COL_G2 = 2048
COL_R = 4096
COL_K = 5120
COL_V = 6144
COL_Q = 7168
COL_LORA = 8192
COL_ATKV = 8704
IN_PACKED = 9216

VMEM_LIMIT = 56 * 1024 * 1024


def _cparams(sem):
    return pltpu.CompilerParams(dimension_semantics=sem, vmem_limit_bytes=VMEM_LIMIT)


def _dot(a, b):
    return jnp.dot(a.astype(BF16), b.astype(BF16), preferred_element_type=F32)


def _dot_nt(a, b):
    return lax.dot_general(a.astype(BF16), b.astype(BF16), (((1,), (1,)), ((), ())),
                           preferred_element_type=F32)


def _split2(x):
    hi = x.astype(BF16)
    lo = (x - hi.astype(F32)).astype(BF16)
    return hi, lo


def _split3(x):
    hi = x.astype(BF16)
    r1 = x - hi.astype(F32)
    mid = r1.astype(BF16)
    lo = (r1 - mid.astype(F32)).astype(BF16)
    return hi, mid, lo


def _sigmoid(x):
    return 1.0 / (1.0 + jnp.exp(-x))


def _softplus(x):
    return jnp.maximum(x, 0.0) + jnp.log(1.0 + jnp.exp(-jnp.abs(x)))


def _head_ones():
    r = lax.broadcasted_iota(jnp.int32, (LANES, LANES), 0) // RW_HEAD
    c = lax.broadcasted_iota(jnp.int32, (LANES, LANES), 1) // RW_HEAD
    return jnp.where(r == c, 1.0, 0.0).astype(BF16)


def _head_sum(x, ones):
    outs = []
    for s in range(x.shape[1] // LANES):
        hi, lo = _split2(x[:, s * LANES:(s + 1) * LANES])
        outs.append(jnp.dot(hi, ones, preferred_element_type=F32)
                    + jnp.dot(lo, ones, preferred_element_type=F32))
    return outs[0] if len(outs) == 1 else jnp.concatenate(outs, axis=1)


def _rms(x, gain):
    ms = jnp.mean(x * x, axis=-1, keepdims=True)
    return x * lax.rsqrt(ms + EPS) * gain


def _norm_proj_kernel(x_ref, g_ref, w_ref, o_ref, h_ref):
    @pl.when(pl.program_id(1) == 0)
    def _():
        h_ref[...] = _rms(x_ref[...], g_ref[...]).astype(BF16)

    o_ref[...] = jnp.dot(h_ref[...], w_ref[...], preferred_element_type=F32).astype(o_ref.dtype)


def _norm_proj(x, gain, w, layer, *, tm, tn, out_dtype):
    m, d = x.shape
    n = w.shape[2]
    return pl.pallas_call(
        _norm_proj_kernel,
        out_shape=jax.ShapeDtypeStruct((m, n), out_dtype),
        grid=(m // tm, n // tn),
        in_specs=[pl.BlockSpec((tm, d), lambda i, j: (i, 0)),
                  pl.BlockSpec((1, d), lambda i, j: (0, 0)),
                  pl.BlockSpec((None, d, tn), lambda i, j: (layer, 0, j))],
        out_specs=pl.BlockSpec((tm, tn), lambda i, j: (i, j)),
        scratch_shapes=[pltpu.VMEM((tm, d), BF16)],
        compiler_params=_cparams(("parallel", "arbitrary")),
        name="norm_proj",
    )(x, gain, w)


def _shift(z, prev_row, next_row, mu_p, mu_n):
    rows = z.shape[0]
    ridx = lax.broadcasted_iota(jnp.int32, z.shape, 0)
    zp = jnp.where(ridx == 0, prev_row, pltpu.roll(z, 1, 0))
    zn = jnp.where(ridx == rows - 1, next_row, pltpu.roll(z, rows - 1, 0))
    return z + mu_p * (zp - z) + mu_n * (zn - z)


def _rwprep_kernel(has_vmix, *refs):
    (zr, zk, zv, zl, pr, pk, pv, pL, nr, nk, nv, nL, mup, mun, w0, wup, a0, aup, gup,
     kk_w, ka_w, rk_w) = refs[:22]
    pos = 22
    if has_vmix:
        vfirst, v0, vdown, vup = refs[pos:pos + 4]
        pos += 4
    r_o, v_o, kk_o, kd_o, lw_o, ba_o, g_o, bon_o = refs[pos:pos + 8]

    i = pl.program_id(0)
    last = pl.num_programs(0) - 1
    keep_p = jnp.where(i == 0, 0.0, 1.0)
    keep_n = jnp.where(i == last, 0.0, 1.0)

    def shifted(z_ref, p_ref, n_ref, lo, hi):
        p_row = p_ref[SUBLANES - 1:SUBLANES, :] * keep_p
        n_row = n_ref[0:1, :] * keep_n
        return _shift(z_ref[...], p_row, n_row, mup[:, lo:hi], mun[:, lo:hi])

    r = shifted(zr, pr, nr, 0, 1024)
    k = shifted(zk, pk, nk, 1024, 2048)
    v = shifted(zv, pv, nv, 2048, 3072)
    lora = shifted(zl, pL, nL, 3072, 3584)

    if has_vmix:
        mix = _sigmoid(v0[...] + _dot(_dot(v, vdown[...]), vup[...]))
        v = v + (vfirst[...] - v) * mix

    u = w0[...] + _dot(jnp.tanh(lora[:, 0:128]), wup[...])
    w_log = -_softplus(-u) - 0.5
    lw = -jnp.exp(w_log)
    a = _sigmoid(a0[...] + _dot(lora[:, 128:256], aup[...]))
    g = _dot(_sigmoid(lora[:, 256:512]), gup[...])

    ones = _head_ones()
    kk = k * kk_w[...]
    norm = jnp.sqrt(_head_sum(kk * kk, ones))
    kk = kk / jnp.maximum(norm, 1e-12)
    ka = ka_w[...]
    a_f = a[:, :RW_WIDTH]
    a_b = a[:, RW_WIDTH:]
    kd_f = k * (1.0 + (a_f - 1.0) * ka)
    kd_b = k * (1.0 + (a_b - 1.0) * ka)
    bonus = _head_sum(r * rk_w[...] * (kd_f + kd_b), ones) * v

    r_o[...] = r
    v_o[...] = v
    kk_o[...] = kk
    kd_o[0] = kd_f
    kd_o[1] = kd_b
    lw_o[0] = lw[:, :RW_WIDTH]
    lw_o[1] = lw[:, RW_WIDTH:]
    ba_o[0] = kk * a_f
    ba_o[1] = kk * a_b
    g_o[...] = g
    bon_o[...] = bonus


def _rwprep(z, mup, mun, w0, wup, a0, aup, gup, kk_w, ka_w, rk_w, vmix, *, tm):
    s = z.shape[0]
    nb8 = s // SUBLANES
    tb = tm // SUBLANES
    has_vmix = vmix is not None

    def main(width, cb):
        return pl.BlockSpec((tm, width), lambda i: (i, cb))

    def prev(width, cb):
        return pl.BlockSpec((SUBLANES, width), lambda i: (jnp.maximum(i * tb - 1, 0), cb))

    def nxt(width, cb):
        return pl.BlockSpec((SUBLANES, width), lambda i: (jnp.minimum((i + 1) * tb, nb8 - 1), cb))

    def full(arr):
        nd = arr.ndim
        return pl.BlockSpec(arr.shape, lambda i: (0,) * nd)

    cols = [(1024, COL_R // 1024), (1024, COL_K // 1024), (1024, COL_V // 1024), (512, COL_LORA // 512)]
    in_specs = ([main(w, c) for w, c in cols] + [prev(w, c) for w, c in cols]
                + [nxt(w, c) for w, c in cols])
    args = [z] * 12
    consts = [mup, mun, w0, wup, a0, aup, gup, kk_w, ka_w, rk_w]
    in_specs += [full(c) for c in consts]
    args += consts
    if has_vmix:
        vfirst, v0, vdown, vup = vmix
        in_specs += [pl.BlockSpec((tm, RW_WIDTH), lambda i: (i, 0)), full(v0), full(vdown), full(vup)]
        args += [vfirst, v0, vdown, vup]

    one = jax.ShapeDtypeStruct((s, RW_WIDTH), F32)
    two = jax.ShapeDtypeStruct((2, s, RW_WIDTH), F32)
    spec1 = pl.BlockSpec((tm, RW_WIDTH), lambda i: (i, 0))
    spec2 = pl.BlockSpec((2, tm, RW_WIDTH), lambda i: (0, i, 0))
    return pl.pallas_call(
        functools.partial(_rwprep_kernel, has_vmix),
        out_shape=[one, one, one, two, two, two, one, one],
        grid=(s // tm,),
        in_specs=in_specs,
        out_specs=[spec1, spec1, spec1, spec2, spec2, spec2, spec1, spec1],
        compiler_params=_cparams(("parallel",)),
        name="rwkv_prep",
    )(*args)


def _rwrec_kernel(rf_ref, vf_ref, kkf_ref, kdf_ref, lwf_ref, baf_ref,
                  rb_ref, vb_ref, kkb_ref, kdb_ref, lwb_ref, bab_ref, yf_ref, yb_ref, s_ref):
    T = CHUNK
    T2 = 2 * T
    n_slab = RW_WIDTH // LANES

    @pl.when(pl.program_id(0) == 0)
    def _():
        s_ref[...] = jnp.zeros_like(s_ref)

    row = lax.broadcasted_iota(jnp.int32, (T, T), 0)
    col = lax.broadcasted_iota(jnp.int32, (T, T), 1)
    R2 = lax.broadcasted_iota(jnp.int32, (T2, T2), 0)
    C2 = lax.broadcasted_iota(jnp.int32, (T2, T2), 1)
    same = (R2 // T) == (C2 // T)
    tdiff = (R2 % T) - (C2 % T)
    mask_d = same & (((R2 % T) // SUB) == ((C2 % T) // SUB))
    eye = R2 == C2
    head0 = lax.broadcasted_iota(jnp.int32, (T, LANES), 1) < RW_HEAD

    def stack_heads(x):
        return jnp.concatenate([jnp.where(head0, x, 0.0), jnp.where(head0, 0.0, x)], axis=0)

    def stack_dup(x):
        return jnp.concatenate([x, x], axis=0)

    cols = [slice(sl * LANES, (sl + 1) * LANES) for sl in range(n_slab)]
    ar2, bk2, v2, bke2, mask_a, mask_y, e_tot = [], [], [], [], [], [], []
    for sgn, (r_ref, v_ref, kk_ref, kd_ref, lw_ref, ba_ref) in (
            (1, (rf_ref, vf_ref, kkf_ref, kdf_ref, lwf_ref, baf_ref)),
            (-1, (rb_ref, vb_ref, kkb_ref, kdb_ref, lwb_ref, bab_ref))):
        tri = jnp.where((row - col) * sgn >= 0, 1.0, 0.0).astype(BF16)
        lw = lw_ref[...]
        hi, mid, lo = _split3(lw)
        cin = (jnp.dot(tri, hi, preferred_element_type=F32)
               + jnp.dot(tri, mid, preferred_element_type=F32)
               + jnp.dot(tri, lo, preferred_element_type=F32))
        ctot = jnp.sum(lw, axis=0, keepdims=True)
        e_in = jnp.exp(cin)
        e_ex = jnp.exp(cin - lw)
        e_neg = jnp.exp(-cin)
        e_end = jnp.exp(ctot - cin)
        etot = jnp.exp(ctot)
        kd = kd_ref[...]
        ba = ba_ref[...]
        r_t = r_ref[...] * e_in
        a_t = -kk_ref[...] * e_ex
        b_t = ba * e_neg
        k_t = kd * e_neg
        b_e = ba * e_end
        k_e = kd * e_end
        v = v_ref[...]
        m_a = same & (tdiff * sgn > 0)
        m_y = same & (tdiff * sgn >= 0)
        for cs in cols:
            ar2.append(jnp.concatenate([stack_heads(a_t[:, cs]), stack_heads(r_t[:, cs])],
                                       axis=0).astype(BF16))
            bk2.append(jnp.concatenate([stack_dup(b_t[:, cs]), stack_dup(k_t[:, cs])],
                                       axis=0).astype(BF16))
            v2.append(stack_heads(v[:, cs]).astype(BF16))
            bke2.append(jnp.concatenate([stack_heads(b_e[:, cs]), stack_heads(k_e[:, cs])],
                                        axis=0).astype(BF16))
            mask_a.append(m_a)
            mask_y.append(m_y)
            e_tot.append(etot[:, cs])

    slabs = range(2 * n_slab)
    sc = [_dot_nt(ar2[s], bk2[s]) for s in slabs]
    n_ab = [jnp.where(mask_a[s], sc[s][:T2, :T2], 0.0) for s in slabs]
    n_ak = [jnp.where(mask_a[s], sc[s][:T2, T2:], 0.0) for s in slabs]
    m_rbk = [jnp.concatenate([jnp.where(mask_y[s], sc[s][T2:, :T2], 0.0),
                              jnp.where(mask_y[s], sc[s][T2:, T2:], 0.0)], axis=1).astype(BF16)
             for s in slabs]

    def dot2(lhs, a, b):
        t = _dot(lhs, jnp.concatenate([a, b], axis=1))
        return t[:, :T2], t[:, T2:]

    p = [jnp.where(mask_d, n_ab[s], 0.0) for s in slabs]
    q = [jnp.where(eye, 1.0, p[s]) for s in slabs]
    p = [_dot(p[s], p[s]) for s in slabs]
    for _ in range(2):
        pq = [dot2(p[s], p[s], q[s]) for s in slabs]
        p = [pq[s][0] for s in slabs]
        q = [q[s] + pq[s][1] for s in slabs]
    q = [q[s] + _dot(p[s], q[s]) for s in slabs]
    akv = [_dot(n_ak[s], v2[s]) for s in slabs]

    st = [s_ref[s // n_slab, s % n_slab] for s in slabs]
    ars = [_dot_nt(ar2[s], st[s]) for s in slabs]
    mx = [dot2(q[s], jnp.where(mask_d, 0.0, n_ab[s]), ars[s][:T2] + akv[s]) for s in slabs]
    m = [mx[s][0] for s in slabs]
    x = [mx[s][1] for s in slabs]
    mx = [dot2(m[s], m[s], x[s]) for s in slabs]
    x = [x[s] + mx[s][1] for s in slabs]
    x = [x[s] + _dot(mx[s][0], x[s]) for s in slabs]
    for s in slabs:
        y2 = ars[s][T2:] + _dot(m_rbk[s], jnp.concatenate([x[s].astype(BF16), v2[s]], axis=0))
        y_ref = yf_ref if s < n_slab else yb_ref
        y_ref[:, cols[s % n_slab]] = y2[:T] + y2[T:]
    for s in slabs:
        upd = _dot(jnp.concatenate([x[s].T, v2[s].astype(F32).T], axis=1), bke2[s])
        s_ref[s // n_slab, s % n_slab] = st[s] * e_tot[s] + upd


def _rwrec(r, v, kk, kd, lw, ba):
    s = r.shape[0]
    nc = s // CHUNK
    fwd1 = pl.BlockSpec((CHUNK, RW_WIDTH), lambda c: (c, 0))
    bwd1 = pl.BlockSpec((CHUNK, RW_WIDTH), lambda c: (nc - 1 - c, 0))
    fwd2 = pl.BlockSpec((None, CHUNK, RW_WIDTH), lambda c: (0, c, 0))
    bwd2 = pl.BlockSpec((None, CHUNK, RW_WIDTH), lambda c: (1, nc - 1 - c, 0))
    out = jax.ShapeDtypeStruct((s, RW_WIDTH), F32)
    return pl.pallas_call(
        _rwrec_kernel,
        out_shape=[out, out],
        grid=(nc,),
        in_specs=[fwd1, fwd1, fwd1, fwd2, fwd2, fwd2, bwd1, bwd1, bwd1, bwd2, bwd2, bwd2],
        out_specs=[fwd1, bwd1],
        scratch_shapes=[pltpu.VMEM((2, RW_WIDTH // LANES, LANES, LANES), F32)],
        compiler_params=_cparams(("arbitrary",)),
        name="rwkv_recurrence",
    )(r, v, kk, kd, lw, ba, r, v, kk, kd, lw, ba)


def _atprep_kernel(zq, zkv, cos_ref, sin_ref, qn, kn, qt_o, k_o, vt_o):
    ones = _head_ones()
    cos = cos_ref[...]
    sin = sin_ref[...]
    lane = lax.broadcasted_iota(jnp.int32, cos.shape, 1)
    even = ((lane // 16) % 2) == 0

    def norm_rope(t, gain):
        ms = _head_sum(t * t, ones) * (1.0 / AT_HEAD)
        tn = t * lax.rsqrt(ms + EPS) * gain
        swapped = jnp.where(even, pltpu.roll(tn, LANES - 16, 1), pltpu.roll(tn, 16, 1))
        return tn * cos + swapped * sin

    q_scale = (AT_HEAD ** -0.5) * LOG2E
    tm = zq.shape[0]
    ones_rows = jnp.where(lax.broadcasted_iota(jnp.int32, (V_AUG - AT_HEAD, tm), 0) == 0, 1.0, 0.0).astype(BF16)
    for sl in range(AT_Q // LANES):
        out = norm_rope(zq[:, sl * LANES:(sl + 1) * LANES], qn[...]) * q_scale
        out_t = out.T.astype(BF16)
        qt_o[2 * sl] = out_t[:AT_HEAD]
        qt_o[2 * sl + 1] = out_t[AT_HEAD:]
    for sl in range(AT_KV // LANES):
        out = norm_rope(zkv[:, sl * LANES:(sl + 1) * LANES], kn[...])
        k_o[2 * sl] = out[:, :AT_HEAD].astype(BF16)
        k_o[2 * sl + 1] = out[:, AT_HEAD:].astype(BF16)
        vt = zkv[:, AT_KV + sl * LANES:AT_KV + (sl + 1) * LANES].T.astype(BF16)
        vt_o[2 * sl] = jnp.concatenate([vt[:AT_HEAD], ones_rows], axis=0)
        vt_o[2 * sl + 1] = jnp.concatenate([vt[AT_HEAD:], ones_rows], axis=0)


def _atprep(z, cos, sin, qn, kn, *, tm):
    s = z.shape[0]
    nb = s // tm
    return pl.pallas_call(
        _atprep_kernel,
        out_shape=[jax.ShapeDtypeStruct((AT_HEADS, AT_HEAD, s), BF16),
                   jax.ShapeDtypeStruct((AT_KV_HEADS, nb, tm, AT_HEAD), BF16),
                   jax.ShapeDtypeStruct((AT_KV_HEADS, nb, V_AUG, tm), BF16)],
        grid=(nb,),
        in_specs=[pl.BlockSpec((tm, AT_Q), lambda i: (i, COL_Q // AT_Q)),
                  pl.BlockSpec((tm, 2 * AT_KV), lambda i: (i, COL_ATKV // (2 * AT_KV))),
                  pl.BlockSpec((tm, LANES), lambda i: (i, 0)),
                  pl.BlockSpec((tm, LANES), lambda i: (i, 0)),
                  pl.BlockSpec((1, LANES), lambda i: (0, 0)),
                  pl.BlockSpec((1, LANES), lambda i: (0, 0))],
        out_specs=[pl.BlockSpec((AT_HEADS, AT_HEAD, tm), lambda i: (0, 0, i)),
                   pl.BlockSpec((AT_KV_HEADS, None, tm, AT_HEAD), lambda i: (0, i, 0, 0)),
                   pl.BlockSpec((AT_KV_HEADS, None, V_AUG, tm), lambda i: (0, i, 0, 0))],
        compiler_params=_cparams(("parallel",)),
        name="attn_prep",
    )(z, z, cos, sin, qn, kn)


def _flash_kernel(qt_ref, k_ref, vt_ref, o_ref, m_ref, acc_ref, sa_ref, sb_ref):
    grp = qt_ref.shape[0]
    nkb = k_ref.shape[0]
    m_ref[...] = jnp.full_like(m_ref, -jnp.inf)
    acc_ref[...] = jnp.zeros_like(acc_ref)

    def step(kb_cur, cur_ref, kb_next, next_ref):
        k_next = k_ref[kb_next]
        vt = vt_ref[kb_cur]
        for g in range(grp):
            next_ref[g] = jnp.dot(k_next, qt_ref[g], preferred_element_type=F32)
            st = cur_ref[g]
            m_prev = m_ref[g]
            m_new = jnp.maximum(m_prev, jnp.max(st, axis=0, keepdims=True))
            alpha = jnp.exp2(m_prev - m_new)
            pt = jnp.exp2(st - m_new).astype(BF16)
            m_ref[g] = m_new
            acc_ref[g] = alpha * acc_ref[g] + jnp.dot(vt, pt, preferred_element_type=F32)

    k0 = k_ref[0]
    for g in range(grp):
        sa_ref[g] = jnp.dot(k0, qt_ref[g], preferred_element_type=F32)

    unroll = max(u for u in (8, 4, 2) if nkb % u == 0)

    def body(j, carry):
        kb = unroll * j
        for u in range(0, unroll, 2):
            step(kb + u, sa_ref, kb + u + 1, sb_ref)
            step(kb + u + 1, sb_ref, jnp.minimum(kb + u + 2, nkb - 1), sa_ref)
        return carry

    lax.fori_loop(0, nkb // unroll, body, 0)
    for pair in range(grp // 2):
        a0 = acc_ref[2 * pair]
        a1 = acc_ref[2 * pair + 1]
        o2 = jnp.concatenate([a0[:AT_HEAD] / a0[AT_HEAD:AT_HEAD + 1],
                              a1[:AT_HEAD] / a1[AT_HEAD:AT_HEAD + 1]], axis=0)
        o_ref[:, pair * LANES:(pair + 1) * LANES] = o2.T.astype(o_ref.dtype)


def _flash(qt, k, vt, *, tq):
    s = qt.shape[2]
    _, nkb, tk, _ = k.shape
    grp = AT_HEADS // AT_KV_HEADS
    return pl.pallas_call(
        _flash_kernel,
        out_shape=jax.ShapeDtypeStruct((s, AT_Q), BF16),
        grid=(AT_KV_HEADS, s // tq),
        in_specs=[pl.BlockSpec((grp, AT_HEAD, tq), lambda h, i: (h, 0, i)),
                  pl.BlockSpec((None, nkb, tk, AT_HEAD), lambda h, i: (h, 0, 0, 0)),
                  pl.BlockSpec((None, nkb, V_AUG, tk), lambda h, i: (h, 0, 0, 0))],
        out_specs=pl.BlockSpec((tq, grp * AT_HEAD), lambda h, i: (i, h)),
        scratch_shapes=[pltpu.VMEM((grp, 1, tq), F32), pltpu.VMEM((grp, V_AUG, tq), F32),
                        pltpu.VMEM((grp, tk, tq), F32), pltpu.VMEM((grp, tk, tq), F32)],
        compiler_params=_cparams(("parallel", "parallel")),
        name="flash_attention",
    )(qt, k, vt)


def _mix_out_kernel(yf_ref, yb_ref, g_ref, bon_ref, yat_ref, zg1_ref, zg2_ref, x_ref, lnw, lnb,
                    wrw, wat, wo, npost, o_ref):
    ones = _head_ones()
    y = yf_ref[...] + yb_ref[...]
    inv = 1.0 / RW_HEAD
    mu = _head_sum(y, ones) * inv
    dlt = y - mu
    var = _head_sum(dlt * dlt, ones) * inv
    yn = dlt * lax.rsqrt(var + RW_GN_EPS) * lnw[...] + lnb[...]
    y_rw = (yn + bon_ref[...]) * g_ref[...]
    pa = _dot(y_rw, wrw[...])
    pb = jnp.dot(yat_ref[...], wat[...], preferred_element_type=F32)
    merged = _sigmoid(zg1_ref[...]) * pa + _sigmoid(zg2_ref[...]) * pb
    out = _dot(merged, wo[...])
    o_ref[...] = x_ref[...] + _rms(out, npost[...])


def _layer_block(arr, layer, **kw):
    return pl.BlockSpec((None,) + arr.shape[1:], lambda *_: (layer, 0, 0), **kw)


def _mix_out(yf, yb, g, bon, yat, z, x, lnw, lnb, wrw, wat, wo, npost, layer, *, tm):
    s = x.shape[0]

    def const(arr):
        if arr.ndim == 3:
            return _layer_block(arr, layer, pipeline_mode=pl.Buffered(1))
        return pl.BlockSpec(arr.shape, lambda i: (0, 0), pipeline_mode=pl.Buffered(1))

    row1k = pl.BlockSpec((tm, RW_WIDTH), lambda i: (i, 0))
    return pl.pallas_call(
        _mix_out_kernel,
        out_shape=jax.ShapeDtypeStruct((s, D_MODEL), F32),
        grid=(s // tm,),
        in_specs=[row1k, row1k, row1k, row1k, row1k,
                  pl.BlockSpec((tm, D_MODEL), lambda i: (i, COL_G1 // D_MODEL)),
                  pl.BlockSpec((tm, D_MODEL), lambda i: (i, COL_G2 // D_MODEL)),
                  pl.BlockSpec((tm, D_MODEL), lambda i: (i, 0)),
                  const(lnw), const(lnb), const(wrw), const(wat), const(wo), const(npost)],
        out_specs=pl.BlockSpec((tm, D_MODEL), lambda i: (i, 0)),
        compiler_params=_cparams(("parallel",)),
        name="mix_out",
    )(yf, yb, g, bon, yat, z, z, x, lnw, lnb, wrw, wat, wo, npost)


def _xattn_kernel(x_ref, kv_ref, npre, wq, wo, npost, o_ref):
    x = x_ref[...]
    h = _rms(x, npre[...])
    q = _dot(h, wq[...]) * (X_HEAD ** -0.5)
    outs = []
    for hd in range(X_HEADS):
        kh = kv_ref[:, hd * X_HEAD:(hd + 1) * X_HEAD]
        vh = kv_ref[:, X_WIDTH + hd * X_HEAD:X_WIDTH + (hd + 1) * X_HEAD]
        sc = _dot_nt(q[:, hd * X_HEAD:(hd + 1) * X_HEAD], kh)
        sc = sc - jnp.max(sc, axis=-1, keepdims=True)
        p = jnp.exp(sc)
        p = p / jnp.sum(p, axis=-1, keepdims=True)
        outs.append(_dot(p, vh))
    o = jnp.concatenate(outs, axis=1)
    c = _dot(o, wo[...])
    o_ref[...] = x + _rms(c, npost[...])


def _xattn(x, kv, npre, wq, wo, npost, layer, *, tm):
    s = x.shape[0]

    def const(arr):
        if arr.ndim == 3:
            return _layer_block(arr, layer)
        return pl.BlockSpec(arr.shape, lambda i: (0, 0))

    return pl.pallas_call(
        _xattn_kernel,
        out_shape=jax.ShapeDtypeStruct((s, D_MODEL), F32),
        grid=(s // tm,),
        in_specs=[pl.BlockSpec((tm, D_MODEL), lambda i: (i, 0)), const(kv), const(npre),
                  const(wq), const(wo), const(npost)],
        out_specs=pl.BlockSpec((tm, D_MODEL), lambda i: (i, 0)),
        compiler_params=_cparams(("parallel",)),
        name="cross_attention",
    )(x, kv, npre, wq, wo, npost)


def _ffn_kernel(x_ref, npre, wg, wu, wd, npost, o_ref, h_ref, acc_ref):
    j = pl.program_id(1)

    @pl.when(j == 0)
    def _():
        h_ref[...] = _rms(x_ref[...], npre[...]).astype(BF16)
        acc_ref[...] = jnp.zeros_like(acc_ref)

    h = h_ref[...]
    gt = jnp.dot(h, wg[...], preferred_element_type=F32)
    ut = jnp.dot(h, wu[...], preferred_element_type=F32)
    act = gt * _sigmoid(gt) * ut
    acc_ref[...] += _dot(act, wd[...])

    @pl.when(j == pl.num_programs(1) - 1)
    def _():
        o_ref[...] = x_ref[...] + _rms(acc_ref[...], npost[...])


def _ffn(x, npre, wg, wu, wd, npost, layer, *, tm, tf):
    s = x.shape[0]
    f = wg.shape[2]
    return pl.pallas_call(
        _ffn_kernel,
        out_shape=jax.ShapeDtypeStruct((s, D_MODEL), F32),
        grid=(s // tm, f // tf),
        in_specs=[pl.BlockSpec((tm, D_MODEL), lambda i, j: (i, 0)),
                  pl.BlockSpec((1, D_MODEL), lambda i, j: (0, 0)),
                  pl.BlockSpec((None, D_MODEL, tf), lambda i, j: (layer, 0, j)),
                  pl.BlockSpec((None, D_MODEL, tf), lambda i, j: (layer, 0, j)),
                  pl.BlockSpec((None, tf, D_MODEL), lambda i, j: (layer, j, 0)),
                  pl.BlockSpec((1, D_MODEL), lambda i, j: (0, 0))],
        out_specs=pl.BlockSpec((tm, D_MODEL), lambda i, j: (i, 0)),
        scratch_shapes=[pltpu.VMEM((tm, D_MODEL), BF16), pltpu.VMEM((tm, D_MODEL), F32)],
        compiler_params=_cparams(("parallel", "arbitrary")),
        name="swiglu",
    )(x, npre, wg, wu, wd, npost)


def _pack_w_in(w):
    o_lora = 3 * RW_WIDTH
    o_gd = o_lora + 2 * W_LORA + 2 * A_LORA
    o_at = o_gd + G_LORA
    o_gate = o_at + AT_Q + 2 * AT_KV
    w = w.astype(BF16)
    parts = [w[..., o_gate:o_gate + 2 * D_MODEL],
             w[..., 0:3 * RW_WIDTH],
             w[..., o_at:o_at + AT_Q],
             w[..., o_lora:o_at], jnp.zeros(w.shape[:2] + (G_PAD - G_LORA,), BF16),
             w[..., o_at + AT_Q:o_gate]]
    return jnp.concatenate(parts, axis=2)


def _pack_mu(mu):
    return jnp.concatenate([mu, jnp.zeros((G_PAD - G_LORA,), mu.dtype)])[None, :]


def _block_diag_up(up):
    z = jnp.zeros_like(up[0])
    return jnp.concatenate([jnp.concatenate([up[0], z], axis=1),
                            jnp.concatenate([z, up[1]], axis=1)], axis=0).astype(BF16)


def _rope_tables(s):
    rows = s // GRID_W
    row = jnp.repeat(jnp.arange(rows), GRID_W).astype(F32)
    col = jnp.tile(jnp.arange(GRID_W), rows).astype(F32)
    n_freq = AT_HEAD // 4
    inv = ROPE_THETA ** (-jnp.arange(n_freq, dtype=F32) / n_freq)
    ar = row[:, None] * inv
    ac = col[:, None] * inv
    cos = jnp.concatenate([jnp.cos(ar), jnp.cos(ar), jnp.cos(ac), jnp.cos(ac)], axis=1)
    sin = jnp.concatenate([-jnp.sin(ar), jnp.sin(ar), -jnp.sin(ac), jnp.sin(ac)], axis=1)
    return jnp.tile(cos, (1, 2)), jnp.tile(sin, (1, 2))


def kernel(x, mem, n_mix_pre, n_mix_post, n_x_pre, n_x_post, n_ffn_pre, n_ffn_post, n_mem, w_in, rw_mu_prev, rw_mu_next, rw_w0, rw_w_up, rw_a0, rw_a_up, rw_g_up, rw_v0, rw_v_down, rw_v_up, rw_k_k, rw_k_a, rw_r_k, rw_ln_w, rw_ln_b, w_rw_out, at_q_norm, at_k_norm, w_at_out, w_o, x_wq, x_wkv, x_wo, ffn_wg, ffn_wu, ffn_wd):
    b, s, d = x.shape
    assert b == 1 and d == D_MODEL and s % 1024 == 0
    xs = x[0]
    mems = mem[0]
    cos, sin = _rope_tables(s)
    row = lambda t: t[None, :]
    w_in_p = _pack_w_in(w_in)
    wrw_b, wat_b, wo_b = w_rw_out.astype(BF16), w_at_out.astype(BF16), w_o.astype(BF16)
    xwq_b, xwkv_b, xwo_b = x_wq.astype(BF16), x_wkv.astype(BF16), x_wo.astype(BF16)
    wg_b, wu_b, wd_b = ffn_wg.astype(BF16), ffn_wu.astype(BF16), ffn_wd.astype(BF16)
    v_first = None
    for l in range(DEPTH):
        z = _norm_proj(xs, row(n_mix_pre[l]), w_in_p, l, tm=1024, tn=1536, out_dtype=F32)

        vmix = None
        if l > 0:
            vmix = (v_first, row(rw_v0[l - 1]), rw_v_down[l - 1].astype(BF16), rw_v_up[l - 1].astype(BF16))
        gup = jnp.concatenate([rw_g_up[l], jnp.zeros((G_PAD - G_LORA, RW_WIDTH), F32)], axis=0).astype(BF16)
        r, v, kk, kd, lw, ba, g, bon = _rwprep(
            z, _pack_mu(rw_mu_prev[l]), _pack_mu(rw_mu_next[l]),
            rw_w0[l].reshape(1, 2 * RW_WIDTH), _block_diag_up(rw_w_up[l]),
            rw_a0[l].reshape(1, 2 * RW_WIDTH), _block_diag_up(rw_a_up[l]), gup,
            row(rw_k_k[l]), row(rw_k_a[l]), rw_r_k[l].reshape(1, RW_WIDTH), vmix, tm=256)
        if l == 0:
            v_first = v
        y_f, y_b = _rwrec(r, v, kk, kd, lw, ba)

        qn = jnp.tile(at_q_norm[l], 2)[None, :]
        kn = jnp.tile(at_k_norm[l], 2)[None, :]
        q_t, k_blk, v_t = _atprep(z, cos, sin, qn, kn, tm=512)
        y_at = _flash(q_t, k_blk, v_t, tq=256)

        xs = _mix_out(y_f, y_b, g, bon, y_at, z, xs, row(rw_ln_w[l]), row(rw_ln_b[l]),
                      wrw_b, wat_b, wo_b, row(n_mix_post[l]), l, tm=256)

        kv = _norm_proj(mems, row(n_mem[l]), xwkv_b, l, tm=mems.shape[0], tn=512, out_dtype=BF16)
        xs = _xattn(xs, kv, row(n_x_pre[l]), xwq_b, xwo_b, row(n_x_post[l]), l, tm=512)

        xs = _ffn(xs, row(n_ffn_pre[l]), wg_b, wu_b, wd_b, row(n_ffn_post[l]), l, tm=512, tf=512)
    return xs[None]
```

```python
import functools

import jax
import jax.numpy as jnp
from jax import lax
from jax.experimental import pallas as pl
from jax.experimental.pallas import tpu as pltpu

F32 = jnp.float32
BF16 = jnp.bfloat16

D_MODEL = 2048
DEPTH = 4
GRID_W = 64
EPS = 1e-6
RW_HEAD = 64
RW_WIDTH = 1024
W_LORA = 64
A_LORA = 64
G_LORA = 160
RW_GN_EPS = 64e-5
AT_HEADS = 16
AT_KV_HEADS = 4
AT_HEAD = 64
AT_Q = AT_HEADS * AT_HEAD
AT_KV = AT_KV_HEADS * AT_HEAD
ROPE_THETA = 10000.0
X_HEADS = 4
X_HEAD = 128
X_WIDTH = X_HEADS * X_HEAD
D_FF = 5632

LANES = 128
SUBLANES = 8
CHUNK = 64
SUB = 16
LOG2E = 1.4426950408889634
V_AUG = AT_HEAD + 16
HALO = 16
G_PAD = 256

COL_G1 = 0
COL_G2 = 2048
COL_R = 4096
COL_K = 5120
COL_V = 6144
COL_Q = 7168
COL_LORA = 8192
COL_ATKV = 8704
IN_PACKED = 9216

VMEM_LIMIT = 56 * 1024 * 1024


def _cparams(sem):
    return pltpu.CompilerParams(dimension_semantics=sem, vmem_limit_bytes=VMEM_LIMIT)


def _dot(a, b):
    return jnp.dot(a.astype(BF16), b.astype(BF16), preferred_element_type=F32)


def _dot_nt(a, b):
    return lax.dot_general(a.astype(BF16), b.astype(BF16), (((1,), (1,)), ((), ())),
                           preferred_element_type=F32)


def _split2(x):
    hi = x.astype(BF16)
    lo = (x - hi.astype(F32)).astype(BF16)
    return hi, lo


def _split3(x):
    hi = x.astype(BF16)
    r1 = x - hi.astype(F32)
    mid = r1.astype(BF16)
    lo = (r1 - mid.astype(F32)).astype(BF16)
    return hi, mid, lo


def _sigmoid(x):
    return 1.0 / (1.0 + jnp.exp(-x))


def _softplus(x):
    return jnp.maximum(x, 0.0) + jnp.log(1.0 + jnp.exp(-jnp.abs(x)))


def _head_ones():
    r = lax.broadcasted_iota(jnp.int32, (LANES, LANES), 0) // RW_HEAD
    c = lax.broadcasted_iota(jnp.int32, (LANES, LANES), 1) // RW_HEAD
    return jnp.where(r == c, 1.0, 0.0).astype(BF16)


def _head_sum(x, ones):
    outs = []
    for s in range(x.shape[1] // LANES):
        hi, lo = _split2(x[:, s * LANES:(s + 1) * LANES])
        outs.append(jnp.dot(hi, ones, preferred_element_type=F32)
                    + jnp.dot(lo, ones, preferred_element_type=F32))
    return outs[0] if len(outs) == 1 else jnp.concatenate(outs, axis=1)


def _rms(x, gain):
    ms = jnp.mean(x * x, axis=-1, keepdims=True)
    return x * lax.rsqrt(ms + EPS) * gain


def _norm_proj_kernel(x_ref, g_ref, w_ref, o_ref, h_ref):
    @pl.when(pl.program_id(1) == 0)
    def _():
        h_ref[...] = _rms(x_ref[...], g_ref[...]).astype(BF16)

    o_ref[...] = jnp.dot(h_ref[...], w_ref[...], preferred_element_type=F32).astype(o_ref.dtype)


def _norm_proj(x, gain, w, layer, *, tm, tn, out_dtype):
    m, d = x.shape
    n = w.shape[2]
    return pl.pallas_call(
        _norm_proj_kernel,
        out_shape=jax.ShapeDtypeStruct((m, n), out_dtype),
        grid=(m // tm, n // tn),
        in_specs=[pl.BlockSpec((tm, d), lambda i, j: (i, 0)),
                  pl.BlockSpec((1, d), lambda i, j: (0, 0)),
                  pl.BlockSpec((None, d, tn), lambda i, j: (layer, 0, j))],
        out_specs=pl.BlockSpec((tm, tn), lambda i, j: (i, j)),
        scratch_shapes=[pltpu.VMEM((tm, d), BF16)],
        compiler_params=_cparams(("parallel", "arbitrary")),
        name="norm_proj",
    )(x, gain, w)


def _shift(z, prev_row, next_row, mu_p, mu_n):
    rows = z.shape[0]
    ridx = lax.broadcasted_iota(jnp.int32, z.shape, 0)
    zp = jnp.where(ridx == 0, prev_row, pltpu.roll(z, 1, 0))
    zn = jnp.where(ridx == rows - 1, next_row, pltpu.roll(z, rows - 1, 0))
    return z + mu_p * (zp - z) + mu_n * (zn - z)


def _rwprep_kernel(has_vmix, *refs):
    (zr, zk, zv, zl, pr, pk, pv, pL, nr, nk, nv, nL, mup, mun, w0, wup, a0, aup, gup,
     kk_w, ka_w, rk_w) = refs[:22]
    pos = 22
    if has_vmix:
        vfirst, v0, vdown, vup = refs[pos:pos + 4]
        pos += 4
    r_o, v_o, kk_o, kd_o, lw_o, ba_o, g_o, bon_o = refs[pos:pos + 8]

    i = pl.program_id(0)
    last = pl.num_programs(0) - 1
    keep_p = jnp.where(i == 0, 0.0, 1.0)
    keep_n = jnp.where(i == last, 0.0, 1.0)

    def shifted(z_ref, p_ref, n_ref, lo, hi):
        p_row = p_ref[...].astype(F32)[HALO - 1:HALO, :] * keep_p
        n_row = n_ref[...].astype(F32)[0:1, :] * keep_n
        return _shift(z_ref[...].astype(F32), p_row, n_row, mup[:, lo:hi], mun[:, lo:hi])

    r = shifted(zr, pr, nr, 0, 1024)
    k = shifted(zk, pk, nk, 1024, 2048)
    v = shifted(zv, pv, nv, 2048, 3072)
    lora = shifted(zl, pL, nL, 3072, 3584)

    if has_vmix:
        mix = _sigmoid(v0[...] + _dot(_dot(v, vdown[...]), vup[...]))
        v = v + (vfirst[...].astype(F32) - v) * mix

    u = w0[...] + _dot(jnp.tanh(lora[:, 0:128]), wup[...])
    w_log = -_softplus(-u) - 0.5
    lw = -jnp.exp(w_log)
    a = _sigmoid(a0[...] + _dot(lora[:, 128:256], aup[...]))
    g = _dot(_sigmoid(lora[:, 256:512]), gup[...])

    ones = _head_ones()
    kk = k * kk_w[...]
    norm = jnp.sqrt(_head_sum(kk * kk, ones))
    kk = kk / jnp.maximum(norm, 1e-12)
    ka = ka_w[...]
    a_f = a[:, :RW_WIDTH]
    a_b = a[:, RW_WIDTH:]
    kd_f = k * (1.0 + (a_f - 1.0) * ka)
    kd_b = k * (1.0 + (a_b - 1.0) * ka)
    bonus = _head_sum(r * rk_w[...] * (kd_f + kd_b), ones) * v

    r_o[...] = r.astype(r_o.dtype)
    v_o[...] = v.astype(v_o.dtype)
    kk_o[...] = kk.astype(kk_o.dtype)
    kd_o[0] = kd_f.astype(kd_o.dtype)
    kd_o[1] = kd_b.astype(kd_o.dtype)
    lw_o[0] = lw[:, :RW_WIDTH]
    lw_o[1] = lw[:, RW_WIDTH:]
    ba_o[0] = (kk * a_f).astype(ba_o.dtype)
    ba_o[1] = (kk * a_b).astype(ba_o.dtype)
    g_o[...] = g.astype(g_o.dtype)
    bon_o[...] = bonus.astype(bon_o.dtype)


def _rwprep(z, mup, mun, w0, wup, a0, aup, gup, kk_w, ka_w, rk_w, vmix, *, tm):
    s = z.shape[0]
    nbh = s // HALO
    tb = tm // HALO
    has_vmix = vmix is not None

    def main(width, cb):
        return pl.BlockSpec((tm, width), lambda i: (i, cb))

    def prev(width, cb):
        return pl.BlockSpec((HALO, width), lambda i: (jnp.maximum(i * tb - 1, 0), cb))

    def nxt(width, cb):
        return pl.BlockSpec((HALO, width), lambda i: (jnp.minimum((i + 1) * tb, nbh - 1), cb))

    def full(arr):
        nd = arr.ndim
        return pl.BlockSpec(arr.shape, lambda i: (0,) * nd)

    cols = [(1024, COL_R // 1024), (1024, COL_K // 1024), (1024, COL_V // 1024), (512, COL_LORA // 512)]
    in_specs = ([main(w, c) for w, c in cols] + [prev(w, c) for w, c in cols]
                + [nxt(w, c) for w, c in cols])
    args = [z] * 12
    consts = [mup, mun, w0, wup, a0, aup, gup, kk_w, ka_w, rk_w]
    in_specs += [full(c) for c in consts]
    args += consts
    if has_vmix:
        vfirst, v0, vdown, vup = vmix
        in_specs += [pl.BlockSpec((tm, RW_WIDTH), lambda i: (i, 0)), full(v0), full(vdown), full(vup)]
        args += [vfirst, v0, vdown, vup]

    one = jax.ShapeDtypeStruct((s, RW_WIDTH), BF16)
    two = jax.ShapeDtypeStruct((2, s, RW_WIDTH), BF16)
    two_f32 = jax.ShapeDtypeStruct((2, s, RW_WIDTH), F32)
    spec1 = pl.BlockSpec((tm, RW_WIDTH), lambda i: (i, 0))
    spec2 = pl.BlockSpec((2, tm, RW_WIDTH), lambda i: (0, i, 0))
    return pl.pallas_call(
        functools.partial(_rwprep_kernel, has_vmix),
        out_shape=[one, one, one, two, two_f32, two, one, one],
        grid=(s // tm,),
        in_specs=in_specs,
        out_specs=[spec1, spec1, spec1, spec2, spec2, spec2, spec1, spec1],
        compiler_params=_cparams(("parallel",)),
        name="rwkv_prep",
    )(*args)


def _rwrec_kernel(rf_ref, vf_ref, kkf_ref, kdf_ref, lwf_ref, baf_ref,
                  rb_ref, vb_ref, kkb_ref, kdb_ref, lwb_ref, bab_ref, yf_ref, yb_ref, s_ref):
    T = CHUNK
    T2 = 2 * T
    n_slab = RW_WIDTH // LANES

    @pl.when(pl.program_id(0) == 0)
    def _():
        s_ref[...] = jnp.zeros_like(s_ref)

    row = lax.broadcasted_iota(jnp.int32, (T, T), 0)
    col = lax.broadcasted_iota(jnp.int32, (T, T), 1)
    R2 = lax.broadcasted_iota(jnp.int32, (T2, T2), 0)
    C2 = lax.broadcasted_iota(jnp.int32, (T2, T2), 1)
    same = (R2 // T) == (C2 // T)
    tdiff = (R2 % T) - (C2 % T)
    mask_d = same & (((R2 % T) // SUB) == ((C2 % T) // SUB))
    eye = R2 == C2
    head0 = lax.broadcasted_iota(jnp.int32, (T, LANES), 1) < RW_HEAD

    def stack_heads(x):
        return jnp.concatenate([jnp.where(head0, x, 0.0), jnp.where(head0, 0.0, x)], axis=0)

    def stack_dup(x):
        return jnp.concatenate([x, x], axis=0)

    cols = [slice(sl * LANES, (sl + 1) * LANES) for sl in range(n_slab)]
    ar2, bk2, v2, bke2, mask_a, mask_y, e_tot = [], [], [], [], [], [], []
    for sgn, (r_ref, v_ref, kk_ref, kd_ref, lw_ref, ba_ref) in (
            (1, (rf_ref, vf_ref, kkf_ref, kdf_ref, lwf_ref, baf_ref)),
            (-1, (rb_ref, vb_ref, kkb_ref, kdb_ref, lwb_ref, bab_ref))):
        tri = jnp.where((row - col) * sgn >= 0, 1.0, 0.0).astype(BF16)
        lw = lw_ref[...]
        hi, mid, lo = _split3(lw)
        cin = (jnp.dot(tri, hi, preferred_element_type=F32)
               + jnp.dot(tri, mid, preferred_element_type=F32)
               + jnp.dot(tri, lo, preferred_element_type=F32))
        ctot = jnp.sum(lw, axis=0, keepdims=True)
        e_in = jnp.exp(cin)
        e_ex = jnp.exp(cin - lw)
        e_neg = jnp.exp(-cin)
        e_end = jnp.exp(ctot - cin)
        etot = jnp.exp(ctot)
        kd = kd_ref[...].astype(F32)
        ba = ba_ref[...].astype(F32)
        r_t = r_ref[...].astype(F32) * e_in
        a_t = -kk_ref[...].astype(F32) * e_ex
        b_t = ba * e_neg
        k_t = kd * e_neg
        b_e = ba * e_end
        k_e = kd * e_end
        v = v_ref[...].astype(F32)
        m_a = same & (tdiff * sgn > 0)
        m_y = same & (tdiff * sgn >= 0)
        for cs in cols:
            ar2.append(jnp.concatenate([stack_heads(a_t[:, cs]), stack_heads(r_t[:, cs])],
                                       axis=0).astype(BF16))
            bk2.append(jnp.concatenate([stack_dup(b_t[:, cs]), stack_dup(k_t[:, cs])],
                                       axis=0).astype(BF16))
            v2.append(stack_heads(v[:, cs]).astype(BF16))
            bke2.append(jnp.concatenate([stack_heads(b_e[:, cs]), stack_heads(k_e[:, cs])],
                                        axis=0).astype(BF16))
            mask_a.append(m_a)
            mask_y.append(m_y)
            e_tot.append(etot[:, cs])

    slabs = range(2 * n_slab)
    sc = [_dot_nt(ar2[s], bk2[s]) for s in slabs]
    n_ab = [jnp.where(mask_a[s], sc[s][:T2, :T2], 0.0) for s in slabs]
    n_ak = [jnp.where(mask_a[s], sc[s][:T2, T2:], 0.0) for s in slabs]
    m_rbk = [jnp.concatenate([jnp.where(mask_y[s], sc[s][T2:, :T2], 0.0),
                              jnp.where(mask_y[s], sc[s][T2:, T2:], 0.0)], axis=1).astype(BF16)
             for s in slabs]

    def dot2(lhs, a, b):
        t = _dot(lhs, jnp.concatenate([a, b], axis=1))
        return t[:, :T2], t[:, T2:]

    p = [jnp.where(mask_d, n_ab[s], 0.0) for s in slabs]
    q = [jnp.where(eye, 1.0, p[s]) for s in slabs]
    p = [_dot(p[s], p[s]) for s in slabs]
    for _ in range(2):
        pq = [dot2(p[s], p[s], q[s]) for s in slabs]
        p = [pq[s][0] for s in slabs]
        q = [q[s] + pq[s][1] for s in slabs]
    q = [q[s] + _dot(p[s], q[s]) for s in slabs]
    akv = [_dot(n_ak[s], v2[s]) for s in slabs]

    st = [s_ref[s // n_slab, s % n_slab] for s in slabs]
    ars = [_dot_nt(ar2[s], st[s]) for s in slabs]
    mx = [dot2(q[s], jnp.where(mask_d, 0.0, n_ab[s]), ars[s][:T2] + akv[s]) for s in slabs]
    m = [mx[s][0] for s in slabs]
    x = [mx[s][1] for s in slabs]
    mx = [dot2(m[s], m[s], x[s]) for s in slabs]
    x = [x[s] + mx[s][1] for s in slabs]
    x = [x[s] + _dot(mx[s][0], x[s]) for s in slabs]
    for s in slabs:
        y2 = ars[s][T2:] + _dot(m_rbk[s], jnp.concatenate([x[s].astype(BF16), v2[s]], axis=0))
        y_ref = yf_ref if s < n_slab else yb_ref
        y_ref[:, cols[s % n_slab]] = y2[:T] + y2[T:]
    for s in slabs:
        upd = _dot(jnp.concatenate([x[s].T, v2[s].astype(F32).T], axis=1), bke2[s])
        s_ref[s // n_slab, s % n_slab] = st[s] * e_tot[s] + upd


def _rwrec(r, v, kk, kd, lw, ba):
    s = r.shape[0]
    nc = s // CHUNK
    fwd1 = pl.BlockSpec((CHUNK, RW_WIDTH), lambda c: (c, 0))
    bwd1 = pl.BlockSpec((CHUNK, RW_WIDTH), lambda c: (nc - 1 - c, 0))
    fwd2 = pl.BlockSpec((None, CHUNK, RW_WIDTH), lambda c: (0, c, 0))
    bwd2 = pl.BlockSpec((None, CHUNK, RW_WIDTH), lambda c: (1, nc - 1 - c, 0))
    out = jax.ShapeDtypeStruct((s, RW_WIDTH), F32)
    return pl.pallas_call(
        _rwrec_kernel,
        out_shape=[out, out],
        grid=(nc,),
        in_specs=[fwd1, fwd1, fwd1, fwd2, fwd2, fwd2, bwd1, bwd1, bwd1, bwd2, bwd2, bwd2],
        out_specs=[fwd1, bwd1],
        scratch_shapes=[pltpu.VMEM((2, RW_WIDTH // LANES, LANES, LANES), F32)],
        compiler_params=_cparams(("arbitrary",)),
        name="rwkv_recurrence",
    )(r, v, kk, kd, lw, ba, r, v, kk, kd, lw, ba)


def _atprep_kernel(zq, zkv, cos_ref, sin_ref, qn, kn, qt_o, k_o, vt_o):
    ones = _head_ones()
    cos = cos_ref[...]
    sin = sin_ref[...]
    lane = lax.broadcasted_iota(jnp.int32, cos.shape, 1)
    even = ((lane // 16) % 2) == 0

    def norm_rope(t, gain):
        ms = _head_sum(t * t, ones) * (1.0 / AT_HEAD)
        tn = t * lax.rsqrt(ms + EPS) * gain
        swapped = jnp.where(even, pltpu.roll(tn, LANES - 16, 1), pltpu.roll(tn, 16, 1))
        return tn * cos + swapped * sin

    q_scale = (AT_HEAD ** -0.5) * LOG2E
    tm = zq.shape[0]
    ones_rows = jnp.where(lax.broadcasted_iota(jnp.int32, (V_AUG - AT_HEAD, tm), 0) == 0, 1.0, 0.0).astype(BF16)
    for sl in range(AT_Q // LANES):
        out = norm_rope(zq[:, sl * LANES:(sl + 1) * LANES].astype(F32), qn[...]) * q_scale
        out_t = out.T.astype(BF16)
        qt_o[2 * sl] = out_t[:AT_HEAD]
        qt_o[2 * sl + 1] = out_t[AT_HEAD:]
    for sl in range(AT_KV // LANES):
        out = norm_rope(zkv[:, sl * LANES:(sl + 1) * LANES].astype(F32), kn[...])
        k_o[2 * sl] = out[:, :AT_HEAD].astype(BF16)
        k_o[2 * sl + 1] = out[:, AT_HEAD:].astype(BF16)
        vt = zkv[:, AT_KV + sl * LANES:AT_KV + (sl + 1) * LANES].astype(F32).T.astype(BF16)
        vt_o[2 * sl] = jnp.concatenate([vt[:AT_HEAD], ones_rows], axis=0)
        vt_o[2 * sl + 1] = jnp.concatenate([vt[AT_HEAD:], ones_rows], axis=0)


def _atprep(z, cos, sin, qn, kn, *, tm):
    s = z.shape[0]
    nb = s // tm
    return pl.pallas_call(
        _atprep_kernel,
        out_shape=[jax.ShapeDtypeStruct((AT_HEADS, AT_HEAD, s), BF16),
                   jax.ShapeDtypeStruct((AT_KV_HEADS, nb, tm, AT_HEAD), BF16),
                   jax.ShapeDtypeStruct((AT_KV_HEADS, nb, V_AUG, tm), BF16)],
        grid=(nb,),
        in_specs=[pl.BlockSpec((tm, AT_Q), lambda i: (i, COL_Q // AT_Q)),
                  pl.BlockSpec((tm, 2 * AT_KV), lambda i: (i, COL_ATKV // (2 * AT_KV))),
                  pl.BlockSpec((tm, LANES), lambda i: (i, 0)),
                  pl.BlockSpec((tm, LANES), lambda i: (i, 0)),
                  pl.BlockSpec((1, LANES), lambda i: (0, 0)),
                  pl.BlockSpec((1, LANES), lambda i: (0, 0))],
        out_specs=[pl.BlockSpec((AT_HEADS, AT_HEAD, tm), lambda i: (0, 0, i)),
                   pl.BlockSpec((AT_KV_HEADS, None, tm, AT_HEAD), lambda i: (0, i, 0, 0)),
                   pl.BlockSpec((AT_KV_HEADS, None, V_AUG, tm), lambda i: (0, i, 0, 0))],
        compiler_params=_cparams(("parallel",)),
        name="attn_prep",
    )(z, z, cos, sin, qn, kn)


def _flash_kernel(qt_ref, k_ref, vt_ref, o_ref, m_ref, acc_ref, sa_ref, sb_ref):
    grp = qt_ref.shape[0]
    nkb = k_ref.shape[0]
    m_ref[...] = jnp.full_like(m_ref, -jnp.inf)
    acc_ref[...] = jnp.zeros_like(acc_ref)

    def step(kb_cur, cur_ref, kb_next, next_ref):
        k_next = k_ref[kb_next]
        vt = vt_ref[kb_cur]
        for g in range(grp):
            next_ref[g] = jnp.dot(k_next, qt_ref[g], preferred_element_type=F32)
            st = cur_ref[g]
            m_prev = m_ref[g]
            m_new = jnp.maximum(m_prev, jnp.max(st, axis=0, keepdims=True))
            alpha = jnp.exp2(m_prev - m_new)
            pt = jnp.exp2(st - m_new).astype(BF16)
            m_ref[g] = m_new
            acc_ref[g] = alpha * acc_ref[g] + jnp.dot(vt, pt, preferred_element_type=F32)

    k0 = k_ref[0]
    for g in range(grp):
        sa_ref[g] = jnp.dot(k0, qt_ref[g], preferred_element_type=F32)

    unroll = max(u for u in (8, 4, 2) if nkb % u == 0)

    def body(j, carry):
        kb = unroll * j
        for u in range(0, unroll, 2):
            step(kb + u, sa_ref, kb + u + 1, sb_ref)
            step(kb + u + 1, sb_ref, jnp.minimum(kb + u + 2, nkb - 1), sa_ref)
        return carry

    lax.fori_loop(0, nkb // unroll, body, 0)
    for pair in range(grp // 2):
        a0 = acc_ref[2 * pair]
        a1 = acc_ref[2 * pair + 1]
        o2 = jnp.concatenate([a0[:AT_HEAD] / a0[AT_HEAD:AT_HEAD + 1],
                              a1[:AT_HEAD] / a1[AT_HEAD:AT_HEAD + 1]], axis=0)
        o_ref[:, pair * LANES:(pair + 1) * LANES] = o2.T.astype(o_ref.dtype)


def _flash(qt, k, vt, *, tq):
    s = qt.shape[2]
    _, nkb, tk, _ = k.shape
    grp = AT_HEADS // AT_KV_HEADS
    return pl.pallas_call(
        _flash_kernel,
        out_shape=jax.ShapeDtypeStruct((s, AT_Q), BF16),
        grid=(AT_KV_HEADS, s // tq),
        in_specs=[pl.BlockSpec((grp, AT_HEAD, tq), lambda h, i: (h, 0, i)),
                  pl.BlockSpec((None, nkb, tk, AT_HEAD), lambda h, i: (h, 0, 0, 0)),
                  pl.BlockSpec((None, nkb, V_AUG, tk), lambda h, i: (h, 0, 0, 0))],
        out_specs=pl.BlockSpec((tq, grp * AT_HEAD), lambda h, i: (i, h)),
        scratch_shapes=[pltpu.VMEM((grp, 1, tq), F32), pltpu.VMEM((grp, V_AUG, tq), F32),
                        pltpu.VMEM((grp, tk, tq), F32), pltpu.VMEM((grp, tk, tq), F32)],
        compiler_params=_cparams(("parallel", "parallel")),
        name="flash_attention",
    )(qt, k, vt)


def _mix_out_kernel(yf_ref, yb_ref, g_ref, bon_ref, yat_ref, zg1_ref, zg2_ref, x_ref, lnw, lnb,
                    wrw, wat, wo, npost, o_ref):
    ones = _head_ones()
    y = yf_ref[...] + yb_ref[...]
    inv = 1.0 / RW_HEAD
    mu = _head_sum(y, ones) * inv
    dlt = y - mu
    var = _head_sum(dlt * dlt, ones) * inv
    yn = dlt * lax.rsqrt(var + RW_GN_EPS) * lnw[...] + lnb[...]
    y_rw = (yn + bon_ref[...].astype(F32)) * g_ref[...].astype(F32)
    pa = _dot(y_rw, wrw[...])
    pb = jnp.dot(yat_ref[...], wat[...], preferred_element_type=F32)
    merged = _sigmoid(zg1_ref[...].astype(F32)) * pa + _sigmoid(zg2_ref[...].astype(F32)) * pb
    out = _dot(merged, wo[...])
    o_ref[...] = x_ref[...] + _rms(out, npost[...])


def _layer_block(arr, layer, **kw):
    return pl.BlockSpec((None,) + arr.shape[1:], lambda *_: (layer, 0, 0), **kw)


def _mix_out(yf, yb, g, bon, yat, z, x, lnw, lnb, wrw, wat, wo, npost, layer, *, tm):
    s = x.shape[0]

    def const(arr):
        if arr.ndim == 3:
            return _layer_block(arr, layer, pipeline_mode=pl.Buffered(1))
        return pl.BlockSpec(arr.shape, lambda i: (0, 0), pipeline_mode=pl.Buffered(1))

    row1k = pl.BlockSpec((tm, RW_WIDTH), lambda i: (i, 0))
    return pl.pallas_call(
        _mix_out_kernel,
        out_shape=jax.ShapeDtypeStruct((s, D_MODEL), F32),
        grid=(s // tm,),
        in_specs=[row1k, row1k, row1k, row1k, row1k,
                  pl.BlockSpec((tm, D_MODEL), lambda i: (i, COL_G1 // D_MODEL)),
                  pl.BlockSpec((tm, D_MODEL), lambda i: (i, COL_G2 // D_MODEL)),
                  pl.BlockSpec((tm, D_MODEL), lambda i: (i, 0)),
                  const(lnw), const(lnb), const(wrw), const(wat), const(wo), const(npost)],
        out_specs=pl.BlockSpec((tm, D_MODEL), lambda i: (i, 0)),
        compiler_params=_cparams(("parallel",)),
        name="mix_out",
    )(yf, yb, g, bon, yat, z, z, x, lnw, lnb, wrw, wat, wo, npost)


def _xattn_kernel(x_ref, kv_ref, npre, wq, wo, npost, o_ref):
    x = x_ref[...]
    h = _rms(x, npre[...])
    q = _dot(h, wq[...]) * (X_HEAD ** -0.5)
    outs = []
    for hd in range(X_HEADS):
        kh = kv_ref[:, hd * X_HEAD:(hd + 1) * X_HEAD]
        vh = kv_ref[:, X_WIDTH + hd * X_HEAD:X_WIDTH + (hd + 1) * X_HEAD]
        sc = _dot_nt(q[:, hd * X_HEAD:(hd + 1) * X_HEAD], kh)
        sc = sc - jnp.max(sc, axis=-1, keepdims=True)
        p = jnp.exp(sc)
        p = p / jnp.sum(p, axis=-1, keepdims=True)
        outs.append(_dot(p, vh))
    o = jnp.concatenate(outs, axis=1)
    c = _dot(o, wo[...])
    o_ref[...] = x + _rms(c, npost[...])


def _xattn(x, kv, npre, wq, wo, npost, layer, *, tm):
    s = x.shape[0]

    def const(arr):
        if arr.ndim == 3:
            return _layer_block(arr, layer)
        return pl.BlockSpec(arr.shape, lambda i: (0, 0))

    return pl.pallas_call(
        _xattn_kernel,
        out_shape=jax.ShapeDtypeStruct((s, D_MODEL), F32),
        grid=(s // tm,),
        in_specs=[pl.BlockSpec((tm, D_MODEL), lambda i: (i, 0)), const(kv), const(npre),
                  const(wq), const(wo), const(npost)],
        out_specs=pl.BlockSpec((tm, D_MODEL), lambda i: (i, 0)),
        compiler_params=_cparams(("parallel",)),
        name="cross_attention",
    )(x, kv, npre, wq, wo, npost)


def _ffn_kernel(x_ref, npre, wg, wu, wd, npost, o_ref, h_ref, acc_ref):
    j = pl.program_id(1)

    @pl.when(j == 0)
    def _():
        h_ref[...] = _rms(x_ref[...], npre[...]).astype(BF16)
        acc_ref[...] = jnp.zeros_like(acc_ref)

    h = h_ref[...]
    gt = jnp.dot(h, wg[...], preferred_element_type=F32)
    ut = jnp.dot(h, wu[...], preferred_element_type=F32)
    act = gt * _sigmoid(gt) * ut
    acc_ref[...] += _dot(act, wd[...])

    @pl.when(j == pl.num_programs(1) - 1)
    def _():
        o_ref[...] = x_ref[...] + _rms(acc_ref[...], npost[...])


def _ffn(x, npre, wg, wu, wd, npost, layer, *, tm, tf):
    s = x.shape[0]
    f = wg.shape[2]
    return pl.pallas_call(
        _ffn_kernel,
        out_shape=jax.ShapeDtypeStruct((s, D_MODEL), F32),
        grid=(s // tm, f // tf),
        in_specs=[pl.BlockSpec((tm, D_MODEL), lambda i, j: (i, 0)),
                  pl.BlockSpec((1, D_MODEL), lambda i, j: (0, 0)),
                  pl.BlockSpec((None, D_MODEL, tf), lambda i, j: (layer, 0, j)),
                  pl.BlockSpec((None, D_MODEL, tf), lambda i, j: (layer, 0, j)),
                  pl.BlockSpec((None, tf, D_MODEL), lambda i, j: (layer, j, 0)),
                  pl.BlockSpec((1, D_MODEL), lambda i, j: (0, 0))],
        out_specs=pl.BlockSpec((tm, D_MODEL), lambda i, j: (i, 0)),
        scratch_shapes=[pltpu.VMEM((tm, D_MODEL), BF16), pltpu.VMEM((tm, D_MODEL), F32)],
        compiler_params=_cparams(("parallel", "arbitrary")),
        name="swiglu",
    )(x, npre, wg, wu, wd, npost)


def _pack_w_in_kernel(w_ref, o_ref):
    o_lora = 3 * RW_WIDTH
    o_at = o_lora + 2 * W_LORA + 2 * A_LORA + G_LORA
    o_gate = o_at + AT_Q + 2 * AT_KV
    n_lora = o_at - o_lora
    rows = o_ref.shape[0]
    o_ref[:, COL_G1:COL_G1 + 2 * D_MODEL] = w_ref[:, o_gate:o_gate + 2 * D_MODEL].astype(BF16)
    o_ref[:, COL_R:COL_R + 3 * RW_WIDTH] = w_ref[:, 0:3 * RW_WIDTH].astype(BF16)
    o_ref[:, COL_Q:COL_Q + AT_Q] = w_ref[:, o_at:o_at + AT_Q].astype(BF16)
    o_ref[:, COL_LORA:COL_LORA + n_lora] = w_ref[:, o_lora:o_at].astype(BF16)
    o_ref[:, COL_LORA + n_lora:COL_ATKV] = jnp.zeros((rows, COL_ATKV - COL_LORA - n_lora), BF16)
    o_ref[:, COL_ATKV:IN_PACKED] = w_ref[:, o_at + AT_Q:o_gate].astype(BF16)


def _pack_w_in(w, *, tr):
    nl, d, n_in = w.shape
    return pl.pallas_call(
        _pack_w_in_kernel,
        out_shape=jax.ShapeDtypeStruct((nl, d, IN_PACKED), BF16),
        grid=(nl, d // tr),
        in_specs=[pl.BlockSpec((None, tr, n_in), lambda l, i: (l, i, 0))],
        out_specs=pl.BlockSpec((None, tr, IN_PACKED), lambda l, i: (l, i, 0)),
        compiler_params=_cparams(("parallel", "parallel")),
        name="pack_w_in",
    )(w)


def _pack_mu(mu):
    return jnp.concatenate([mu, jnp.zeros((G_PAD - G_LORA,), mu.dtype)])[None, :]


def _block_diag_up(up):
    z = jnp.zeros_like(up[0])
    return jnp.concatenate([jnp.concatenate([up[0], z], axis=1),
                            jnp.concatenate([z, up[1]], axis=1)], axis=0).astype(BF16)


def _rope_tables(s):
    rows = s // GRID_W
    row = jnp.repeat(jnp.arange(rows), GRID_W).astype(F32)
    col = jnp.tile(jnp.arange(GRID_W), rows).astype(F32)
    n_freq = AT_HEAD // 4
    inv = ROPE_THETA ** (-jnp.arange(n_freq, dtype=F32) / n_freq)
    ar = row[:, None] * inv
    ac = col[:, None] * inv
    cos = jnp.concatenate([jnp.cos(ar), jnp.cos(ar), jnp.cos(ac), jnp.cos(ac)], axis=1)
    sin = jnp.concatenate([-jnp.sin(ar), jnp.sin(ar), -jnp.sin(ac), jnp.sin(ac)], axis=1)
    return jnp.tile(cos, (1, 2)), jnp.tile(sin, (1, 2))


def kernel(x, mem, n_mix_pre, n_mix_post, n_x_pre, n_x_post, n_ffn_pre, n_ffn_post, n_mem, w_in, rw_mu_prev, rw_mu_next, rw_w0, rw_w_up, rw_a0, rw_a_up, rw_g_up, rw_v0, rw_v_down, rw_v_up, rw_k_k, rw_k_a, rw_r_k, rw_ln_w, rw_ln_b, w_rw_out, at_q_norm, at_k_norm, w_at_out, w_o, x_wq, x_wkv, x_wo, ffn_wg, ffn_wu, ffn_wd):
    b, s, d = x.shape
    assert b == 1 and d == D_MODEL and s % 1024 == 0
    xs = x[0]
    mems = mem[0]
    cos, sin = _rope_tables(s)
    row = lambda t: t[None, :]
    w_in_p = _pack_w_in(w_in, tr=256)
    wrw_b, wat_b, wo_b = w_rw_out.astype(BF16), w_at_out.astype(BF16), w_o.astype(BF16)
    xwq_b, xwkv_b, xwo_b = x_wq.astype(BF16), x_wkv.astype(BF16), x_wo.astype(BF16)
    wg_b, wu_b, wd_b = ffn_wg.astype(BF16), ffn_wu.astype(BF16), ffn_wd.astype(BF16)
    v_first = None
    for l in range(DEPTH):
        z = _norm_proj(xs, row(n_mix_pre[l]), w_in_p, l, tm=1024, tn=2304, out_dtype=BF16)

        vmix = None
        if l > 0:
            vmix = (v_first, row(rw_v0[l - 1]), rw_v_down[l - 1].astype(BF16), rw_v_up[l - 1].astype(BF16))
        gup = jnp.concatenate([rw_g_up[l], jnp.zeros((G_PAD - G_LORA, RW_WIDTH), F32)], axis=0).astype(BF16)
        r, v, kk, kd, lw, ba, g, bon = _rwprep(
            z, _pack_mu(rw_mu_prev[l]), _pack_mu(rw_mu_next[l]),
            rw_w0[l].reshape(1, 2 * RW_WIDTH), _block_diag_up(rw_w_up[l]),
            rw_a0[l].reshape(1, 2 * RW_WIDTH), _block_diag_up(rw_a_up[l]), gup,
            row(rw_k_k[l]), row(rw_k_a[l]), rw_r_k[l].reshape(1, RW_WIDTH), vmix, tm=256)
        if l == 0:
            v_first = v
        y_f, y_b = _rwrec(r, v, kk, kd, lw, ba)

        qn = jnp.tile(at_q_norm[l], 2)[None, :]
        kn = jnp.tile(at_k_norm[l], 2)[None, :]
        q_t, k_blk, v_t = _atprep(z, cos, sin, qn, kn, tm=512)
        y_at = _flash(q_t, k_blk, v_t, tq=256)

        xs = _mix_out(y_f, y_b, g, bon, y_at, z, xs, row(rw_ln_w[l]), row(rw_ln_b[l]),
                      wrw_b, wat_b, wo_b, row(n_mix_post[l]), l, tm=256)

        kv = _norm_proj(mems, row(n_mem[l]), xwkv_b, l, tm=mems.shape[0], tn=512, out_dtype=BF16)
        xs = _xattn(xs, kv, row(n_x_pre[l]), xwq_b, xwo_b, row(n_x_post[l]), l, tm=512)

        xs = _ffn(xs, row(n_ffn_pre[l]), wg_b, wu_b, wd_b, row(n_ffn_post[l]), l, tm=512, tf=512)
    return xs[None]
```

```python
import functools

import jax
import jax.numpy as jnp
from jax import lax
from jax.experimental import pallas as pl
from jax.experimental.pallas import tpu as pltpu

F32 = jnp.float32
BF16 = jnp.bfloat16

D_MODEL = 2048
DEPTH = 4
GRID_W = 64
EPS = 1e-6
RW_HEAD = 64
RW_WIDTH = 1024
W_LORA = 64
A_LORA = 64
G_LORA = 160
RW_GN_EPS = 64e-5
AT_HEADS = 16
AT_KV_HEADS = 4
AT_HEAD = 64
AT_Q = AT_HEADS * AT_HEAD
AT_KV = AT_KV_HEADS * AT_HEAD
ROPE_THETA = 10000.0
X_HEADS = 4
X_HEAD = 128
X_WIDTH = X_HEADS * X_HEAD
D_FF = 5632

LANES = 128
SUBLANES = 8
CHUNK = 64
SUB = 16
LOG2E = 1.4426950408889634
V_AUG = AT_HEAD + 16
HALO = 16
G_PAD = 256

COL_G1 = 0
COL_G2 = 2048
COL_R = 4096
COL_K = 5120
COL_V = 6144
COL_Q = 7168
COL_LORA = 8192
COL_ATKV = 8704
IN_PACKED = 9216

VMEM_LIMIT = 56 * 1024 * 1024


def _cparams(sem):
    return pltpu.CompilerParams(dimension_semantics=sem, vmem_limit_bytes=VMEM_LIMIT)


def _dot(a, b):
    return jnp.dot(a.astype(BF16), b.astype(BF16), preferred_element_type=F32)


def _dot_nt(a, b):
    return lax.dot_general(a.astype(BF16), b.astype(BF16), (((1,), (1,)), ((), ())),
                           preferred_element_type=F32)


def _split2(x):
    hi = x.astype(BF16)
    lo = (x - hi.astype(F32)).astype(BF16)
    return hi, lo


def _split3(x):
    hi = x.astype(BF16)
    r1 = x - hi.astype(F32)
    mid = r1.astype(BF16)
    lo = (r1 - mid.astype(F32)).astype(BF16)
    return hi, mid, lo


def _sigmoid(x):
    return 1.0 / (1.0 + jnp.exp(-x))


def _softplus(x):
    return jnp.maximum(x, 0.0) + jnp.log(1.0 + jnp.exp(-jnp.abs(x)))


def _head_ones():
    r = lax.broadcasted_iota(jnp.int32, (LANES, LANES), 0) // RW_HEAD
    c = lax.broadcasted_iota(jnp.int32, (LANES, LANES), 1) // RW_HEAD
    return jnp.where(r == c, 1.0, 0.0).astype(BF16)


def _head_sum(x, ones):
    outs = []
    for s in range(x.shape[1] // LANES):
        hi, lo = _split2(x[:, s * LANES:(s + 1) * LANES])
        outs.append(jnp.dot(hi, ones, preferred_element_type=F32)
                    + jnp.dot(lo, ones, preferred_element_type=F32))
    return outs[0] if len(outs) == 1 else jnp.concatenate(outs, axis=1)


def _rms(x, gain):
    ms = jnp.mean(x * x, axis=-1, keepdims=True)
    return x * lax.rsqrt(ms + EPS) * gain


def _norm_proj_kernel(x_ref, g_ref, w_ref, o_ref, h_ref):
    @pl.when(pl.program_id(1) == 0)
    def _():
        h_ref[...] = _rms(x_ref[...], g_ref[...]).astype(BF16)

    o_ref[...] = jnp.dot(h_ref[...], w_ref[...], preferred_element_type=F32).astype(o_ref.dtype)


def _norm_proj(x, gain, w, layer, *, tm, tn, out_dtype):
    m, d = x.shape
    n = w.shape[2]
    return pl.pallas_call(
        _norm_proj_kernel,
        out_shape=jax.ShapeDtypeStruct((m, n), out_dtype),
        grid=(m // tm, n // tn),
        in_specs=[pl.BlockSpec((tm, d), lambda i, j: (i, 0)),
                  pl.BlockSpec((1, d), lambda i, j: (0, 0)),
                  pl.BlockSpec((None, d, tn), lambda i, j: (layer, 0, j))],
        out_specs=pl.BlockSpec((tm, tn), lambda i, j: (i, j)),
        scratch_shapes=[pltpu.VMEM((tm, d), BF16)],
        compiler_params=_cparams(("parallel", "arbitrary")),
        name="norm_proj",
    )(x, gain, w)


def _shift(z, prev_row, next_row, mu_p, mu_n):
    rows = z.shape[0]
    ridx = lax.broadcasted_iota(jnp.int32, z.shape, 0)
    zp = jnp.where(ridx == 0, prev_row, pltpu.roll(z, 1, 0))
    zn = jnp.where(ridx == rows - 1, next_row, pltpu.roll(z, rows - 1, 0))
    return z + mu_p * (zp - z) + mu_n * (zn - z)


def _rwprep_kernel(has_vmix, *refs):
    (zr, zk, zv, zl, pr, pk, pv, pL, nr, nk, nv, nL, mup, mun, w0, wup, a0, aup, gup,
     kk_w, ka_w, rk_w) = refs[:22]
    pos = 22
    if has_vmix:
        vfirst, v0, vdown, vup = refs[pos:pos + 4]
        pos += 4
    r_o, v_o, kk_o, kd_o, lw_o, ba_o, g_o, bon_o = refs[pos:pos + 8]

    i = pl.program_id(0)
    last = pl.num_programs(0) - 1
    keep_p = jnp.where(i == 0, 0.0, 1.0)
    keep_n = jnp.where(i == last, 0.0, 1.0)

    def shifted(z_ref, p_ref, n_ref, lo, hi):
        p_row = p_ref[...].astype(F32)[HALO - 1:HALO, :] * keep_p
        n_row = n_ref[...].astype(F32)[0:1, :] * keep_n
        return _shift(z_ref[...].astype(F32), p_row, n_row, mup[:, lo:hi], mun[:, lo:hi])

    r = shifted(zr, pr, nr, 0, 1024)
    k = shifted(zk, pk, nk, 1024, 2048)
    v = shifted(zv, pv, nv, 2048, 3072)
    lora = shifted(zl, pL, nL, 3072, 3584)

    if has_vmix:
        mix = _sigmoid(v0[...] + _dot(_dot(v, vdown[...]), vup[...]))
        v = v + (vfirst[...].astype(F32) - v) * mix

    u = w0[...] + _dot(jnp.tanh(lora[:, 0:128]), wup[...])
    w_log = -_softplus(-u) - 0.5
    lw = -jnp.exp(w_log)
    a = _sigmoid(a0[...] + _dot(lora[:, 128:256], aup[...]))
    g = _dot(_sigmoid(lora[:, 256:512]), gup[...])

    ones = _head_ones()
    kk = k * kk_w[...]
    norm = jnp.sqrt(_head_sum(kk * kk, ones))
    kk = kk / jnp.maximum(norm, 1e-12)
    ka = ka_w[...]
    a_f = a[:, :RW_WIDTH]
    a_b = a[:, RW_WIDTH:]
    kd_f = k * (1.0 + (a_f - 1.0) * ka)
    kd_b = k * (1.0 + (a_b - 1.0) * ka)
    bonus = _head_sum(r * rk_w[...] * (kd_f + kd_b), ones) * v

    r_o[...] = r.astype(r_o.dtype)
    v_o[...] = v.astype(v_o.dtype)
    kk_o[...] = kk.astype(kk_o.dtype)
    kd_o[0] = kd_f.astype(kd_o.dtype)
    kd_o[1] = kd_b.astype(kd_o.dtype)
    lw_o[0] = lw[:, :RW_WIDTH]
    lw_o[1] = lw[:, RW_WIDTH:]
    ba_o[0] = (kk * a_f).astype(ba_o.dtype)
    ba_o[1] = (kk * a_b).astype(ba_o.dtype)
    g_o[...] = g.astype(g_o.dtype)
    bon_o[...] = bonus.astype(bon_o.dtype)


def _rwprep(z, mup, mun, w0, wup, a0, aup, gup, kk_w, ka_w, rk_w, vmix, *, tm):
    s = z.shape[0]
    nbh = s // HALO
    tb = tm // HALO
    has_vmix = vmix is not None

    def main(width, cb):
        return pl.BlockSpec((tm, width), lambda i: (i, cb))

    def prev(width, cb):
        return pl.BlockSpec((HALO, width), lambda i: (jnp.maximum(i * tb - 1, 0), cb))

    def nxt(width, cb):
        return pl.BlockSpec((HALO, width), lambda i: (jnp.minimum((i + 1) * tb, nbh - 1), cb))

    def full(arr):
        nd = arr.ndim
        return pl.BlockSpec(arr.shape, lambda i: (0,) * nd)

    cols = [(1024, COL_R // 1024), (1024, COL_K // 1024), (1024, COL_V // 1024), (512, COL_LORA // 512)]
    in_specs = ([main(w, c) for w, c in cols] + [prev(w, c) for w, c in cols]
                + [nxt(w, c) for w, c in cols])
    args = [z] * 12
    consts = [mup, mun, w0, wup, a0, aup, gup, kk_w, ka_w, rk_w]
    in_specs += [full(c) for c in consts]
    args += consts
    if has_vmix:
        vfirst, v0, vdown, vup = vmix
        in_specs += [pl.BlockSpec((tm, RW_WIDTH), lambda i: (i, 0)), full(v0), full(vdown), full(vup)]
        args += [vfirst, v0, vdown, vup]

    one = jax.ShapeDtypeStruct((s, RW_WIDTH), F32)
    two = jax.ShapeDtypeStruct((2, s, RW_WIDTH), F32)
    spec1 = pl.BlockSpec((tm, RW_WIDTH), lambda i: (i, 0))
    spec2 = pl.BlockSpec((2, tm, RW_WIDTH), lambda i: (0, i, 0))
    return pl.pallas_call(
        functools.partial(_rwprep_kernel, has_vmix),
        out_shape=[one, one, one, two, two, two, one, one],
        grid=(s // tm,),
        in_specs=in_specs,
        out_specs=[spec1, spec1, spec1, spec2, spec2, spec2, spec1, spec1],
        compiler_params=_cparams(("parallel",)),
        name="rwkv_prep",
    )(*args)


def _rwrec_kernel(rf_ref, vf_ref, kkf_ref, kdf_ref, lwf_ref, baf_ref,
                  rb_ref, vb_ref, kkb_ref, kdb_ref, lwb_ref, bab_ref, yf_ref, yb_ref, s_ref):
    T = CHUNK
    T2 = 2 * T
    n_slab = RW_WIDTH // LANES

    @pl.when(pl.program_id(0) == 0)
    def _():
        s_ref[...] = jnp.zeros_like(s_ref)

    row = lax.broadcasted_iota(jnp.int32, (T, T), 0)
    col = lax.broadcasted_iota(jnp.int32, (T, T), 1)
    R2 = lax.broadcasted_iota(jnp.int32, (T2, T2), 0)
    C2 = lax.broadcasted_iota(jnp.int32, (T2, T2), 1)
    same = (R2 // T) == (C2 // T)
    tdiff = (R2 % T) - (C2 % T)
    mask_d = same & (((R2 % T) // SUB) == ((C2 % T) // SUB))
    eye = R2 == C2
    head0 = lax.broadcasted_iota(jnp.int32, (T, LANES), 1) < RW_HEAD

    def stack_heads(x):
        return jnp.concatenate([jnp.where(head0, x, 0.0), jnp.where(head0, 0.0, x)], axis=0)

    def stack_dup(x):
        return jnp.concatenate([x, x], axis=0)

    cols = [slice(sl * LANES, (sl + 1) * LANES) for sl in range(n_slab)]
    ar2, bk2, v2, bke2, mask_a, mask_y, e_tot = [], [], [], [], [], [], []
    for sgn, (r_ref, v_ref, kk_ref, kd_ref, lw_ref, ba_ref) in (
            (1, (rf_ref, vf_ref, kkf_ref, kdf_ref, lwf_ref, baf_ref)),
            (-1, (rb_ref, vb_ref, kkb_ref, kdb_ref, lwb_ref, bab_ref))):
        tri = jnp.where((row - col) * sgn >= 0, 1.0, 0.0).astype(BF16)
        lw = lw_ref[...]
        hi, mid, lo = _split3(lw)
        cin = (jnp.dot(tri, hi, preferred_element_type=F32)
               + jnp.dot(tri, mid, preferred_element_type=F32)
               + jnp.dot(tri, lo, preferred_element_type=F32))
        ctot = jnp.sum(lw, axis=0, keepdims=True)
        e_in = jnp.exp(cin)
        e_ex = jnp.exp(cin - lw)
        e_neg = jnp.exp(-cin)
        e_end = jnp.exp(ctot - cin)
        etot = jnp.exp(ctot)
        kd = kd_ref[...].astype(F32)
        ba = ba_ref[...].astype(F32)
        r_t = r_ref[...].astype(F32) * e_in
        a_t = -kk_ref[...].astype(F32) * e_ex
        b_t = ba * e_neg
        k_t = kd * e_neg
        b_e = ba * e_end
        k_e = kd * e_end
        v = v_ref[...].astype(F32)
        m_a = same & (tdiff * sgn > 0)
        m_y = same & (tdiff * sgn >= 0)
        for cs in cols:
            ar2.append(jnp.concatenate([stack_heads(a_t[:, cs]), stack_heads(r_t[:, cs])],
                                       axis=0).astype(BF16))
            bk2.append(jnp.concatenate([stack_dup(b_t[:, cs]), stack_dup(k_t[:, cs])],
                                       axis=0).astype(BF16))
            v2.append(stack_heads(v[:, cs]).astype(BF16))
            bke2.append(jnp.concatenate([stack_heads(b_e[:, cs]), stack_heads(k_e[:, cs])],
                                        axis=0).astype(BF16))
            mask_a.append(m_a)
            mask_y.append(m_y)
            e_tot.append(etot[:, cs])

    slabs = range(2 * n_slab)
    sc = [_dot_nt(ar2[s], bk2[s]) for s in slabs]
    n_ab = [jnp.where(mask_a[s], sc[s][:T2, :T2], 0.0) for s in slabs]
    n_ak = [jnp.where(mask_a[s], sc[s][:T2, T2:], 0.0) for s in slabs]
    m_rbk = [jnp.concatenate([jnp.where(mask_y[s], sc[s][T2:, :T2], 0.0),
                              jnp.where(mask_y[s], sc[s][T2:, T2:], 0.0)], axis=1).astype(BF16)
             for s in slabs]

    def dot2(lhs, a, b):
        t = _dot(lhs, jnp.concatenate([a, b], axis=1))
        return t[:, :T2], t[:, T2:]

    p = [jnp.where(mask_d, n_ab[s], 0.0) for s in slabs]
    q = [jnp.where(eye, 1.0, p[s]) for s in slabs]
    p = [_dot(p[s], p[s]) for s in slabs]
    for _ in range(2):
        pq = [dot2(p[s], p[s], q[s]) for s in slabs]
        p = [pq[s][0] for s in slabs]
        q = [q[s] + pq[s][1] for s in slabs]
    q = [q[s] + _dot(p[s], q[s]) for s in slabs]
    akv = [_dot(n_ak[s], v2[s]) for s in slabs]

    st = [s_ref[s // n_slab, s % n_slab] for s in slabs]
    ars = [_dot_nt(ar2[s], st[s]) for s in slabs]
    mx = [dot2(q[s], jnp.where(mask_d, 0.0, n_ab[s]), ars[s][:T2] + akv[s]) for s in slabs]
    m = [mx[s][0] for s in slabs]
    x = [mx[s][1] for s in slabs]
    mx = [dot2(m[s], m[s], x[s]) for s in slabs]
    x = [x[s] + mx[s][1] for s in slabs]
    x = [x[s] + _dot(mx[s][0], x[s]) for s in slabs]
    for s in slabs:
        y2 = ars[s][T2:] + _dot(m_rbk[s], jnp.concatenate([x[s].astype(BF16), v2[s]], axis=0))
        y_ref = yf_ref if s < n_slab else yb_ref
        y_ref[:, cols[s % n_slab]] = y2[:T] + y2[T:]
    for s in slabs:
        upd = _dot(jnp.concatenate([x[s].T, v2[s].astype(F32).T], axis=1), bke2[s])
        s_ref[s // n_slab, s % n_slab] = st[s] * e_tot[s] + upd


def _rwrec(r, v, kk, kd, lw, ba):
    s = r.shape[0]
    nc = s // CHUNK
    fwd1 = pl.BlockSpec((CHUNK, RW_WIDTH), lambda c: (c, 0))
    bwd1 = pl.BlockSpec((CHUNK, RW_WIDTH), lambda c: (nc - 1 - c, 0))
    fwd2 = pl.BlockSpec((None, CHUNK, RW_WIDTH), lambda c: (0, c, 0))
    bwd2 = pl.BlockSpec((None, CHUNK, RW_WIDTH), lambda c: (1, nc - 1 - c, 0))
    out = jax.ShapeDtypeStruct((s, RW_WIDTH), F32)
    return pl.pallas_call(
        _rwrec_kernel,
        out_shape=[out, out],
        grid=(nc,),
        in_specs=[fwd1, fwd1, fwd1, fwd2, fwd2, fwd2, bwd1, bwd1, bwd1, bwd2, bwd2, bwd2],
        out_specs=[fwd1, bwd1],
        scratch_shapes=[pltpu.VMEM((2, RW_WIDTH // LANES, LANES, LANES), F32)],
        compiler_params=_cparams(("arbitrary",)),
        name="rwkv_recurrence",
    )(r, v, kk, kd, lw, ba, r, v, kk, kd, lw, ba)


def _atprep_kernel(zq, zkv, cos_ref, sin_ref, qn, kn, qt_o, k_o, vt_o):
    ones = _head_ones()
    cos = cos_ref[...]
    sin = sin_ref[...]
    lane = lax.broadcasted_iota(jnp.int32, cos.shape, 1)
    even = ((lane // 16) % 2) == 0

    def norm_rope(t, gain):
        ms = _head_sum(t * t, ones) * (1.0 / AT_HEAD)
        tn = t * lax.rsqrt(ms + EPS) * gain
        swapped = jnp.where(even, pltpu.roll(tn, LANES - 16, 1), pltpu.roll(tn, 16, 1))
        return tn * cos + swapped * sin

    q_scale = (AT_HEAD ** -0.5) * LOG2E
    tm = zq.shape[0]
    ones_rows = jnp.where(lax.broadcasted_iota(jnp.int32, (V_AUG - AT_HEAD, tm), 0) == 0, 1.0, 0.0).astype(BF16)
    for sl in range(AT_Q // LANES):
        out = norm_rope(zq[:, sl * LANES:(sl + 1) * LANES].astype(F32), qn[...]) * q_scale
        out_t = out.T.astype(BF16)
        qt_o[2 * sl] = out_t[:AT_HEAD]
        qt_o[2 * sl + 1] = out_t[AT_HEAD:]
    for sl in range(AT_KV // LANES):
        out = norm_rope(zkv[:, sl * LANES:(sl + 1) * LANES].astype(F32), kn[...])
        k_o[2 * sl] = out[:, :AT_HEAD].astype(BF16)
        k_o[2 * sl + 1] = out[:, AT_HEAD:].astype(BF16)
        vt = zkv[:, AT_KV + sl * LANES:AT_KV + (sl + 1) * LANES].astype(F32).T.astype(BF16)
        vt_o[2 * sl] = jnp.concatenate([vt[:AT_HEAD], ones_rows], axis=0)
        vt_o[2 * sl + 1] = jnp.concatenate([vt[AT_HEAD:], ones_rows], axis=0)


def _atprep(z, cos, sin, qn, kn, *, tm):
    s = z.shape[0]
    nb = s // tm
    return pl.pallas_call(
        _atprep_kernel,
        out_shape=[jax.ShapeDtypeStruct((AT_HEADS, AT_HEAD, s), BF16),
                   jax.ShapeDtypeStruct((AT_KV_HEADS, nb, tm, AT_HEAD), BF16),
                   jax.ShapeDtypeStruct((AT_KV_HEADS, nb, V_AUG, tm), BF16)],
        grid=(nb,),
        in_specs=[pl.BlockSpec((tm, AT_Q), lambda i: (i, COL_Q // AT_Q)),
                  pl.BlockSpec((tm, 2 * AT_KV), lambda i: (i, COL_ATKV // (2 * AT_KV))),
                  pl.BlockSpec((tm, LANES), lambda i: (i, 0)),
                  pl.BlockSpec((tm, LANES), lambda i: (i, 0)),
                  pl.BlockSpec((1, LANES), lambda i: (0, 0)),
                  pl.BlockSpec((1, LANES), lambda i: (0, 0))],
        out_specs=[pl.BlockSpec((AT_HEADS, AT_HEAD, tm), lambda i: (0, 0, i)),
                   pl.BlockSpec((AT_KV_HEADS, None, tm, AT_HEAD), lambda i: (0, i, 0, 0)),
                   pl.BlockSpec((AT_KV_HEADS, None, V_AUG, tm), lambda i: (0, i, 0, 0))],
        compiler_params=_cparams(("parallel",)),
        name="attn_prep",
    )(z, z, cos, sin, qn, kn)


def _flash_kernel(qt_ref, k_ref, vt_ref, o_ref, m_ref, acc_ref, sa_ref, sb_ref):
    grp = qt_ref.shape[0]
    nkb = k_ref.shape[0]
    m_ref[...] = jnp.full_like(m_ref, -jnp.inf)
    acc_ref[...] = jnp.zeros_like(acc_ref)

    def step(kb_cur, cur_ref, kb_next, next_ref):
        k_next = k_ref[kb_next]
        vt = vt_ref[kb_cur]
        for g in range(grp):
            next_ref[g] = jnp.dot(k_next, qt_ref[g], preferred_element_type=F32)
            st = cur_ref[g]
            m_prev = m_ref[g]
            m_new = jnp.maximum(m_prev, jnp.max(st, axis=0, keepdims=True))
            alpha = jnp.exp2(m_prev - m_new)
            pt = jnp.exp2(st - m_new).astype(BF16)
            m_ref[g] = m_new
            acc_ref[g] = alpha * acc_ref[g] + jnp.dot(vt, pt, preferred_element_type=F32)

    k0 = k_ref[0]
    for g in range(grp):
        sa_ref[g] = jnp.dot(k0, qt_ref[g], preferred_element_type=F32)

    unroll = max(u for u in (8, 4, 2) if nkb % u == 0)

    def body(j, carry):
        kb = unroll * j
        for u in range(0, unroll, 2):
            step(kb + u, sa_ref, kb + u + 1, sb_ref)
            step(kb + u + 1, sb_ref, jnp.minimum(kb + u + 2, nkb - 1), sa_ref)
        return carry

    lax.fori_loop(0, nkb // unroll, body, 0)
    for pair in range(grp // 2):
        a0 = acc_ref[2 * pair]
        a1 = acc_ref[2 * pair + 1]
        o2 = jnp.concatenate([a0[:AT_HEAD] / a0[AT_HEAD:AT_HEAD + 1],
                              a1[:AT_HEAD] / a1[AT_HEAD:AT_HEAD + 1]], axis=0)
        o_ref[:, pair * LANES:(pair + 1) * LANES] = o2.T.astype(o_ref.dtype)


def _flash(qt, k, vt, *, tq):
    s = qt.shape[2]
    _, nkb, tk, _ = k.shape
    grp = AT_HEADS // AT_KV_HEADS
    return pl.pallas_call(
        _flash_kernel,
        out_shape=jax.ShapeDtypeStruct((s, AT_Q), BF16),
        grid=(AT_KV_HEADS, s // tq),
        in_specs=[pl.BlockSpec((grp, AT_HEAD, tq), lambda h, i: (h, 0, i)),
                  pl.BlockSpec((None, nkb, tk, AT_HEAD), lambda h, i: (h, 0, 0, 0)),
                  pl.BlockSpec((None, nkb, V_AUG, tk), lambda h, i: (h, 0, 0, 0))],
        out_specs=pl.BlockSpec((tq, grp * AT_HEAD), lambda h, i: (i, h)),
        scratch_shapes=[pltpu.VMEM((grp, 1, tq), F32), pltpu.VMEM((grp, V_AUG, tq), F32),
                        pltpu.VMEM((grp, tk, tq), F32), pltpu.VMEM((grp, tk, tq), F32)],
        compiler_params=_cparams(("parallel", "parallel")),
        name="flash_attention",
    )(qt, k, vt)


def _mix_out_kernel(yf_ref, yb_ref, g_ref, bon_ref, yat_ref, zg1_ref, zg2_ref, x_ref, lnw, lnb,
                    wrw, wat, wo, npost, o_ref):
    ones = _head_ones()
    y = yf_ref[...] + yb_ref[...]
    inv = 1.0 / RW_HEAD
    mu = _head_sum(y, ones) * inv
    dlt = y - mu
    var = _head_sum(dlt * dlt, ones) * inv
    yn = dlt * lax.rsqrt(var + RW_GN_EPS) * lnw[...] + lnb[...]
    y_rw = (yn + bon_ref[...].astype(F32)) * g_ref[...].astype(F32)
    pa = _dot(y_rw, wrw[...])
    pb = jnp.dot(yat_ref[...], wat[...], preferred_element_type=F32)
    merged = _sigmoid(zg1_ref[...].astype(F32)) * pa + _sigmoid(zg2_ref[...].astype(F32)) * pb
    out = _dot(merged, wo[...])
    o_ref[...] = x_ref[...] + _rms(out, npost[...])


def _layer_block(arr, layer, **kw):
    return pl.BlockSpec((None,) + arr.shape[1:], lambda *_: (layer, 0, 0), **kw)


def _mix_out(yf, yb, g, bon, yat, z, x, lnw, lnb, wrw, wat, wo, npost, layer, *, tm):
    s = x.shape[0]

    def const(arr):
        if arr.ndim == 3:
            return _layer_block(arr, layer, pipeline_mode=pl.Buffered(1))
        return pl.BlockSpec(arr.shape, lambda i: (0, 0), pipeline_mode=pl.Buffered(1))

    row1k = pl.BlockSpec((tm, RW_WIDTH), lambda i: (i, 0))
    return pl.pallas_call(
        _mix_out_kernel,
        out_shape=jax.ShapeDtypeStruct((s, D_MODEL), F32),
        grid=(s // tm,),
        in_specs=[row1k, row1k, row1k, row1k, row1k,
                  pl.BlockSpec((tm, D_MODEL), lambda i: (i, COL_G1 // D_MODEL)),
                  pl.BlockSpec((tm, D_MODEL), lambda i: (i, COL_G2 // D_MODEL)),
                  pl.BlockSpec((tm, D_MODEL), lambda i: (i, 0)),
                  const(lnw), const(lnb), const(wrw), const(wat), const(wo), const(npost)],
        out_specs=pl.BlockSpec((tm, D_MODEL), lambda i: (i, 0)),
        compiler_params=_cparams(("parallel",)),
        name="mix_out",
    )(yf, yb, g, bon, yat, z, z, x, lnw, lnb, wrw, wat, wo, npost)


def _xattn_kernel(x_ref, kv_ref, npre, wq, wo, npost, o_ref):
    x = x_ref[...]
    h = _rms(x, npre[...])
    q = _dot(h, wq[...]) * (X_HEAD ** -0.5)
    outs = []
    for hd in range(X_HEADS):
        kh = kv_ref[:, hd * X_HEAD:(hd + 1) * X_HEAD]
        vh = kv_ref[:, X_WIDTH + hd * X_HEAD:X_WIDTH + (hd + 1) * X_HEAD]
        sc = _dot_nt(q[:, hd * X_HEAD:(hd + 1) * X_HEAD], kh)
        sc = sc - jnp.max(sc, axis=-1, keepdims=True)
        p = jnp.exp(sc)
        p = p / jnp.sum(p, axis=-1, keepdims=True)
        outs.append(_dot(p, vh))
    o = jnp.concatenate(outs, axis=1)
    c = _dot(o, wo[...])
    o_ref[...] = x + _rms(c, npost[...])


def _xattn(x, kv, npre, wq, wo, npost, layer, *, tm):
    s = x.shape[0]

    def const(arr):
        if arr.ndim == 3:
            return _layer_block(arr, layer)
        return pl.BlockSpec(arr.shape, lambda i: (0, 0))

    return pl.pallas_call(
        _xattn_kernel,
        out_shape=jax.ShapeDtypeStruct((s, D_MODEL), F32),
        grid=(s // tm,),
        in_specs=[pl.BlockSpec((tm, D_MODEL), lambda i: (i, 0)), const(kv), const(npre),
                  const(wq), const(wo), const(npost)],
        out_specs=pl.BlockSpec((tm, D_MODEL), lambda i: (i, 0)),
        compiler_params=_cparams(("parallel",)),
        name="cross_attention",
    )(x, kv, npre, wq, wo, npost)


def _ffn_kernel(x_ref, npre, wg, wu, wd, npost, o_ref, h_ref):
    j = pl.program_id(1)

    @pl.when(j == 0)
    def _():
        h_ref[...] = _rms(x_ref[...], npre[...]).astype(BF16)
        o_ref[...] = jnp.zeros_like(o_ref)

    h = h_ref[...]
    gt = jnp.dot(h, wg[...].astype(BF16), preferred_element_type=F32)
    ut = jnp.dot(h, wu[...].astype(BF16), preferred_element_type=F32)
    act = gt * _sigmoid(gt) * ut
    o_ref[...] += _dot(act, wd[...])

    @pl.when(j == pl.num_programs(1) - 1)
    def _():
        o_ref[...] = x_ref[...] + _rms(o_ref[...], npost[...])


def _ffn(x, npre, wg, wu, wd, npost, layer, *, tm, tf):
    s = x.shape[0]
    f = wg.shape[2]
    return pl.pallas_call(
        _ffn_kernel,
        out_shape=jax.ShapeDtypeStruct((s, D_MODEL), F32),
        grid=(s // tm, f // tf),
        in_specs=[pl.BlockSpec((tm, D_MODEL), lambda i, j: (i, 0), pipeline_mode=pl.Buffered(1)),
                  pl.BlockSpec((1, D_MODEL), lambda i, j: (0, 0)),
                  pl.BlockSpec((None, D_MODEL, tf), lambda i, j: (layer, 0, j)),
                  pl.BlockSpec((None, D_MODEL, tf), lambda i, j: (layer, 0, j)),
                  pl.BlockSpec((None, tf, D_MODEL), lambda i, j: (layer, j, 0)),
                  pl.BlockSpec((1, D_MODEL), lambda i, j: (0, 0))],
        out_specs=pl.BlockSpec((tm, D_MODEL), lambda i, j: (i, 0)),
        scratch_shapes=[pltpu.VMEM((tm, D_MODEL), BF16)],
        compiler_params=_cparams(("parallel", "arbitrary")),
        name="swiglu",
    )(x, npre, wg, wu, wd, npost)


def _pack_w_in_kernel(w_ref, o_ref):
    o_lora = 3 * RW_WIDTH
    o_at = o_lora + 2 * W_LORA + 2 * A_LORA + G_LORA
    o_gate = o_at + AT_Q + 2 * AT_KV
    n_lora = o_at - o_lora
    rows = o_ref.shape[0]
    o_ref[:, COL_G1:COL_G1 + 2 * D_MODEL] = w_ref[:, o_gate:o_gate + 2 * D_MODEL].astype(BF16)
    o_ref[:, COL_R:COL_R + 3 * RW_WIDTH] = w_ref[:, 0:3 * RW_WIDTH].astype(BF16)
    o_ref[:, COL_Q:COL_Q + AT_Q] = w_ref[:, o_at:o_at + AT_Q].astype(BF16)
    o_ref[:, COL_LORA:COL_LORA + n_lora] = w_ref[:, o_lora:o_at].astype(BF16)
    o_ref[:, COL_LORA + n_lora:COL_ATKV] = jnp.zeros((rows, COL_ATKV - COL_LORA - n_lora), BF16)
    o_ref[:, COL_ATKV:IN_PACKED] = w_ref[:, o_at + AT_Q:o_gate].astype(BF16)


def _pack_w_in(w, *, tr):
    nl, d, n_in = w.shape
    return pl.pallas_call(
        _pack_w_in_kernel,
        out_shape=jax.ShapeDtypeStruct((nl, d, IN_PACKED), BF16),
        grid=(nl, d // tr),
        in_specs=[pl.BlockSpec((None, tr, n_in), lambda l, i: (l, i, 0))],
        out_specs=pl.BlockSpec((None, tr, IN_PACKED), lambda l, i: (l, i, 0)),
        compiler_params=_cparams(("parallel", "parallel")),
        name="pack_w_in",
    )(w)


def _pack_mu(mu):
    return jnp.concatenate([mu, jnp.zeros((G_PAD - G_LORA,), mu.dtype)])[None, :]


def _block_diag_up(up):
    z = jnp.zeros_like(up[0])
    return jnp.concatenate([jnp.concatenate([up[0], z], axis=1),
                            jnp.concatenate([z, up[1]], axis=1)], axis=0).astype(BF16)


def _rope_tables(s):
    rows = s // GRID_W
    row = jnp.repeat(jnp.arange(rows), GRID_W).astype(F32)
    col = jnp.tile(jnp.arange(GRID_W), rows).astype(F32)
    n_freq = AT_HEAD // 4
    inv = ROPE_THETA ** (-jnp.arange(n_freq, dtype=F32) / n_freq)
    ar = row[:, None] * inv
    ac = col[:, None] * inv
    cos = jnp.concatenate([jnp.cos(ar), jnp.cos(ar), jnp.cos(ac), jnp.cos(ac)], axis=1)
    sin = jnp.concatenate([-jnp.sin(ar), jnp.sin(ar), -jnp.sin(ac), jnp.sin(ac)], axis=1)
    return jnp.tile(cos, (1, 2)), jnp.tile(sin, (1, 2))


def kernel(x, mem, n_mix_pre, n_mix_post, n_x_pre, n_x_post, n_ffn_pre, n_ffn_post, n_mem, w_in, rw_mu_prev, rw_mu_next, rw_w0, rw_w_up, rw_a0, rw_a_up, rw_g_up, rw_v0, rw_v_down, rw_v_up, rw_k_k, rw_k_a, rw_r_k, rw_ln_w, rw_ln_b, w_rw_out, at_q_norm, at_k_norm, w_at_out, w_o, x_wq, x_wkv, x_wo, ffn_wg, ffn_wu, ffn_wd):
    b, s, d = x.shape
    assert b == 1 and d == D_MODEL and s % 1024 == 0
    xs = x[0]
    mems = mem[0]
    cos, sin = _rope_tables(s)
    row = lambda t: t[None, :]
    w_in_p = _pack_w_in(w_in, tr=256)
    wrw_b, wat_b, wo_b = w_rw_out.astype(BF16), w_at_out.astype(BF16), w_o.astype(BF16)
    xwq_b, xwkv_b, xwo_b = x_wq.astype(BF16), x_wkv.astype(BF16), x_wo.astype(BF16)
    v_first = None
    for l in range(DEPTH):
        z = _norm_proj(xs, row(n_mix_pre[l]), w_in_p, l, tm=1024, tn=1536, out_dtype=F32)

        vmix = None
        if l > 0:
            vmix = (v_first, row(rw_v0[l - 1]), rw_v_down[l - 1].astype(BF16), rw_v_up[l - 1].astype(BF16))
        gup = jnp.concatenate([rw_g_up[l], jnp.zeros((G_PAD - G_LORA, RW_WIDTH), F32)], axis=0).astype(BF16)
        r, v, kk, kd, lw, ba, g, bon = _rwprep(
            z, _pack_mu(rw_mu_prev[l]), _pack_mu(rw_mu_next[l]),
            rw_w0[l].reshape(1, 2 * RW_WIDTH), _block_diag_up(rw_w_up[l]),
            rw_a0[l].reshape(1, 2 * RW_WIDTH), _block_diag_up(rw_a_up[l]), gup,
            row(rw_k_k[l]), row(rw_k_a[l]), rw_r_k[l].reshape(1, RW_WIDTH), vmix, tm=256)
        if l == 0:
            v_first = v
        y_f, y_b = _rwrec(r, v, kk, kd, lw, ba)

        qn = jnp.tile(at_q_norm[l], 2)[None, :]
        kn = jnp.tile(at_k_norm[l], 2)[None, :]
        q_t, k_blk, v_t = _atprep(z, cos, sin, qn, kn, tm=512)
        y_at = _flash(q_t, k_blk, v_t, tq=256)

        xs = _mix_out(y_f, y_b, g, bon, y_at, z, xs, row(rw_ln_w[l]), row(rw_ln_b[l]),
                      wrw_b, wat_b, wo_b, row(n_mix_post[l]), l, tm=256)

        kv = _norm_proj(mems, row(n_mem[l]), xwkv_b, l, tm=mems.shape[0], tn=512, out_dtype=BF16)
        xs = _xattn(xs, kv, row(n_x_pre[l]), xwq_b, xwo_b, row(n_x_post[l]), l, tm=512)

        xs = _ffn(xs, row(n_ffn_pre[l]), ffn_wg, ffn_wu, ffn_wd, row(n_ffn_post[l]), l, tm=1024, tf=256)
    return xs[None]
```

```python
import functools

import jax
import jax.numpy as jnp
from jax import lax
from jax.experimental import pallas as pl
from jax.experimental.pallas import tpu as pltpu

F32 = jnp.float32
BF16 = jnp.bfloat16

D_MODEL = 2048
DEPTH = 4
GRID_W = 64
EPS = 1e-6
RW_HEAD = 64
RW_WIDTH = 1024
W_LORA = 64
A_LORA = 64
G_LORA = 160
RW_GN_EPS = 64e-5
AT_HEADS = 16
AT_KV_HEADS = 4
AT_HEAD = 64
AT_Q = AT_HEADS * AT_HEAD
AT_KV = AT_KV_HEADS * AT_HEAD
ROPE_THETA = 10000.0
X_HEADS = 4
X_HEAD = 128
X_WIDTH = X_HEADS * X_HEAD
D_FF = 5632

LANES = 128
SUBLANES = 8
CHUNK = 64
SUB = 16
LOG2E = 1.4426950408889634
V_AUG = AT_HEAD + 16
HALO = 16
G_PAD = 256

COL_G1 = 0
COL_G2 = 2048
COL_R = 4096
COL_K = 5120
COL_V = 6144
COL_Q = 7168
COL_LORA = 8192
COL_ATKV = 8704
IN_PACKED = 9216

VMEM_LIMIT = 56 * 1024 * 1024


def _cparams(sem):
    return pltpu.CompilerParams(dimension_semantics=sem, vmem_limit_bytes=VMEM_LIMIT)


def _dot(a, b):
    return jnp.dot(a.astype(BF16), b.astype(BF16), preferred_element_type=F32)


def _dot_nt(a, b):
    return lax.dot_general(a.astype(BF16), b.astype(BF16), (((1,), (1,)), ((), ())),
                           preferred_element_type=F32)


def _split2(x):
    hi = x.astype(BF16)
    lo = (x - hi.astype(F32)).astype(BF16)
    return hi, lo


def _split3(x):
    hi = x.astype(BF16)
    r1 = x - hi.astype(F32)
    mid = r1.astype(BF16)
    lo = (r1 - mid.astype(F32)).astype(BF16)
    return hi, mid, lo


def _sigmoid(x):
    return 1.0 / (1.0 + jnp.exp(-x))


def _softplus(x):
    return jnp.maximum(x, 0.0) + jnp.log(1.0 + jnp.exp(-jnp.abs(x)))


def _head_ones():
    r = lax.broadcasted_iota(jnp.int32, (LANES, LANES), 0) // RW_HEAD
    c = lax.broadcasted_iota(jnp.int32, (LANES, LANES), 1) // RW_HEAD
    return jnp.where(r == c, 1.0, 0.0).astype(BF16)


def _head_sum(x, ones):
    outs = []
    for s in range(x.shape[1] // LANES):
        hi, lo = _split2(x[:, s * LANES:(s + 1) * LANES])
        outs.append(jnp.dot(hi, ones, preferred_element_type=F32)
                    + jnp.dot(lo, ones, preferred_element_type=F32))
    return outs[0] if len(outs) == 1 else jnp.concatenate(outs, axis=1)


def _rms(x, gain):
    ms = jnp.mean(x * x, axis=-1, keepdims=True)
    return x * lax.rsqrt(ms + EPS) * gain


def _norm_proj_kernel(w_is_nk, x_ref, g_ref, w_ref, o_ref, h_ref):
    @pl.when(pl.program_id(1) == 0)
    def _():
        h_ref[...] = _rms(x_ref[...], g_ref[...]).astype(BF16)

    if w_is_nk:
        out = lax.dot_general(h_ref[...], w_ref[...], (((1,), (1,)), ((), ())), preferred_element_type=F32)
    else:
        out = jnp.dot(h_ref[...], w_ref[...], preferred_element_type=F32)
    o_ref[...] = out.astype(o_ref.dtype)


def _norm_proj(x, gain, w, layer, *, tm, tn, out_dtype, w_is_nk=False):
    m, d = x.shape
    if w_is_nk:
        n = w.shape[1]
        w_spec = pl.BlockSpec((None, tn, d), lambda i, j: (layer, j, 0))
    else:
        n = w.shape[2]
        w_spec = pl.BlockSpec((None, d, tn), lambda i, j: (layer, 0, j))
    return pl.pallas_call(
        functools.partial(_norm_proj_kernel, w_is_nk),
        out_shape=jax.ShapeDtypeStruct((m, n), out_dtype),
        grid=(m // tm, n // tn),
        in_specs=[pl.BlockSpec((tm, d), lambda i, j: (i, 0)),
                  pl.BlockSpec((1, d), lambda i, j: (0, 0)),
                  w_spec],
        out_specs=pl.BlockSpec((tm, tn), lambda i, j: (i, j)),
        scratch_shapes=[pltpu.VMEM((tm, d), BF16)],
        compiler_params=_cparams(("parallel", "arbitrary")),
        name="norm_proj",
    )(x, gain, w)


def _shift(z, prev_row, next_row, mu_p, mu_n):
    rows = z.shape[0]
    ridx = lax.broadcasted_iota(jnp.int32, z.shape, 0)
    zp = jnp.where(ridx == 0, prev_row, pltpu.roll(z, 1, 0))
    zn = jnp.where(ridx == rows - 1, next_row, pltpu.roll(z, rows - 1, 0))
    return z + mu_p * (zp - z) + mu_n * (zn - z)


def _rwprep_kernel(has_vmix, *refs):
    (zr, zk, zv, zl, pr, pk, pv, pL, nr, nk, nv, nL, mup, mun, w0, wup, a0, aup, gup,
     kk_w, ka_w, rk_w) = refs[:22]
    pos = 22
    if has_vmix:
        vfirst, v0, vdown, vup = refs[pos:pos + 4]
        pos += 4
    r_o, v_o, kk_o, kd_o, lw_o, ba_o, g_o, bon_o = refs[pos:pos + 8]

    i = pl.program_id(0)
    last = pl.num_programs(0) - 1
    keep_p = jnp.where(i == 0, 0.0, 1.0)
    keep_n = jnp.where(i == last, 0.0, 1.0)

    def shifted(z_ref, p_ref, n_ref, lo, hi):
        p_row = p_ref[...].astype(F32)[HALO - 1:HALO, :] * keep_p
        n_row = n_ref[...].astype(F32)[0:1, :] * keep_n
        return _shift(z_ref[...].astype(F32), p_row, n_row, mup[:, lo:hi], mun[:, lo:hi])

    r = shifted(zr, pr, nr, 0, 1024)
    k = shifted(zk, pk, nk, 1024, 2048)
    v = shifted(zv, pv, nv, 2048, 3072)
    lora = shifted(zl, pL, nL, 3072, 3584)

    if has_vmix:
        mix = _sigmoid(v0[...] + _dot(_dot(v, vdown[...]), vup[...]))
        v = v + (vfirst[...].astype(F32) - v) * mix

    u = w0[...] + _dot(jnp.tanh(lora[:, 0:128]), wup[...])
    w_log = -_softplus(-u) - 0.5
    lw = -jnp.exp(w_log)
    a = _sigmoid(a0[...] + _dot(lora[:, 128:256], aup[...]))
    g = _dot(_sigmoid(lora[:, 256:512]), gup[...])

    ones = _head_ones()
    kk = k * kk_w[...]
    norm = jnp.sqrt(_head_sum(kk * kk, ones))
    kk = kk / jnp.maximum(norm, 1e-12)
    ka = ka_w[...]
    a_f = a[:, :RW_WIDTH]
    a_b = a[:, RW_WIDTH:]
    kd_f = k * (1.0 + (a_f - 1.0) * ka)
    kd_b = k * (1.0 + (a_b - 1.0) * ka)
    bonus = _head_sum(r * rk_w[...] * (kd_f + kd_b), ones) * v

    r_o[...] = r.astype(r_o.dtype)
    v_o[...] = v.astype(v_o.dtype)
    kk_o[...] = kk.astype(kk_o.dtype)
    kd_o[0] = kd_f.astype(kd_o.dtype)
    kd_o[1] = kd_b.astype(kd_o.dtype)
    lw_o[0] = lw[:, :RW_WIDTH]
    lw_o[1] = lw[:, RW_WIDTH:]
    ba_o[0] = (kk * a_f).astype(ba_o.dtype)
    ba_o[1] = (kk * a_b).astype(ba_o.dtype)
    g_o[...] = g.astype(g_o.dtype)
    bon_o[...] = bonus.astype(bon_o.dtype)


def _rwprep(z, mup, mun, w0, wup, a0, aup, gup, kk_w, ka_w, rk_w, vmix, *, tm):
    s = z.shape[0]
    nbh = s // HALO
    tb = tm // HALO
    has_vmix = vmix is not None

    def main(width, cb):
        return pl.BlockSpec((tm, width), lambda i: (i, cb))

    def prev(width, cb):
        return pl.BlockSpec((HALO, width), lambda i: (jnp.maximum(i * tb - 1, 0), cb))

    def nxt(width, cb):
        return pl.BlockSpec((HALO, width), lambda i: (jnp.minimum((i + 1) * tb, nbh - 1), cb))

    def full(arr):
        nd = arr.ndim
        return pl.BlockSpec(arr.shape, lambda i: (0,) * nd)

    cols = [(1024, COL_R // 1024), (1024, COL_K // 1024), (1024, COL_V // 1024), (512, COL_LORA // 512)]
    in_specs = ([main(w, c) for w, c in cols] + [prev(w, c) for w, c in cols]
                + [nxt(w, c) for w, c in cols])
    args = [z] * 12
    consts = [mup, mun, w0, wup, a0, aup, gup, kk_w, ka_w, rk_w]
    in_specs += [full(c) for c in consts]
    args += consts
    if has_vmix:
        vfirst, v0, vdown, vup = vmix
        in_specs += [pl.BlockSpec((tm, RW_WIDTH), lambda i: (i, 0)), full(v0), full(vdown), full(vup)]
        args += [vfirst, v0, vdown, vup]

    one = jax.ShapeDtypeStruct((s, RW_WIDTH), F32)
    two = jax.ShapeDtypeStruct((2, s, RW_WIDTH), F32)
    spec1 = pl.BlockSpec((tm, RW_WIDTH), lambda i: (i, 0))
    spec2 = pl.BlockSpec((2, tm, RW_WIDTH), lambda i: (0, i, 0))
    return pl.pallas_call(
        functools.partial(_rwprep_kernel, has_vmix),
        out_shape=[one, one, one, two, two, two, one, one],
        grid=(s // tm,),
        in_specs=in_specs,
        out_specs=[spec1, spec1, spec1, spec2, spec2, spec2, spec1, spec1],
        compiler_params=_cparams(("parallel",)),
        name="rwkv_prep",
    )(*args)


def _rwrec_kernel(rf_ref, vf_ref, kkf_ref, kdf_ref, lwf_ref, baf_ref,
                  rb_ref, vb_ref, kkb_ref, kdb_ref, lwb_ref, bab_ref, yf_ref, yb_ref, s_ref):
    T = CHUNK
    T2 = 2 * T
    n_slab = RW_WIDTH // LANES

    @pl.when(pl.program_id(0) == 0)
    def _():
        s_ref[...] = jnp.zeros_like(s_ref)

    row = lax.broadcasted_iota(jnp.int32, (T, T), 0)
    col = lax.broadcasted_iota(jnp.int32, (T, T), 1)
    R2 = lax.broadcasted_iota(jnp.int32, (T2, T2), 0)
    C2 = lax.broadcasted_iota(jnp.int32, (T2, T2), 1)
    same = (R2 // T) == (C2 // T)
    tdiff = (R2 % T) - (C2 % T)
    mask_d = same & (((R2 % T) // SUB) == ((C2 % T) // SUB))
    eye = R2 == C2
    head0 = lax.broadcasted_iota(jnp.int32, (T, LANES), 1) < RW_HEAD

    def stack_heads(x):
        return jnp.concatenate([jnp.where(head0, x, 0.0), jnp.where(head0, 0.0, x)], axis=0)

    def stack_dup(x):
        return jnp.concatenate([x, x], axis=0)

    cols = [slice(sl * LANES, (sl + 1) * LANES) for sl in range(n_slab)]
    ar2, bk2, v2, bke2, mask_a, mask_y, e_tot = [], [], [], [], [], [], []
    for sgn, (r_ref, v_ref, kk_ref, kd_ref, lw_ref, ba_ref) in (
            (1, (rf_ref, vf_ref, kkf_ref, kdf_ref, lwf_ref, baf_ref)),
            (-1, (rb_ref, vb_ref, kkb_ref, kdb_ref, lwb_ref, bab_ref))):
        tri = jnp.where((row - col) * sgn >= 0, 1.0, 0.0).astype(BF16)
        lw = lw_ref[...]
        hi, mid, lo = _split3(lw)
        cin = (jnp.dot(tri, hi, preferred_element_type=F32)
               + jnp.dot(tri, mid, preferred_element_type=F32)
               + jnp.dot(tri, lo, preferred_element_type=F32))
        ctot = jnp.sum(lw, axis=0, keepdims=True)
        e_in = jnp.exp(cin)
        e_ex = jnp.exp(cin - lw)
        e_neg = jnp.exp(-cin)
        e_end = jnp.exp(ctot - cin)
        etot = jnp.exp(ctot)
        kd = kd_ref[...].astype(F32)
        ba = ba_ref[...].astype(F32)
        r_t = r_ref[...].astype(F32) * e_in
        a_t = -kk_ref[...].astype(F32) * e_ex
        b_t = ba * e_neg
        k_t = kd * e_neg
        b_e = ba * e_end
        k_e = kd * e_end
        v = v_ref[...].astype(F32)
        m_a = same & (tdiff * sgn > 0)
        m_y = same & (tdiff * sgn >= 0)
        for cs in cols:
            ar2.append(jnp.concatenate([stack_heads(a_t[:, cs]), stack_heads(r_t[:, cs])],
                                       axis=0).astype(BF16))
            bk2.append(jnp.concatenate([stack_dup(b_t[:, cs]), stack_dup(k_t[:, cs])],
                                       axis=0).astype(BF16))
            v2.append(stack_heads(v[:, cs]).astype(BF16))
            bke2.append(jnp.concatenate([stack_heads(b_e[:, cs]), stack_heads(k_e[:, cs])],
                                        axis=0).astype(BF16))
            mask_a.append(m_a)
            mask_y.append(m_y)
            e_tot.append(etot[:, cs])

    slabs = range(2 * n_slab)
    sc = [_dot_nt(ar2[s], bk2[s]) for s in slabs]
    n_ab = [jnp.where(mask_a[s], sc[s][:T2, :T2], 0.0) for s in slabs]
    n_ak = [jnp.where(mask_a[s], sc[s][:T2, T2:], 0.0) for s in slabs]
    m_rbk = [jnp.concatenate([jnp.where(mask_y[s], sc[s][T2:, :T2], 0.0),
                              jnp.where(mask_y[s], sc[s][T2:, T2:], 0.0)], axis=1).astype(BF16)
             for s in slabs]

    def dot2(lhs, a, b):
        t = _dot(lhs, jnp.concatenate([a, b], axis=1))
        return t[:, :T2], t[:, T2:]

    p = [jnp.where(mask_d, n_ab[s], 0.0) for s in slabs]
    q = [jnp.where(eye, 1.0, p[s]) for s in slabs]
    p = [_dot(p[s], p[s]) for s in slabs]
    for _ in range(2):
        pq = [dot2(p[s], p[s], q[s]) for s in slabs]
        p = [pq[s][0] for s in slabs]
        q = [q[s] + pq[s][1] for s in slabs]
    q = [q[s] + _dot(p[s], q[s]) for s in slabs]
    akv = [_dot(n_ak[s], v2[s]) for s in slabs]

    st = [s_ref[s // n_slab, s % n_slab] for s in slabs]
    ars = [_dot_nt(ar2[s], st[s]) for s in slabs]
    mx = [dot2(q[s], jnp.where(mask_d, 0.0, n_ab[s]), ars[s][:T2] + akv[s]) for s in slabs]
    m = [mx[s][0] for s in slabs]
    x = [mx[s][1] for s in slabs]
    mx = [dot2(m[s], m[s], x[s]) for s in slabs]
    x = [x[s] + mx[s][1] for s in slabs]
    x = [x[s] + _dot(mx[s][0], x[s]) for s in slabs]
    for s in slabs:
        y2 = ars[s][T2:] + _dot(m_rbk[s], jnp.concatenate([x[s].astype(BF16), v2[s]], axis=0))
        y_ref = yf_ref if s < n_slab else yb_ref
        y_ref[:, cols[s % n_slab]] = y2[:T] + y2[T:]
    for s in slabs:
        upd = _dot(jnp.concatenate([x[s].T, v2[s].astype(F32).T], axis=1), bke2[s])
        s_ref[s // n_slab, s % n_slab] = st[s] * e_tot[s] + upd


def _rwrec(r, v, kk, kd, lw, ba):
    s = r.shape[0]
    nc = s // CHUNK
    fwd1 = pl.BlockSpec((CHUNK, RW_WIDTH), lambda c: (c, 0))
    bwd1 = pl.BlockSpec((CHUNK, RW_WIDTH), lambda c: (nc - 1 - c, 0))
    fwd2 = pl.BlockSpec((None, CHUNK, RW_WIDTH), lambda c: (0, c, 0))
    bwd2 = pl.BlockSpec((None, CHUNK, RW_WIDTH), lambda c: (1, nc - 1 - c, 0))
    out = jax.ShapeDtypeStruct((s, RW_WIDTH), F32)
    return pl.pallas_call(
        _rwrec_kernel,
        out_shape=[out, out],
        grid=(nc,),
        in_specs=[fwd1, fwd1, fwd1, fwd2, fwd2, fwd2, bwd1, bwd1, bwd1, bwd2, bwd2, bwd2],
        out_specs=[fwd1, bwd1],
        scratch_shapes=[pltpu.VMEM((2, RW_WIDTH // LANES, LANES, LANES), F32)],
        compiler_params=_cparams(("arbitrary",)),
        name="rwkv_recurrence",
    )(r, v, kk, kd, lw, ba, r, v, kk, kd, lw, ba)


def _atprep_kernel(zq, zkv, cos_ref, sin_ref, qn, kn, qt_o, k_o, vt_o):
    ones = _head_ones()
    cos = cos_ref[...]
    sin = sin_ref[...]
    lane = lax.broadcasted_iota(jnp.int32, cos.shape, 1)
    even = ((lane // 16) % 2) == 0

    def norm_rope(t, gain):
        ms = _head_sum(t * t, ones) * (1.0 / AT_HEAD)
        tn = t * lax.rsqrt(ms + EPS) * gain
        swapped = jnp.where(even, pltpu.roll(tn, LANES - 16, 1), pltpu.roll(tn, 16, 1))
        return tn * cos + swapped * sin

    q_scale = (AT_HEAD ** -0.5) * LOG2E
    tm = zq.shape[0]
    ones_rows = jnp.where(lax.broadcasted_iota(jnp.int32, (V_AUG - AT_HEAD, tm), 0) == 0, 1.0, 0.0).astype(BF16)
    for sl in range(AT_Q // LANES):
        out = norm_rope(zq[:, sl * LANES:(sl + 1) * LANES].astype(F32), qn[...]) * q_scale
        out_t = out.T.astype(BF16)
        qt_o[2 * sl] = out_t[:AT_HEAD]
        qt_o[2 * sl + 1] = out_t[AT_HEAD:]
    for sl in range(AT_KV // LANES):
        out = norm_rope(zkv[:, sl * LANES:(sl + 1) * LANES].astype(F32), kn[...])
        k_o[2 * sl] = out[:, :AT_HEAD].astype(BF16)
        k_o[2 * sl + 1] = out[:, AT_HEAD:].astype(BF16)
        vt = zkv[:, AT_KV + sl * LANES:AT_KV + (sl + 1) * LANES].astype(F32).T.astype(BF16)
        vt_o[2 * sl] = jnp.concatenate([vt[:AT_HEAD], ones_rows], axis=0)
        vt_o[2 * sl + 1] = jnp.concatenate([vt[AT_HEAD:], ones_rows], axis=0)


def _atprep(z, cos, sin, qn, kn, *, tm):
    s = z.shape[0]
    nb = s // tm
    return pl.pallas_call(
        _atprep_kernel,
        out_shape=[jax.ShapeDtypeStruct((AT_HEADS, AT_HEAD, s), BF16),
                   jax.ShapeDtypeStruct((AT_KV_HEADS, nb, tm, AT_HEAD), BF16),
                   jax.ShapeDtypeStruct((AT_KV_HEADS, nb, V_AUG, tm), BF16)],
        grid=(nb,),
        in_specs=[pl.BlockSpec((tm, AT_Q), lambda i: (i, COL_Q // AT_Q)),
                  pl.BlockSpec((tm, 2 * AT_KV), lambda i: (i, COL_ATKV // (2 * AT_KV))),
                  pl.BlockSpec((tm, LANES), lambda i: (i, 0)),
                  pl.BlockSpec((tm, LANES), lambda i: (i, 0)),
                  pl.BlockSpec((1, LANES), lambda i: (0, 0)),
                  pl.BlockSpec((1, LANES), lambda i: (0, 0))],
        out_specs=[pl.BlockSpec((AT_HEADS, AT_HEAD, tm), lambda i: (0, 0, i)),
                   pl.BlockSpec((AT_KV_HEADS, None, tm, AT_HEAD), lambda i: (0, i, 0, 0)),
                   pl.BlockSpec((AT_KV_HEADS, None, V_AUG, tm), lambda i: (0, i, 0, 0))],
        compiler_params=_cparams(("parallel",)),
        name="attn_prep",
    )(z, z, cos, sin, qn, kn)


def _flash_kernel(qt_ref, k_ref, vt_ref, o_ref, m_ref, acc_ref, sa_ref, sb_ref):
    grp = qt_ref.shape[0]
    nkb = k_ref.shape[0]
    m_ref[...] = jnp.full_like(m_ref, -jnp.inf)
    acc_ref[...] = jnp.zeros_like(acc_ref)

    def step(kb_cur, cur_ref, kb_next, next_ref):
        k_next = k_ref[kb_next]
        vt = vt_ref[kb_cur]
        for g in range(grp):
            next_ref[g] = jnp.dot(k_next, qt_ref[g], preferred_element_type=F32)
            st = cur_ref[g]
            m_prev = m_ref[g]
            m_new = jnp.maximum(m_prev, jnp.max(st, axis=0, keepdims=True))
            alpha = jnp.exp2(m_prev - m_new)
            pt = jnp.exp2(st - m_new).astype(BF16)
            m_ref[g] = m_new
            acc_ref[g] = alpha * acc_ref[g] + jnp.dot(vt, pt, preferred_element_type=F32)

    k0 = k_ref[0]
    for g in range(grp):
        sa_ref[g] = jnp.dot(k0, qt_ref[g], preferred_element_type=F32)

    unroll = max(u for u in (8, 4, 2) if nkb % u == 0)

    def body(j, carry):
        kb = unroll * j
        for u in range(0, unroll, 2):
            step(kb + u, sa_ref, kb + u + 1, sb_ref)
            step(kb + u + 1, sb_ref, jnp.minimum(kb + u + 2, nkb - 1), sa_ref)
        return carry

    lax.fori_loop(0, nkb // unroll, body, 0)
    for pair in range(grp // 2):
        a0 = acc_ref[2 * pair]
        a1 = acc_ref[2 * pair + 1]
        o2 = jnp.concatenate([a0[:AT_HEAD] / a0[AT_HEAD:AT_HEAD + 1],
                              a1[:AT_HEAD] / a1[AT_HEAD:AT_HEAD + 1]], axis=0)
        o_ref[:, pair * LANES:(pair + 1) * LANES] = o2.T.astype(o_ref.dtype)


def _flash(qt, k, vt, *, tq):
    s = qt.shape[2]
    _, nkb, tk, _ = k.shape
    grp = AT_HEADS // AT_KV_HEADS
    return pl.pallas_call(
        _flash_kernel,
        out_shape=jax.ShapeDtypeStruct((s, AT_Q), BF16),
        grid=(AT_KV_HEADS, s // tq),
        in_specs=[pl.BlockSpec((grp, AT_HEAD, tq), lambda h, i: (h, 0, i)),
                  pl.BlockSpec((None, nkb, tk, AT_HEAD), lambda h, i: (h, 0, 0, 0)),
                  pl.BlockSpec((None, nkb, V_AUG, tk), lambda h, i: (h, 0, 0, 0))],
        out_specs=pl.BlockSpec((tq, grp * AT_HEAD), lambda h, i: (i, h)),
        scratch_shapes=[pltpu.VMEM((grp, 1, tq), F32), pltpu.VMEM((grp, V_AUG, tq), F32),
                        pltpu.VMEM((grp, tk, tq), F32), pltpu.VMEM((grp, tk, tq), F32)],
        compiler_params=_cparams(("parallel", "parallel")),
        name="flash_attention",
    )(qt, k, vt)


def _mix_out_kernel(yf_ref, yb_ref, g_ref, bon_ref, yat_ref, zg1_ref, zg2_ref, x_ref, lnw, lnb,
                    wrw, wat, wo, npost, o_ref):
    ones = _head_ones()
    y = yf_ref[...] + yb_ref[...]
    inv = 1.0 / RW_HEAD
    mu = _head_sum(y, ones) * inv
    dlt = y - mu
    var = _head_sum(dlt * dlt, ones) * inv
    yn = dlt * lax.rsqrt(var + RW_GN_EPS) * lnw[...] + lnb[...]
    y_rw = (yn + bon_ref[...].astype(F32)) * g_ref[...].astype(F32)
    pa = _dot(y_rw, wrw[...])
    pb = jnp.dot(yat_ref[...], wat[...], preferred_element_type=F32)
    merged = _sigmoid(zg1_ref[...].astype(F32)) * pa + _sigmoid(zg2_ref[...].astype(F32)) * pb
    out = _dot(merged, wo[...])
    o_ref[...] = x_ref[...] + _rms(out, npost[...])


def _layer_block(arr, layer, **kw):
    return pl.BlockSpec((None,) + arr.shape[1:], lambda *_: (layer, 0, 0), **kw)


def _mix_out(yf, yb, g, bon, yat, z, x, lnw, lnb, wrw, wat, wo, npost, layer, *, tm):
    s = x.shape[0]

    def const(arr):
        if arr.ndim == 3:
            return _layer_block(arr, layer, pipeline_mode=pl.Buffered(1))
        return pl.BlockSpec(arr.shape, lambda i: (0, 0), pipeline_mode=pl.Buffered(1))

    row1k = pl.BlockSpec((tm, RW_WIDTH), lambda i: (i, 0))
    return pl.pallas_call(
        _mix_out_kernel,
        out_shape=jax.ShapeDtypeStruct((s, D_MODEL), F32),
        grid=(s // tm,),
        in_specs=[row1k, row1k, row1k, row1k, row1k,
                  pl.BlockSpec((tm, D_MODEL), lambda i: (i, COL_G1 // D_MODEL)),
                  pl.BlockSpec((tm, D_MODEL), lambda i: (i, COL_G2 // D_MODEL)),
                  pl.BlockSpec((tm, D_MODEL), lambda i: (i, 0)),
                  const(lnw), const(lnb), const(wrw), const(wat), const(wo), const(npost)],
        out_specs=pl.BlockSpec((tm, D_MODEL), lambda i: (i, 0)),
        compiler_params=_cparams(("parallel",)),
        name="mix_out",
    )(yf, yb, g, bon, yat, z, z, x, lnw, lnb, wrw, wat, wo, npost)


def _xattn_kernel(x_ref, kv_ref, npre, wq, wo, npost, o_ref):
    x = x_ref[...]
    h = _rms(x, npre[...])
    q = _dot(h, wq[...]) * (X_HEAD ** -0.5)
    outs = []
    for hd in range(X_HEADS):
        kh = kv_ref[:, hd * X_HEAD:(hd + 1) * X_HEAD]
        vh = kv_ref[:, X_WIDTH + hd * X_HEAD:X_WIDTH + (hd + 1) * X_HEAD]
        sc = _dot_nt(q[:, hd * X_HEAD:(hd + 1) * X_HEAD], kh)
        sc = sc - jnp.max(sc, axis=-1, keepdims=True)
        p = jnp.exp(sc)
        p = p / jnp.sum(p, axis=-1, keepdims=True)
        outs.append(_dot(p, vh))
    o = jnp.concatenate(outs, axis=1)
    c = _dot(o, wo[...])
    o_ref[...] = x + _rms(c, npost[...])


def _xattn(x, kv, npre, wq, wo, npost, layer, *, tm):
    s = x.shape[0]

    def const(arr):
        if arr.ndim == 3:
            return _layer_block(arr, layer)
        return pl.BlockSpec(arr.shape, lambda i: (0, 0))

    return pl.pallas_call(
        _xattn_kernel,
        out_shape=jax.ShapeDtypeStruct((s, D_MODEL), F32),
        grid=(s // tm,),
        in_specs=[pl.BlockSpec((tm, D_MODEL), lambda i: (i, 0)), const(kv), const(npre),
                  const(wq), const(wo), const(npost)],
        out_specs=pl.BlockSpec((tm, D_MODEL), lambda i: (i, 0)),
        compiler_params=_cparams(("parallel",)),
        name="cross_attention",
    )(x, kv, npre, wq, wo, npost)


def _ffn_kernel(x_ref, npre, wg, wu, wd, npost, o_ref, h_ref):
    j = pl.program_id(1)

    @pl.when(j == 0)
    def _():
        h_ref[...] = _rms(x_ref[...], npre[...]).astype(BF16)
        o_ref[...] = jnp.zeros_like(o_ref)

    h = h_ref[...]
    gt = jnp.dot(h, wg[...], preferred_element_type=F32)
    ut = jnp.dot(h, wu[...], preferred_element_type=F32)
    act = gt * _sigmoid(gt) * ut
    o_ref[...] += jnp.dot(act.astype(BF16), wd[...], preferred_element_type=F32)

    @pl.when(j == pl.num_programs(1) - 1)
    def _():
        o_ref[...] = x_ref[...] + _rms(o_ref[...], npost[...])


def _ffn(x, npre, wg, wu, wd, npost, layer, *, tm, tf):
    s = x.shape[0]
    f = wg.shape[2]
    return pl.pallas_call(
        _ffn_kernel,
        out_shape=jax.ShapeDtypeStruct((s, D_MODEL), F32),
        grid=(s // tm, f // tf),
        in_specs=[pl.BlockSpec((tm, D_MODEL), lambda i, j: (i, 0)),
                  pl.BlockSpec((1, D_MODEL), lambda i, j: (0, 0)),
                  pl.BlockSpec((None, D_MODEL, tf), lambda i, j: (layer, 0, j)),
                  pl.BlockSpec((None, D_MODEL, tf), lambda i, j: (layer, 0, j)),
                  pl.BlockSpec((None, tf, D_MODEL), lambda i, j: (layer, j, 0)),
                  pl.BlockSpec((1, D_MODEL), lambda i, j: (0, 0))],
        out_specs=pl.BlockSpec((tm, D_MODEL), lambda i, j: (i, 0)),
        scratch_shapes=[pltpu.VMEM((tm, D_MODEL), BF16)],
        compiler_params=_cparams(("parallel", "arbitrary")),
        name="swiglu",
    )(x, npre, wg, wu, wd, npost)


def _pack_w_in_kernel(w_ref, o_ref):
    o_lora = 3 * RW_WIDTH
    o_at = o_lora + 2 * W_LORA + 2 * A_LORA + G_LORA
    o_gate = o_at + AT_Q + 2 * AT_KV
    n_lora = o_at - o_lora
    cols = o_ref.shape[1]
    o_ref[COL_G1:COL_G1 + 2 * D_MODEL, :] = w_ref[o_gate:o_gate + 2 * D_MODEL, :].astype(BF16)
    o_ref[COL_R:COL_R + 3 * RW_WIDTH, :] = w_ref[0:3 * RW_WIDTH, :].astype(BF16)
    o_ref[COL_Q:COL_Q + AT_Q, :] = w_ref[o_at:o_at + AT_Q, :].astype(BF16)
    o_ref[COL_LORA:COL_LORA + n_lora, :] = w_ref[o_lora:o_at, :].astype(BF16)
    o_ref[COL_LORA + n_lora:COL_ATKV, :] = jnp.zeros((COL_ATKV - COL_LORA - n_lora, cols), BF16)
    o_ref[COL_ATKV:IN_PACKED, :] = w_ref[o_at + AT_Q:o_gate, :].astype(BF16)


def _pack_w_in(w_t, *, tc):
    nl, n_in, d = w_t.shape
    return pl.pallas_call(
        _pack_w_in_kernel,
        out_shape=jax.ShapeDtypeStruct((nl, IN_PACKED, d), BF16),
        grid=(nl, d // tc),
        in_specs=[pl.BlockSpec((None, n_in, tc), lambda l, i: (l, 0, i))],
        out_specs=pl.BlockSpec((None, IN_PACKED, tc), lambda l, i: (l, 0, i)),
        compiler_params=_cparams(("parallel", "parallel")),
        name="pack_w_in",
    )(w_t)


def _pack_mu(mu):
    return jnp.concatenate([mu, jnp.zeros((G_PAD - G_LORA,), mu.dtype)])[None, :]


def _block_diag_up(up):
    z = jnp.zeros_like(up[0])
    return jnp.concatenate([jnp.concatenate([up[0], z], axis=1),
                            jnp.concatenate([z, up[1]], axis=1)], axis=0).astype(BF16)


def _rope_tables(s):
    rows = s // GRID_W
    row = jnp.repeat(jnp.arange(rows), GRID_W).astype(F32)
    col = jnp.tile(jnp.arange(GRID_W), rows).astype(F32)
    n_freq = AT_HEAD // 4
    inv = ROPE_THETA ** (-jnp.arange(n_freq, dtype=F32) / n_freq)
    ar = row[:, None] * inv
    ac = col[:, None] * inv
    cos = jnp.concatenate([jnp.cos(ar), jnp.cos(ar), jnp.cos(ac), jnp.cos(ac)], axis=1)
    sin = jnp.concatenate([-jnp.sin(ar), jnp.sin(ar), -jnp.sin(ac), jnp.sin(ac)], axis=1)
    return jnp.tile(cos, (1, 2)), jnp.tile(sin, (1, 2))


def kernel(x, mem, n_mix_pre, n_mix_post, n_x_pre, n_x_post, n_ffn_pre, n_ffn_post, n_mem, w_in, rw_mu_prev, rw_mu_next, rw_w0, rw_w_up, rw_a0, rw_a_up, rw_g_up, rw_v0, rw_v_down, rw_v_up, rw_k_k, rw_k_a, rw_r_k, rw_ln_w, rw_ln_b, w_rw_out, at_q_norm, at_k_norm, w_at_out, w_o, x_wq, x_wkv, x_wo, ffn_wg, ffn_wu, ffn_wd):
    b, s, d = x.shape
    assert b == 1 and d == D_MODEL and s % 1024 == 0
    xs = x[0]
    mems = mem[0]
    cos, sin = _rope_tables(s)
    row = lambda t: t[None, :]
    w_in_p = _pack_w_in(jnp.swapaxes(w_in, 1, 2), tc=256)
    wrw_b, wat_b, wo_b = w_rw_out.astype(BF16), w_at_out.astype(BF16), w_o.astype(BF16)
    xwq_b, xwkv_b, xwo_b = x_wq.astype(BF16), x_wkv.astype(BF16), x_wo.astype(BF16)
    wg_b, wu_b, wd_b = ffn_wg.astype(BF16), ffn_wu.astype(BF16), ffn_wd.astype(BF16)
    v_first = None
    for l in range(DEPTH):
        z = _norm_proj(xs, row(n_mix_pre[l]), w_in_p, l, tm=1024, tn=1536, out_dtype=F32, w_is_nk=True)

        vmix = None
        if l > 0:
            vmix = (v_first, row(rw_v0[l - 1]), rw_v_down[l - 1].astype(BF16), rw_v_up[l - 1].astype(BF16))
        gup = jnp.concatenate([rw_g_up[l], jnp.zeros((G_PAD - G_LORA, RW_WIDTH), F32)], axis=0).astype(BF16)
        r, v, kk, kd, lw, ba, g, bon = _rwprep(
            z, _pack_mu(rw_mu_prev[l]), _pack_mu(rw_mu_next[l]),
            rw_w0[l].reshape(1, 2 * RW_WIDTH), _block_diag_up(rw_w_up[l]),
            rw_a0[l].reshape(1, 2 * RW_WIDTH), _block_diag_up(rw_a_up[l]), gup,
            row(rw_k_k[l]), row(rw_k_a[l]), rw_r_k[l].reshape(1, RW_WIDTH), vmix, tm=256)
        if l == 0:
            v_first = v
        y_f, y_b = _rwrec(r, v, kk, kd, lw, ba)

        qn = jnp.tile(at_q_norm[l], 2)[None, :]
        kn = jnp.tile(at_k_norm[l], 2)[None, :]
        q_t, k_blk, v_t = _atprep(z, cos, sin, qn, kn, tm=512)
        y_at = _flash(q_t, k_blk, v_t, tq=256)

        xs = _mix_out(y_f, y_b, g, bon, y_at, z, xs, row(rw_ln_w[l]), row(rw_ln_b[l]),
                      wrw_b, wat_b, wo_b, row(n_mix_post[l]), l, tm=256)

        kv = _norm_proj(mems, row(n_mem[l]), xwkv_b, l, tm=mems.shape[0], tn=512, out_dtype=BF16)
        xs = _xattn(xs, kv, row(n_x_pre[l]), xwq_b, xwo_b, row(n_x_post[l]), l, tm=512)

        xs = _ffn(xs, row(n_ffn_pre[l]), wg_b, wu_b, wd_b, row(n_ffn_post[l]), l, tm=512, tf=512)
    return xs[None]
```

```python
import functools

import jax
import jax.numpy as jnp
from jax import lax
from jax.experimental import pallas as pl
from jax.experimental.pallas import tpu as pltpu

F32 = jnp.float32
BF16 = jnp.bfloat16

D_MODEL = 2048
DEPTH = 4
GRID_W = 64
EPS = 1e-6
RW_HEAD = 64
RW_WIDTH = 1024
W_LORA = 64
A_LORA = 64
G_LORA = 160
RW_GN_EPS = 64e-5
AT_HEADS = 16
AT_KV_HEADS = 4
AT_HEAD = 64
AT_Q = AT_HEADS * AT_HEAD
AT_KV = AT_KV_HEADS * AT_HEAD
ROPE_THETA = 10000.0
X_HEADS = 4
X_HEAD = 128
X_WIDTH = X_HEADS * X_HEAD
D_FF = 5632

LANES = 128
SUBLANES = 8
CHUNK = 64
SUB = 16
LOG2E = 1.4426950408889634
V_AUG = AT_HEAD + 16
HALO = 16
G_PAD = 256

COL_G1 = 0
COL_G2 = 2048
COL_R = 4096
COL_K = 5120
COL_V = 6144
COL_Q = 7168
COL_LORA = 8192
COL_ATKV = 8704
IN_PACKED = 9216

VMEM_LIMIT = 56 * 1024 * 1024


def _cparams(sem):
    return pltpu.CompilerParams(dimension_semantics=sem, vmem_limit_bytes=VMEM_LIMIT)


def _dot(a, b):
    return jnp.dot(a.astype(BF16), b.astype(BF16), preferred_element_type=F32)


def _dot_nt(a, b):
    return lax.dot_general(a.astype(BF16), b.astype(BF16), (((1,), (1,)), ((), ())),
                           preferred_element_type=F32)


def _split2(x):
    hi = x.astype(BF16)
    lo = (x - hi.astype(F32)).astype(BF16)
    return hi, lo


def _split3(x):
    hi = x.astype(BF16)
    r1 = x - hi.astype(F32)
    mid = r1.astype(BF16)
    lo = (r1 - mid.astype(F32)).astype(BF16)
    return hi, mid, lo


def _sigmoid(x):
    return 1.0 / (1.0 + jnp.exp(-x))


def _softplus(x):
    return jnp.maximum(x, 0.0) + jnp.log(1.0 + jnp.exp(-jnp.abs(x)))


def _head_ones():
    r = lax.broadcasted_iota(jnp.int32, (LANES, LANES), 0) // RW_HEAD
    c = lax.broadcasted_iota(jnp.int32, (LANES, LANES), 1) // RW_HEAD
    return jnp.where(r == c, 1.0, 0.0).astype(BF16)


def _head_sum(x, ones):
    outs = []
    for s in range(x.shape[1] // LANES):
        hi, lo = _split2(x[:, s * LANES:(s + 1) * LANES])
        outs.append(jnp.dot(hi, ones, preferred_element_type=F32)
                    + jnp.dot(lo, ones, preferred_element_type=F32))
    return outs[0] if len(outs) == 1 else jnp.concatenate(outs, axis=1)


def _rms(x, gain):
    ms = jnp.mean(x * x, axis=-1, keepdims=True)
    return x * lax.rsqrt(ms + EPS) * gain


def _norm_proj_kernel(w_is_nk, x_ref, g_ref, w_ref, o_ref, h_ref):
    @pl.when(pl.program_id(1) == 0)
    def _():
        h_ref[...] = _rms(x_ref[...], g_ref[...]).astype(BF16)

    if w_is_nk:
        out = lax.dot_general(h_ref[...], w_ref[...], (((1,), (1,)), ((), ())), preferred_element_type=F32)
    else:
        out = jnp.dot(h_ref[...], w_ref[...], preferred_element_type=F32)
    o_ref[...] = out.astype(o_ref.dtype)


def _norm_proj(x, gain, w, layer, *, tm, tn, out_dtype, w_is_nk=False):
    m, d = x.shape
    if w_is_nk:
        n = w.shape[1]
        w_spec = pl.BlockSpec((None, tn, d), lambda i, j: (layer, j, 0))
    else:
        n = w.shape[2]
        w_spec = pl.BlockSpec((None, d, tn), lambda i, j: (layer, 0, j))
    return pl.pallas_call(
        functools.partial(_norm_proj_kernel, w_is_nk),
        out_shape=jax.ShapeDtypeStruct((m, n), out_dtype),
        grid=(m // tm, n // tn),
        in_specs=[pl.BlockSpec((tm, d), lambda i, j: (i, 0)),
                  pl.BlockSpec((1, d), lambda i, j: (0, 0)),
                  w_spec],
        out_specs=pl.BlockSpec((tm, tn), lambda i, j: (i, j)),
        scratch_shapes=[pltpu.VMEM((tm, d), BF16)],
        compiler_params=_cparams(("parallel", "arbitrary")),
        name="norm_proj",
    )(x, gain, w)


def _shift(z, prev_row, next_row, mu_p, mu_n):
    rows = z.shape[0]
    ridx = lax.broadcasted_iota(jnp.int32, z.shape, 0)
    zp = jnp.where(ridx == 0, prev_row, pltpu.roll(z, 1, 0))
    zn = jnp.where(ridx == rows - 1, next_row, pltpu.roll(z, rows - 1, 0))
    return z + mu_p * (zp - z) + mu_n * (zn - z)


def _rwprep_kernel(has_vmix, *refs):
    (zr, zk, zv, zl, pr, pk, pv, pL, nr, nk, nv, nL, mup, mun, w0, wup, a0, aup, gup,
     kk_w, ka_w, rk_w) = refs[:22]
    pos = 22
    if has_vmix:
        vfirst, v0, vdown, vup = refs[pos:pos + 4]
        pos += 4
    r_o, v_o, kk_o, kd_o, lw_o, ba_o, g_o, bon_o = refs[pos:pos + 8]

    i = pl.program_id(0)
    last = pl.num_programs(0) - 1
    keep_p = jnp.where(i == 0, 0.0, 1.0)
    keep_n = jnp.where(i == last, 0.0, 1.0)

    def shifted(z_ref, p_ref, n_ref, lo, hi):
        p_row = p_ref[...].astype(F32)[HALO - 1:HALO, :] * keep_p
        n_row = n_ref[...].astype(F32)[0:1, :] * keep_n
        return _shift(z_ref[...].astype(F32), p_row, n_row, mup[:, lo:hi], mun[:, lo:hi])

    r = shifted(zr, pr, nr, 0, 1024)
    k = shifted(zk, pk, nk, 1024, 2048)
    v = shifted(zv, pv, nv, 2048, 3072)
    lora = shifted(zl, pL, nL, 3072, 3584)

    if has_vmix:
        mix = _sigmoid(v0[...] + _dot(_dot(v, vdown[...]), vup[...]))
        v = v + (vfirst[...].astype(F32) - v) * mix

    u = w0[...] + _dot(jnp.tanh(lora[:, 0:128]), wup[...])
    w_log = -_softplus(-u) - 0.5
    lw = -jnp.exp(w_log)
    a = _sigmoid(a0[...] + _dot(lora[:, 128:256], aup[...]))
    g = _dot(_sigmoid(lora[:, 256:512]), gup[...])

    ones = _head_ones()
    kk = k * kk_w[...]
    norm = jnp.sqrt(_head_sum(kk * kk, ones))
    kk = kk / jnp.maximum(norm, 1e-12)
    ka = ka_w[...]
    a_f = a[:, :RW_WIDTH]
    a_b = a[:, RW_WIDTH:]
    kd_f = k * (1.0 + (a_f - 1.0) * ka)
    kd_b = k * (1.0 + (a_b - 1.0) * ka)
    bonus = _head_sum(r * rk_w[...] * (kd_f + kd_b), ones) * v

    r_o[...] = r.astype(r_o.dtype)
    v_o[...] = v.astype(v_o.dtype)
    kk_o[...] = kk.astype(kk_o.dtype)
    kd_o[0] = kd_f.astype(kd_o.dtype)
    kd_o[1] = kd_b.astype(kd_o.dtype)
    lw_o[0] = lw[:, :RW_WIDTH]
    lw_o[1] = lw[:, RW_WIDTH:]
    ba_o[0] = (kk * a_f).astype(ba_o.dtype)
    ba_o[1] = (kk * a_b).astype(ba_o.dtype)
    g_o[...] = g.astype(g_o.dtype)
    bon_o[...] = bonus.astype(bon_o.dtype)


def _rwprep(z, mup, mun, w0, wup, a0, aup, gup, kk_w, ka_w, rk_w, vmix, *, tm):
    s = z.shape[0]
    nbh = s // HALO
    tb = tm // HALO
    has_vmix = vmix is not None

    def main(width, cb):
        return pl.BlockSpec((tm, width), lambda i: (i, cb))

    def prev(width, cb):
        return pl.BlockSpec((HALO, width), lambda i: (jnp.maximum(i * tb - 1, 0), cb))

    def nxt(width, cb):
        return pl.BlockSpec((HALO, width), lambda i: (jnp.minimum((i + 1) * tb, nbh - 1), cb))

    def full(arr):
        nd = arr.ndim
        return pl.BlockSpec(arr.shape, lambda i: (0,) * nd)

    cols = [(1024, COL_R // 1024), (1024, COL_K // 1024), (1024, COL_V // 1024), (512, COL_LORA // 512)]
    in_specs = ([main(w, c) for w, c in cols] + [prev(w, c) for w, c in cols]
                + [nxt(w, c) for w, c in cols])
    args = [z] * 12
    consts = [mup, mun, w0, wup, a0, aup, gup, kk_w, ka_w, rk_w]
    in_specs += [full(c) for c in consts]
    args += consts
    if has_vmix:
        vfirst, v0, vdown, vup = vmix
        in_specs += [pl.BlockSpec((tm, RW_WIDTH), lambda i: (i, 0)), full(v0), full(vdown), full(vup)]
        args += [vfirst, v0, vdown, vup]

    one = jax.ShapeDtypeStruct((s, RW_WIDTH), F32)
    two = jax.ShapeDtypeStruct((2, s, RW_WIDTH), F32)
    spec1 = pl.BlockSpec((tm, RW_WIDTH), lambda i: (i, 0))
    spec2 = pl.BlockSpec((2, tm, RW_WIDTH), lambda i: (0, i, 0))
    return pl.pallas_call(
        functools.partial(_rwprep_kernel, has_vmix),
        out_shape=[one, one, one, two, two, two, one, one],
        grid=(s // tm,),
        in_specs=in_specs,
        out_specs=[spec1, spec1, spec1, spec2, spec2, spec2, spec1, spec1],
        compiler_params=_cparams(("parallel",)),
        name="rwkv_prep",
    )(*args)


def _rwrec_kernel(rf_ref, vf_ref, kkf_ref, kdf_ref, lwf_ref, baf_ref,
                  rb_ref, vb_ref, kkb_ref, kdb_ref, lwb_ref, bab_ref, yf_ref, yb_ref, s_ref):
    T = CHUNK
    T2 = 2 * T
    n_slab = RW_WIDTH // LANES

    @pl.when(pl.program_id(0) == 0)
    def _():
        s_ref[...] = jnp.zeros_like(s_ref)

    row = lax.broadcasted_iota(jnp.int32, (T, T), 0)
    col = lax.broadcasted_iota(jnp.int32, (T, T), 1)
    R2 = lax.broadcasted_iota(jnp.int32, (T2, T2), 0)
    C2 = lax.broadcasted_iota(jnp.int32, (T2, T2), 1)
    same = (R2 // T) == (C2 // T)
    tdiff = (R2 % T) - (C2 % T)
    mask_d = same & (((R2 % T) // SUB) == ((C2 % T) // SUB))
    eye = R2 == C2
    head0 = lax.broadcasted_iota(jnp.int32, (T, LANES), 1) < RW_HEAD

    def stack_heads(x):
        return jnp.concatenate([jnp.where(head0, x, 0.0), jnp.where(head0, 0.0, x)], axis=0)

    def stack_dup(x):
        return jnp.concatenate([x, x], axis=0)

    cols = [slice(sl * LANES, (sl + 1) * LANES) for sl in range(n_slab)]
    ar2, bk2, v2, bke2, mask_a, mask_y, e_tot = [], [], [], [], [], [], []
    for sgn, (r_ref, v_ref, kk_ref, kd_ref, lw_ref, ba_ref) in (
            (1, (rf_ref, vf_ref, kkf_ref, kdf_ref, lwf_ref, baf_ref)),
            (-1, (rb_ref, vb_ref, kkb_ref, kdb_ref, lwb_ref, bab_ref))):
        tri = jnp.where((row - col) * sgn >= 0, 1.0, 0.0).astype(BF16)
        lw = lw_ref[...]
        hi, mid, lo = _split3(lw)
        cin = (jnp.dot(tri, hi, preferred_element_type=F32)
               + jnp.dot(tri, mid, preferred_element_type=F32)
               + jnp.dot(tri, lo, preferred_element_type=F32))
        ctot = jnp.sum(lw, axis=0, keepdims=True)
        e_in = jnp.exp(cin)
        e_ex = jnp.exp(cin - lw)
        e_neg = jnp.exp(-cin)
        e_end = jnp.exp(ctot - cin)
        etot = jnp.exp(ctot)
        kd = kd_ref[...].astype(F32)
        ba = ba_ref[...].astype(F32)
        r_t = r_ref[...].astype(F32) * e_in
        a_t = -kk_ref[...].astype(F32) * e_ex
        b_t = ba * e_neg
        k_t = kd * e_neg
        b_e = ba * e_end
        k_e = kd * e_end
        v = v_ref[...].astype(F32)
        m_a = same & (tdiff * sgn > 0)
        m_y = same & (tdiff * sgn >= 0)
        for cs in cols:
            ar2.append(jnp.concatenate([stack_heads(a_t[:, cs]), stack_heads(r_t[:, cs])],
                                       axis=0).astype(BF16))
            bk2.append(jnp.concatenate([stack_dup(b_t[:, cs]), stack_dup(k_t[:, cs])],
                                       axis=0).astype(BF16))
            v2.append(stack_heads(v[:, cs]).astype(BF16))
            bke2.append(jnp.concatenate([stack_heads(b_e[:, cs]), stack_heads(k_e[:, cs])],
                                        axis=0).astype(BF16))
            mask_a.append(m_a)
            mask_y.append(m_y)
            e_tot.append(etot[:, cs])

    slabs = range(2 * n_slab)
    sc = [_dot_nt(ar2[s], bk2[s]) for s in slabs]
    n_ab = [jnp.where(mask_a[s], sc[s][:T2, :T2], 0.0) for s in slabs]
    n_ak = [jnp.where(mask_a[s], sc[s][:T2, T2:], 0.0) for s in slabs]
    m_rbk = [jnp.concatenate([jnp.where(mask_y[s], sc[s][T2:, :T2], 0.0),
                              jnp.where(mask_y[s], sc[s][T2:, T2:], 0.0)], axis=1).astype(BF16)
             for s in slabs]

    def dot2(lhs, a, b):
        t = _dot(lhs, jnp.concatenate([a, b], axis=1))
        return t[:, :T2], t[:, T2:]

    p = [jnp.where(mask_d, n_ab[s], 0.0) for s in slabs]
    q = [jnp.where(eye, 1.0, p[s]) for s in slabs]
    p = [_dot(p[s], p[s]) for s in slabs]
    for _ in range(2):
        pq = [dot2(p[s], p[s], q[s]) for s in slabs]
        p = [pq[s][0] for s in slabs]
        q = [q[s] + pq[s][1] for s in slabs]
    q = [q[s] + _dot(p[s], q[s]) for s in slabs]
    akv = [_dot(n_ak[s], v2[s]) for s in slabs]

    st = [s_ref[s // n_slab, s % n_slab] for s in slabs]
    ars = [_dot_nt(ar2[s], st[s]) for s in slabs]
    mx = [dot2(q[s], jnp.where(mask_d, 0.0, n_ab[s]), ars[s][:T2] + akv[s]) for s in slabs]
    m = [mx[s][0] for s in slabs]
    x = [mx[s][1] for s in slabs]
    mx = [dot2(m[s], m[s], x[s]) for s in slabs]
    x = [x[s] + mx[s][1] for s in slabs]
    x = [x[s] + _dot(mx[s][0], x[s]) for s in slabs]
    for s in slabs:
        y2 = ars[s][T2:] + _dot(m_rbk[s], jnp.concatenate([x[s].astype(BF16), v2[s]], axis=0))
        y_ref = yf_ref if s < n_slab else yb_ref
        y_ref[:, cols[s % n_slab]] = y2[:T] + y2[T:]
    for s in slabs:
        upd = _dot(jnp.concatenate([x[s].T, v2[s].astype(F32).T], axis=1), bke2[s])
        s_ref[s // n_slab, s % n_slab] = st[s] * e_tot[s] + upd


def _rwrec(r, v, kk, kd, lw, ba):
    s = r.shape[0]
    nc = s // CHUNK
    fwd1 = pl.BlockSpec((CHUNK, RW_WIDTH), lambda c: (c, 0))
    bwd1 = pl.BlockSpec((CHUNK, RW_WIDTH), lambda c: (nc - 1 - c, 0))
    fwd2 = pl.BlockSpec((None, CHUNK, RW_WIDTH), lambda c: (0, c, 0))
    bwd2 = pl.BlockSpec((None, CHUNK, RW_WIDTH), lambda c: (1, nc - 1 - c, 0))
    out = jax.ShapeDtypeStruct((s, RW_WIDTH), F32)
    return pl.pallas_call(
        _rwrec_kernel,
        out_shape=[out, out],
        grid=(nc,),
        in_specs=[fwd1, fwd1, fwd1, fwd2, fwd2, fwd2, bwd1, bwd1, bwd1, bwd2, bwd2, bwd2],
        out_specs=[fwd1, bwd1],
        scratch_shapes=[pltpu.VMEM((2, RW_WIDTH // LANES, LANES, LANES), F32)],
        compiler_params=_cparams(("arbitrary",)),
        name="rwkv_recurrence",
    )(r, v, kk, kd, lw, ba, r, v, kk, kd, lw, ba)


def _atprep_kernel(zq, zkv, cos_ref, sin_ref, qn, kn, qt_o, k_o, vt_o):
    ones = _head_ones()
    cos = cos_ref[...]
    sin = sin_ref[...]
    lane = lax.broadcasted_iota(jnp.int32, cos.shape, 1)
    even = ((lane // 16) % 2) == 0

    def norm_rope(t, gain):
        ms = _head_sum(t * t, ones) * (1.0 / AT_HEAD)
        tn = t * lax.rsqrt(ms + EPS) * gain
        swapped = jnp.where(even, pltpu.roll(tn, LANES - 16, 1), pltpu.roll(tn, 16, 1))
        return tn * cos + swapped * sin

    q_scale = (AT_HEAD ** -0.5) * LOG2E
    tm = zq.shape[0]
    ones_rows = jnp.where(lax.broadcasted_iota(jnp.int32, (V_AUG - AT_HEAD, tm), 0) == 0, 1.0, 0.0).astype(BF16)
    for sl in range(AT_Q // LANES):
        out = norm_rope(zq[:, sl * LANES:(sl + 1) * LANES].astype(F32), qn[...]) * q_scale
        out_t = out.T.astype(BF16)
        tq = qt_o.shape[3]
        for qi in range(qt_o.shape[1]):
            qt_o[2 * sl, qi] = out_t[:AT_HEAD, qi * tq:(qi + 1) * tq]
            qt_o[2 * sl + 1, qi] = out_t[AT_HEAD:, qi * tq:(qi + 1) * tq]
    for sl in range(AT_KV // LANES):
        out = norm_rope(zkv[:, sl * LANES:(sl + 1) * LANES].astype(F32), kn[...])
        k_o[2 * sl] = out[:, :AT_HEAD].astype(BF16)
        k_o[2 * sl + 1] = out[:, AT_HEAD:].astype(BF16)
        vt = zkv[:, AT_KV + sl * LANES:AT_KV + (sl + 1) * LANES].astype(F32).T.astype(BF16)
        vt_o[2 * sl] = jnp.concatenate([vt[:AT_HEAD], ones_rows], axis=0)
        vt_o[2 * sl + 1] = jnp.concatenate([vt[AT_HEAD:], ones_rows], axis=0)


def _atprep(z, cos, sin, qn, kn, *, tm, tq):
    s = z.shape[0]
    nb = s // tm
    return pl.pallas_call(
        _atprep_kernel,
        out_shape=[jax.ShapeDtypeStruct((AT_HEADS, s // tq, AT_HEAD, tq), BF16),
                   jax.ShapeDtypeStruct((AT_KV_HEADS, nb, tm, AT_HEAD), BF16),
                   jax.ShapeDtypeStruct((AT_KV_HEADS, nb, V_AUG, tm), BF16)],
        grid=(nb,),
        in_specs=[pl.BlockSpec((tm, AT_Q), lambda i: (i, COL_Q // AT_Q)),
                  pl.BlockSpec((tm, 2 * AT_KV), lambda i: (i, COL_ATKV // (2 * AT_KV))),
                  pl.BlockSpec((tm, LANES), lambda i: (i, 0)),
                  pl.BlockSpec((tm, LANES), lambda i: (i, 0)),
                  pl.BlockSpec((1, LANES), lambda i: (0, 0)),
                  pl.BlockSpec((1, LANES), lambda i: (0, 0))],
        out_specs=[pl.BlockSpec((AT_HEADS, tm // tq, AT_HEAD, tq), lambda i: (0, i, 0, 0)),
                   pl.BlockSpec((AT_KV_HEADS, None, tm, AT_HEAD), lambda i: (0, i, 0, 0)),
                   pl.BlockSpec((AT_KV_HEADS, None, V_AUG, tm), lambda i: (0, i, 0, 0))],
        compiler_params=_cparams(("parallel",)),
        name="attn_prep",
    )(z, z, cos, sin, qn, kn)


def _flash_kernel(qt_ref, k_ref, vt_ref, o_ref, m_ref, acc_ref, sa_ref, sb_ref):
    grp, n_tile, _, tq = qt_ref.shape
    nkb = k_ref.shape[0]
    m_ref[...] = jnp.full_like(m_ref, -jnp.inf)
    acc_ref[...] = jnp.zeros_like(acc_ref)

    def step(t_cur, kb_cur, cur_ref, t_next, kb_next, next_ref):
        k_next = k_ref[kb_next]
        vt = vt_ref[kb_cur]
        for g in range(grp):
            next_ref[g] = jnp.dot(k_next, qt_ref[g, t_next], preferred_element_type=F32)
            st = cur_ref[g]
            m_prev = m_ref[t_cur, g]
            m_new = jnp.maximum(m_prev, jnp.max(st, axis=0, keepdims=True))
            alpha = jnp.exp2(m_prev - m_new)
            pt = jnp.exp2(st - m_new).astype(BF16)
            m_ref[t_cur, g] = m_new
            acc_ref[t_cur, g] = alpha * acc_ref[t_cur, g] + jnp.dot(vt, pt, preferred_element_type=F32)

    k0 = k_ref[0]
    for g in range(grp):
        sa_ref[g] = jnp.dot(k0, qt_ref[g, 0], preferred_element_type=F32)

    unroll = max(u for u in (8, 4, 2) if nkb % u == 0)
    trips_per_tile = nkb // unroll

    def body(j, carry):
        t = j // trips_per_tile
        kb = unroll * (j % trips_per_tile)
        for u in range(0, unroll - 2, 2):
            step(t, kb + u, sa_ref, t, kb + u + 1, sb_ref)
            step(t, kb + u + 1, sb_ref, t, kb + u + 2, sa_ref)
        u = unroll - 2
        step(t, kb + u, sa_ref, t, kb + u + 1, sb_ref)
        wrap = (kb + unroll == nkb).astype(jnp.int32)
        step(t, kb + u + 1, sb_ref, jnp.minimum(t + wrap, n_tile - 1), (kb + unroll) * (1 - wrap), sa_ref)
        return carry

    lax.fori_loop(0, n_tile * trips_per_tile, body, 0)
    for t in range(n_tile):
        for pair in range(grp // 2):
            a0 = acc_ref[t, 2 * pair]
            a1 = acc_ref[t, 2 * pair + 1]
            o2 = jnp.concatenate([a0[:AT_HEAD] / a0[AT_HEAD:AT_HEAD + 1],
                                  a1[:AT_HEAD] / a1[AT_HEAD:AT_HEAD + 1]], axis=0)
            o_ref[t * tq:(t + 1) * tq, pair * LANES:(pair + 1) * LANES] = o2.T.astype(o_ref.dtype)


def _flash(qt, k, vt, *, n_tile):
    _, nqt, _, tq = qt.shape
    _, nkb, tk, _ = k.shape
    grp = AT_HEADS // AT_KV_HEADS
    return pl.pallas_call(
        _flash_kernel,
        out_shape=jax.ShapeDtypeStruct((nqt * tq, AT_Q), BF16),
        grid=(AT_KV_HEADS, nqt // n_tile),
        in_specs=[pl.BlockSpec((grp, n_tile, AT_HEAD, tq), lambda h, i: (h, i, 0, 0)),
                  pl.BlockSpec((None, nkb, tk, AT_HEAD), lambda h, i: (h, 0, 0, 0)),
                  pl.BlockSpec((None, nkb, V_AUG, tk), lambda h, i: (h, 0, 0, 0))],
        out_specs=pl.BlockSpec((n_tile * tq, grp * AT_HEAD), lambda h, i: (i, h)),
        scratch_shapes=[pltpu.VMEM((n_tile, grp, 1, tq), F32), pltpu.VMEM((n_tile, grp, V_AUG, tq), F32),
                        pltpu.VMEM((grp, tk, tq), F32), pltpu.VMEM((grp, tk, tq), F32)],
        compiler_params=_cparams(("parallel", "parallel")),
        name="flash_attention",
    )(qt, k, vt)


def _mix_out_kernel(yf_ref, yb_ref, g_ref, bon_ref, yat_ref, zg1_ref, zg2_ref, x_ref, lnw, lnb,
                    wrw, wat, wo, npost, o_ref):
    ones = _head_ones()
    y = yf_ref[...] + yb_ref[...]
    inv = 1.0 / RW_HEAD
    mu = _head_sum(y, ones) * inv
    dlt = y - mu
    var = _head_sum(dlt * dlt, ones) * inv
    yn = dlt * lax.rsqrt(var + RW_GN_EPS) * lnw[...] + lnb[...]
    y_rw = (yn + bon_ref[...].astype(F32)) * g_ref[...].astype(F32)
    pa = _dot(y_rw, wrw[...])
    pb = jnp.dot(yat_ref[...], wat[...], preferred_element_type=F32)
    merged = _sigmoid(zg1_ref[...].astype(F32)) * pa + _sigmoid(zg2_ref[...].astype(F32)) * pb
    out = _dot(merged, wo[...])
    o_ref[...] = x_ref[...] + _rms(out, npost[...])


def _layer_block(arr, layer, **kw):
    return pl.BlockSpec((None,) + arr.shape[1:], lambda *_: (layer, 0, 0), **kw)


def _mix_out(yf, yb, g, bon, yat, z, x, lnw, lnb, wrw, wat, wo, npost, layer, *, tm):
    s = x.shape[0]

    def const(arr):
        if arr.ndim == 3:
            return _layer_block(arr, layer, pipeline_mode=pl.Buffered(1))
        return pl.BlockSpec(arr.shape, lambda i: (0, 0), pipeline_mode=pl.Buffered(1))

    row1k = pl.BlockSpec((tm, RW_WIDTH), lambda i: (i, 0))
    return pl.pallas_call(
        _mix_out_kernel,
        out_shape=jax.ShapeDtypeStruct((s, D_MODEL), F32),
        grid=(s // tm,),
        in_specs=[row1k, row1k, row1k, row1k, row1k,
                  pl.BlockSpec((tm, D_MODEL), lambda i: (i, COL_G1 // D_MODEL)),
                  pl.BlockSpec((tm, D_MODEL), lambda i: (i, COL_G2 // D_MODEL)),
                  pl.BlockSpec((tm, D_MODEL), lambda i: (i, 0)),
                  const(lnw), const(lnb), const(wrw), const(wat), const(wo), const(npost)],
        out_specs=pl.BlockSpec((tm, D_MODEL), lambda i: (i, 0)),
        compiler_params=_cparams(("parallel",)),
        name="mix_out",
    )(yf, yb, g, bon, yat, z, z, x, lnw, lnb, wrw, wat, wo, npost)


def _xattn_kernel(x_ref, kv_ref, npre, wq, wo, npost, o_ref):
    x = x_ref[...]
    h = _rms(x, npre[...])
    q = _dot(h, wq[...]) * (X_HEAD ** -0.5)
    outs = []
    for hd in range(X_HEADS):
        kh = kv_ref[:, hd * X_HEAD:(hd + 1) * X_HEAD]
        vh = kv_ref[:, X_WIDTH + hd * X_HEAD:X_WIDTH + (hd + 1) * X_HEAD]
        sc = _dot_nt(q[:, hd * X_HEAD:(hd + 1) * X_HEAD], kh)
        sc = sc - jnp.max(sc, axis=-1, keepdims=True)
        p = jnp.exp(sc)
        p = p / jnp.sum(p, axis=-1, keepdims=True)
        outs.append(_dot(p, vh))
    o = jnp.concatenate(outs, axis=1)
    c = _dot(o, wo[...])
    o_ref[...] = x + _rms(c, npost[...])


def _xattn(x, kv, npre, wq, wo, npost, layer, *, tm):
    s = x.shape[0]

    def const(arr):
        if arr.ndim == 3:
            return _layer_block(arr, layer)
        return pl.BlockSpec(arr.shape, lambda i: (0, 0))

    return pl.pallas_call(
        _xattn_kernel,
        out_shape=jax.ShapeDtypeStruct((s, D_MODEL), F32),
        grid=(s // tm,),
        in_specs=[pl.BlockSpec((tm, D_MODEL), lambda i: (i, 0)), const(kv), const(npre),
                  const(wq), const(wo), const(npost)],
        out_specs=pl.BlockSpec((tm, D_MODEL), lambda i: (i, 0)),
        compiler_params=_cparams(("parallel",)),
        name="cross_attention",
    )(x, kv, npre, wq, wo, npost)


def _ffn_kernel(x_ref, npre, wg, wu, wd, npost, o_ref, h_ref):
    j = pl.program_id(1)

    @pl.when(j == 0)
    def _():
        h_ref[...] = _rms(x_ref[...], npre[...]).astype(BF16)
        o_ref[...] = jnp.zeros_like(o_ref)

    h = h_ref[...]
    gt = jnp.dot(h, wg[...], preferred_element_type=F32)
    ut = jnp.dot(h, wu[...], preferred_element_type=F32)
    act = gt * _sigmoid(gt) * ut
    o_ref[...] += jnp.dot(act.astype(BF16), wd[...], preferred_element_type=F32)

    @pl.when(j == pl.num_programs(1) - 1)
    def _():
        o_ref[...] = x_ref[...] + _rms(o_ref[...], npost[...])


def _ffn(x, npre, wg, wu, wd, npost, layer, *, tm, tf):
    s = x.shape[0]
    f = wg.shape[2]
    return pl.pallas_call(
        _ffn_kernel,
        out_shape=jax.ShapeDtypeStruct((s, D_MODEL), F32),
        grid=(s // tm, f // tf),
        in_specs=[pl.BlockSpec((tm, D_MODEL), lambda i, j: (i, 0)),
                  pl.BlockSpec((1, D_MODEL), lambda i, j: (0, 0)),
                  pl.BlockSpec((None, D_MODEL, tf), lambda i, j: (layer, 0, j)),
                  pl.BlockSpec((None, D_MODEL, tf), lambda i, j: (layer, 0, j)),
                  pl.BlockSpec((None, tf, D_MODEL), lambda i, j: (layer, j, 0)),
                  pl.BlockSpec((1, D_MODEL), lambda i, j: (0, 0))],
        out_specs=pl.BlockSpec((tm, D_MODEL), lambda i, j: (i, 0)),
        scratch_shapes=[pltpu.VMEM((tm, D_MODEL), BF16)],
        compiler_params=_cparams(("parallel", "arbitrary")),
        name="swiglu",
    )(x, npre, wg, wu, wd, npost)


def _pack_w_in_kernel(w_ref, o_ref):
    o_lora = 3 * RW_WIDTH
    o_at = o_lora + 2 * W_LORA + 2 * A_LORA + G_LORA
    o_gate = o_at + AT_Q + 2 * AT_KV
    n_lora = o_at - o_lora
    cols = o_ref.shape[1]
    o_ref[COL_G1:COL_G1 + 2 * D_MODEL, :] = w_ref[o_gate:o_gate + 2 * D_MODEL, :].astype(BF16)
    o_ref[COL_R:COL_R + 3 * RW_WIDTH, :] = w_ref[0:3 * RW_WIDTH, :].astype(BF16)
    o_ref[COL_Q:COL_Q + AT_Q, :] = w_ref[o_at:o_at + AT_Q, :].astype(BF16)
    o_ref[COL_LORA:COL_LORA + n_lora, :] = w_ref[o_lora:o_at, :].astype(BF16)
    o_ref[COL_LORA + n_lora:COL_ATKV, :] = jnp.zeros((COL_ATKV - COL_LORA - n_lora, cols), BF16)
    o_ref[COL_ATKV:IN_PACKED, :] = w_ref[o_at + AT_Q:o_gate, :].astype(BF16)


def _pack_w_in(w_t, *, tc):
    nl, n_in, d = w_t.shape
    return pl.pallas_call(
        _pack_w_in_kernel,
        out_shape=jax.ShapeDtypeStruct((nl, IN_PACKED, d), BF16),
        grid=(nl, d // tc),
        in_specs=[pl.BlockSpec((None, n_in, tc), lambda l, i: (l, 0, i))],
        out_specs=pl.BlockSpec((None, IN_PACKED, tc), lambda l, i: (l, 0, i)),
        compiler_params=_cparams(("parallel", "parallel")),
        name="pack_w_in",
    )(w_t)


def _pack_mu(mu):
    return jnp.concatenate([mu, jnp.zeros((G_PAD - G_LORA,), mu.dtype)])[None, :]


def _block_diag_up(up):
    z = jnp.zeros_like(up[0])
    return jnp.concatenate([jnp.concatenate([up[0], z], axis=1),
                            jnp.concatenate([z, up[1]], axis=1)], axis=0).astype(BF16)


def _rope_tables(s):
    rows = s // GRID_W
    row = jnp.repeat(jnp.arange(rows), GRID_W).astype(F32)
    col = jnp.tile(jnp.arange(GRID_W), rows).astype(F32)
    n_freq = AT_HEAD // 4
    inv = ROPE_THETA ** (-jnp.arange(n_freq, dtype=F32) / n_freq)
    ar = row[:, None] * inv
    ac = col[:, None] * inv
    cos = jnp.concatenate([jnp.cos(ar), jnp.cos(ar), jnp.cos(ac), jnp.cos(ac)], axis=1)
    sin = jnp.concatenate([-jnp.sin(ar), jnp.sin(ar), -jnp.sin(ac), jnp.sin(ac)], axis=1)
    return jnp.tile(cos, (1, 2)), jnp.tile(sin, (1, 2))


def kernel(x, mem, n_mix_pre, n_mix_post, n_x_pre, n_x_post, n_ffn_pre, n_ffn_post, n_mem, w_in, rw_mu_prev, rw_mu_next, rw_w0, rw_w_up, rw_a0, rw_a_up, rw_g_up, rw_v0, rw_v_down, rw_v_up, rw_k_k, rw_k_a, rw_r_k, rw_ln_w, rw_ln_b, w_rw_out, at_q_norm, at_k_norm, w_at_out, w_o, x_wq, x_wkv, x_wo, ffn_wg, ffn_wu, ffn_wd):
    b, s, d = x.shape
    assert b == 1 and d == D_MODEL and s % 1024 == 0
    xs = x[0]
    mems = mem[0]
    cos, sin = _rope_tables(s)
    row = lambda t: t[None, :]
    w_in_p = _pack_w_in(jnp.swapaxes(w_in, 1, 2), tc=256)
    wrw_b, wat_b, wo_b = w_rw_out.astype(BF16), w_at_out.astype(BF16), w_o.astype(BF16)
    xwq_b, xwkv_b, xwo_b = x_wq.astype(BF16), x_wkv.astype(BF16), x_wo.astype(BF16)
    wg_b, wu_b, wd_b = ffn_wg.astype(BF16), ffn_wu.astype(BF16), ffn_wd.astype(BF16)
    v_first = None
    for l in range(DEPTH):
        z = _norm_proj(xs, row(n_mix_pre[l]), w_in_p, l, tm=1024, tn=1536, out_dtype=F32, w_is_nk=True)

        vmix = None
        if l > 0:
            vmix = (v_first, row(rw_v0[l - 1]), rw_v_down[l - 1].astype(BF16), rw_v_up[l - 1].astype(BF16))
        gup = jnp.concatenate([rw_g_up[l], jnp.zeros((G_PAD - G_LORA, RW_WIDTH), F32)], axis=0).astype(BF16)
        r, v, kk, kd, lw, ba, g, bon = _rwprep(
            z, _pack_mu(rw_mu_prev[l]), _pack_mu(rw_mu_next[l]),
            rw_w0[l].reshape(1, 2 * RW_WIDTH), _block_diag_up(rw_w_up[l]),
            rw_a0[l].reshape(1, 2 * RW_WIDTH), _block_diag_up(rw_a_up[l]), gup,
            row(rw_k_k[l]), row(rw_k_a[l]), rw_r_k[l].reshape(1, RW_WIDTH), vmix, tm=256)
        if l == 0:
            v_first = v
        y_f, y_b = _rwrec(r, v, kk, kd, lw, ba)

        qn = jnp.tile(at_q_norm[l], 2)[None, :]
        kn = jnp.tile(at_k_norm[l], 2)[None, :]
        tq = 256
        q_t, k_blk, v_t = _atprep(z, cos, sin, qn, kn, tm=512, tq=tq)
        y_at = _flash(q_t, k_blk, v_t, n_tile=min(8, s // tq))

        xs = _mix_out(y_f, y_b, g, bon, y_at, z, xs, row(rw_ln_w[l]), row(rw_ln_b[l]),
                      wrw_b, wat_b, wo_b, row(n_mix_post[l]), l, tm=256)

        kv = _norm_proj(mems, row(n_mem[l]), xwkv_b, l, tm=mems.shape[0], tn=512, out_dtype=BF16)
        xs = _xattn(xs, kv, row(n_x_pre[l]), xwq_b, xwo_b, row(n_x_post[l]), l, tm=512)

        xs = _ffn(xs, row(n_ffn_pre[l]), wg_b, wu_b, wd_b, row(n_ffn_post[l]), l, tm=512, tf=512)
    return xs[None]
```

```python
import functools

import jax
import jax.numpy as jnp
from jax import lax
from jax.experimental import pallas as pl
from jax.experimental.pallas import tpu as pltpu

F32 = jnp.float32
BF16 = jnp.bfloat16

D_MODEL = 2048
DEPTH = 4
GRID_W = 64
EPS = 1e-6
RW_HEAD = 64
RW_WIDTH = 1024
W_LORA = 64
A_LORA = 64
G_LORA = 160
RW_GN_EPS = 64e-5
AT_HEADS = 16
AT_KV_HEADS = 4
AT_HEAD = 64
AT_Q = AT_HEADS * AT_HEAD
AT_KV = AT_KV_HEADS * AT_HEAD
ROPE_THETA = 10000.0
X_HEADS = 4
X_HEAD = 128
X_WIDTH = X_HEADS * X_HEAD
D_FF = 5632

LANES = 128
SUBLANES = 8
CHUNK = 64
SUB = 16
LOG2E = 1.4426950408889634
EXP_NEG_HALF = 0.6065306597126334
V_AUG = AT_HEAD + 16
HALO = 16
G_PAD = 256

COL_G1 = 0
COL_G2 = 2048
COL_R = 4096
COL_K = 5120
COL_V = 6144
COL_Q = 7168
COL_LORA = 8192
COL_ATKV = 8704
IN_PACKED = 9216

VMEM_LIMIT = 56 * 1024 * 1024


def _cparams(sem):
    return pltpu.CompilerParams(dimension_semantics=sem, vmem_limit_bytes=VMEM_LIMIT)


def _dot(a, b):
    return jnp.dot(a.astype(BF16), b.astype(BF16), preferred_element_type=F32)


def _dot_nt(a, b):
    return lax.dot_general(a.astype(BF16), b.astype(BF16), (((1,), (1,)), ((), ())),
                           preferred_element_type=F32)


def _split2(x):
    hi = x.astype(BF16)
    lo = (x - hi.astype(F32)).astype(BF16)
    return hi, lo


def _split3(x):
    hi = x.astype(BF16)
    r1 = x - hi.astype(F32)
    mid = r1.astype(BF16)
    lo = (r1 - mid.astype(F32)).astype(BF16)
    return hi, mid, lo


def _sigmoid(x):
    return 1.0 / (1.0 + jnp.exp(-x))


def _head_ones():
    r = lax.broadcasted_iota(jnp.int32, (LANES, LANES), 0) // RW_HEAD
    c = lax.broadcasted_iota(jnp.int32, (LANES, LANES), 1) // RW_HEAD
    return jnp.where(r == c, 1.0, 0.0).astype(BF16)


def _head_sum(x, ones):
    outs = []
    for s in range(x.shape[1] // LANES):
        hi, lo = _split2(x[:, s * LANES:(s + 1) * LANES])
        outs.append(jnp.dot(hi, ones, preferred_element_type=F32)
                    + jnp.dot(lo, ones, preferred_element_type=F32))
    return outs[0] if len(outs) == 1 else jnp.concatenate(outs, axis=1)


def _rms(x, gain):
    ms = jnp.mean(x * x, axis=-1, keepdims=True)
    return x * lax.rsqrt(ms + EPS) * gain


def _norm_proj_kernel(w_is_nk, x_ref, g_ref, w_ref, o_ref, h_ref):
    @pl.when(pl.program_id(1) == 0)
    def _():
        h_ref[...] = _rms(x_ref[...], g_ref[...]).astype(BF16)

    if w_is_nk:
        out = lax.dot_general(h_ref[...], w_ref[...], (((1,), (1,)), ((), ())), preferred_element_type=F32)
    else:
        out = jnp.dot(h_ref[...], w_ref[...], preferred_element_type=F32)
    o_ref[...] = out.astype(o_ref.dtype)


def _norm_proj(x, gain, w, layer, *, tm, tn, out_dtype, w_is_nk=False):
    m, d = x.shape
    if w_is_nk:
        n = w.shape[1]
        w_spec = pl.BlockSpec((None, tn, d), lambda i, j: (layer, j, 0))
    else:
        n = w.shape[2]
        w_spec = pl.BlockSpec((None, d, tn), lambda i, j: (layer, 0, j))
    return pl.pallas_call(
        functools.partial(_norm_proj_kernel, w_is_nk),
        out_shape=jax.ShapeDtypeStruct((m, n), out_dtype),
        grid=(m // tm, n // tn),
        in_specs=[pl.BlockSpec((tm, d), lambda i, j: (i, 0)),
                  pl.BlockSpec((1, d), lambda i, j: (0, 0)),
                  w_spec],
        out_specs=pl.BlockSpec((tm, tn), lambda i, j: (i, j)),
        scratch_shapes=[pltpu.VMEM((tm, d), BF16)],
        compiler_params=_cparams(("parallel", "arbitrary")),
        name="norm_proj",
    )(x, gain, w)


def _shift(z, prev_row, next_row, mu_p, mu_n):
    rows = z.shape[0]
    ridx = lax.broadcasted_iota(jnp.int32, z.shape, 0)
    zp = jnp.where(ridx == 0, prev_row, pltpu.roll(z, 1, 0))
    zn = jnp.where(ridx == rows - 1, next_row, pltpu.roll(z, rows - 1, 0))
    return (1.0 - mu_p - mu_n) * z + mu_p * zp + mu_n * zn


def _rwprep_kernel(has_vmix, *refs):
    (zr, zk, zv, zl, pr, pk, pv, pL, nr, nk, nv, nL, mup, mun, w0, wup, a0, aup, gup,
     kk_w, ka_w, rk_w) = refs[:22]
    pos = 22
    if has_vmix:
        vfirst, v0, vdown, vup = refs[pos:pos + 4]
        pos += 4
    r_o, v_o, kk_o, kd_o, lw_o, ba_o, g_o, bon_o = refs[pos:pos + 8]

    i = pl.program_id(0)
    last = pl.num_programs(0) - 1
    keep_p = jnp.where(i == 0, 0.0, 1.0)
    keep_n = jnp.where(i == last, 0.0, 1.0)

    def shifted(z_ref, p_ref, n_ref, lo, hi):
        p_row = p_ref[...].astype(F32)[HALO - 1:HALO, :] * keep_p
        n_row = n_ref[...].astype(F32)[0:1, :] * keep_n
        return _shift(z_ref[...].astype(F32), p_row, n_row, mup[:, lo:hi], mun[:, lo:hi])

    r = shifted(zr, pr, nr, 0, 1024)
    k = shifted(zk, pk, nk, 1024, 2048)
    v = shifted(zv, pv, nv, 2048, 3072)
    lora = shifted(zl, pL, nL, 3072, 3584)

    if has_vmix:
        mix = _sigmoid(v0[...] + _dot(_dot(v, vdown[...]), vup[...]))
        v = v + (vfirst[...].astype(F32) - v) * mix

    u = w0[...] + _dot(jnp.tanh(lora[:, 0:128]), wup[...])
    lw = -EXP_NEG_HALF * _sigmoid(u)
    a = _sigmoid(a0[...] + _dot(lora[:, 128:256], aup[...]))
    g = _dot(_sigmoid(lora[:, 256:512]), gup[...])

    ones = _head_ones()
    kk = k * kk_w[...]
    norm = jnp.sqrt(_head_sum(kk * kk, ones))
    kk = kk / jnp.maximum(norm, 1e-12)
    ka = ka_w[...]
    a_f = a[:, :RW_WIDTH]
    a_b = a[:, RW_WIDTH:]
    kd_f = k * (1.0 + (a_f - 1.0) * ka)
    kd_b = k * (1.0 + (a_b - 1.0) * ka)
    bonus = _head_sum(r * rk_w[...] * (kd_f + kd_b), ones) * v

    r_o[...] = r.astype(r_o.dtype)
    v_o[...] = v.astype(v_o.dtype)
    kk_o[...] = kk.astype(kk_o.dtype)
    kd_o[0] = kd_f.astype(kd_o.dtype)
    kd_o[1] = kd_b.astype(kd_o.dtype)
    lw_o[0] = lw[:, :RW_WIDTH]
    lw_o[1] = lw[:, RW_WIDTH:]
    ba_o[0] = (kk * a_f).astype(ba_o.dtype)
    ba_o[1] = (kk * a_b).astype(ba_o.dtype)
    g_o[...] = g.astype(g_o.dtype)
    bon_o[...] = bonus.astype(bon_o.dtype)


def _rwprep(z, mup, mun, w0, wup, a0, aup, gup, kk_w, ka_w, rk_w, vmix, *, tm):
    s = z.shape[0]
    nbh = s // HALO
    tb = tm // HALO
    has_vmix = vmix is not None

    def main(width, cb):
        return pl.BlockSpec((tm, width), lambda i: (i, cb))

    def prev(width, cb):
        return pl.BlockSpec((HALO, width), lambda i: (jnp.maximum(i * tb - 1, 0), cb))

    def nxt(width, cb):
        return pl.BlockSpec((HALO, width), lambda i: (jnp.minimum((i + 1) * tb, nbh - 1), cb))

    def full(arr):
        nd = arr.ndim
        return pl.BlockSpec(arr.shape, lambda i: (0,) * nd)

    cols = [(1024, COL_R // 1024), (1024, COL_K // 1024), (1024, COL_V // 1024), (512, COL_LORA // 512)]
    in_specs = ([main(w, c) for w, c in cols] + [prev(w, c) for w, c in cols]
                + [nxt(w, c) for w, c in cols])
    args = [z] * 12
    consts = [mup, mun, w0, wup, a0, aup, gup, kk_w, ka_w, rk_w]
    in_specs += [full(c) for c in consts]
    args += consts
    if has_vmix:
        vfirst, v0, vdown, vup = vmix
        in_specs += [pl.BlockSpec((tm, RW_WIDTH), lambda i: (i, 0)), full(v0), full(vdown), full(vup)]
        args += [vfirst, v0, vdown, vup]

    one = jax.ShapeDtypeStruct((s, RW_WIDTH), F32)
    two = jax.ShapeDtypeStruct((2, s, RW_WIDTH), F32)
    spec1 = pl.BlockSpec((tm, RW_WIDTH), lambda i: (i, 0))
    spec2 = pl.BlockSpec((2, tm, RW_WIDTH), lambda i: (0, i, 0))
    return pl.pallas_call(
        functools.partial(_rwprep_kernel, has_vmix),
        out_shape=[one, one, one, two, two, two, one, one],
        grid=(s // tm,),
        in_specs=in_specs,
        out_specs=[spec1, spec1, spec1, spec2, spec2, spec2, spec1, spec1],
        compiler_params=_cparams(("parallel",)),
        name="rwkv_prep",
    )(*args)


def _rwrec_kernel(rf_ref, vf_ref, kkf_ref, kdf_ref, lwf_ref, baf_ref,
                  rb_ref, vb_ref, kkb_ref, kdb_ref, lwb_ref, bab_ref, yf_ref, yb_ref, s_ref):
    T = CHUNK
    T2 = 2 * T
    n_slab = RW_WIDTH // LANES
    n_ck = rf_ref.shape[0] // T

    @pl.when(pl.program_id(0) == 0)
    def _():
        s_ref[...] = jnp.zeros_like(s_ref)

    row = lax.broadcasted_iota(jnp.int32, (T, T), 0)
    col = lax.broadcasted_iota(jnp.int32, (T, T), 1)
    R2 = lax.broadcasted_iota(jnp.int32, (T2, T2), 0)
    C2 = lax.broadcasted_iota(jnp.int32, (T2, T2), 1)
    same = (R2 // T) == (C2 // T)
    tdiff = (R2 % T) - (C2 % T)
    mask_d = same & (((R2 % T) // SUB) == ((C2 % T) // SUB))
    eye = R2 == C2
    head0 = lax.broadcasted_iota(jnp.int32, (T, LANES), 1) < RW_HEAD

    def stack_heads(x):
        return jnp.concatenate([jnp.where(head0, x, 0.0), jnp.where(head0, 0.0, x)], axis=0)

    def stack_dup(x):
        return jnp.concatenate([x, x], axis=0)

    cols = [slice(sl * LANES, (sl + 1) * LANES) for sl in range(n_slab)]
    ar2, bk2, v2, bke2, mask_a, mask_y, e_tot, u_rows = [], [], [], [], [], [], [], []
    dirs = ((1, (rf_ref, vf_ref, kkf_ref, kdf_ref, lwf_ref, baf_ref)),
            (-1, (rb_ref, vb_ref, kkb_ref, kdb_ref, lwb_ref, bab_ref)))
    for ck, (sgn, (r_ref, v_ref, kk_ref, kd_ref, lw_ref, ba_ref)) in (
            (ck, d) for ck in range(n_ck) for d in dirs):
        c_in_blk = ck if sgn > 0 else n_ck - 1 - ck
        rows = slice(c_in_blk * T, (c_in_blk + 1) * T)
        tri = jnp.where((row - col) * sgn >= 0, 1.0, 0.0).astype(BF16)
        lw = lw_ref[rows, :]
        hi, mid, lo = _split3(lw)
        cin = (jnp.dot(tri, hi, preferred_element_type=F32)
               + jnp.dot(tri, mid, preferred_element_type=F32)
               + jnp.dot(tri, lo, preferred_element_type=F32))
        ctot = jnp.sum(lw, axis=0, keepdims=True)
        e_in = jnp.exp(cin)
        e_ex = jnp.exp(cin - lw)
        e_neg = jnp.exp(-cin)
        e_end = jnp.exp(ctot - cin)
        etot = jnp.exp(ctot)
        kd = kd_ref[rows, :].astype(F32)
        ba = ba_ref[rows, :].astype(F32)
        r_t = r_ref[rows, :].astype(F32) * e_in
        a_t = -kk_ref[rows, :].astype(F32) * e_ex
        b_t = ba * e_neg
        k_t = kd * e_neg
        b_e = ba * e_end
        k_e = kd * e_end
        v = v_ref[rows, :].astype(F32)
        m_a = same & (tdiff * sgn > 0)
        m_y = same & (tdiff * sgn >= 0)
        for cs in cols:
            ar2.append(jnp.concatenate([stack_heads(a_t[:, cs]), stack_heads(r_t[:, cs])],
                                       axis=0).astype(BF16))
            bk2.append(jnp.concatenate([stack_dup(b_t[:, cs]), stack_dup(k_t[:, cs])],
                                       axis=0).astype(BF16))
            v2.append(stack_heads(v[:, cs]).astype(BF16))
            bke2.append(jnp.concatenate([stack_heads(b_e[:, cs]), stack_heads(k_e[:, cs])],
                                        axis=0).astype(BF16))
            mask_a.append(m_a)
            mask_y.append(m_y)
            e_tot.append(etot[:, cs])
            u_rows.append(rows)

    n_unit = 2 * n_slab
    slabs = range(n_ck * n_unit)
    sc = [_dot_nt(ar2[s], bk2[s]) for s in slabs]
    n_ab = [jnp.where(mask_a[s], sc[s][:T2, :T2], 0.0) for s in slabs]
    n_ak = [jnp.where(mask_a[s], sc[s][:T2, T2:], 0.0) for s in slabs]
    m_rbk = [jnp.concatenate([jnp.where(mask_y[s], sc[s][T2:, :T2], 0.0),
                              jnp.where(mask_y[s], sc[s][T2:, T2:], 0.0)], axis=1).astype(BF16)
             for s in slabs]

    def dot2(lhs, a, b):
        t = _dot(lhs, jnp.concatenate([a, b], axis=1))
        return t[:, :T2], t[:, T2:]

    p = [jnp.where(mask_d, n_ab[s], 0.0) for s in slabs]
    q = [jnp.where(eye, 1.0, p[s]) for s in slabs]
    p = [_dot(p[s], p[s]) for s in slabs]
    for _ in range(2):
        pq = [dot2(p[s], p[s], q[s]) for s in slabs]
        p = [pq[s][0] for s in slabs]
        q = [q[s] + pq[s][1] for s in slabs]
    q = [q[s] + _dot(p[s], q[s]) for s in slabs]
    akv = [_dot(n_ak[s], v2[s]) for s in slabs]

    for ck in range(n_ck):
        us = range(ck * n_unit, (ck + 1) * n_unit)
        sidx = {u: ((u % n_unit) // n_slab, u % n_slab) for u in us}
        st = {u: s_ref[sidx[u]] for u in us}
        ars = {u: _dot_nt(ar2[u], st[u]) for u in us}
        mx = {u: dot2(q[u], jnp.where(mask_d, 0.0, n_ab[u]), ars[u][:T2] + akv[u]) for u in us}
        m = {u: mx[u][0] for u in us}
        x = {u: mx[u][1] for u in us}
        mx = {u: dot2(m[u], m[u], x[u]) for u in us}
        x = {u: x[u] + mx[u][1] for u in us}
        x = {u: x[u] + _dot(mx[u][0], x[u]) for u in us}
        for u in us:
            y2 = ars[u][T2:] + _dot(m_rbk[u], jnp.concatenate([x[u].astype(BF16), v2[u]], axis=0))
            y_ref = yf_ref if sidx[u][0] == 0 else yb_ref
            y_ref[u_rows[u], cols[sidx[u][1]]] = y2[:T] + y2[T:]
        for u in us:
            upd = _dot(jnp.concatenate([x[u].T, v2[u].astype(F32).T], axis=1), bke2[u])
            s_ref[sidx[u]] = st[u] * e_tot[u] + upd


def _rwrec(r, v, kk, kd, lw, ba, *, n_ck):
    s = r.shape[0]
    rows = n_ck * CHUNK
    nc = s // rows
    fwd1 = pl.BlockSpec((rows, RW_WIDTH), lambda c: (c, 0))
    bwd1 = pl.BlockSpec((rows, RW_WIDTH), lambda c: (nc - 1 - c, 0))
    fwd2 = pl.BlockSpec((None, rows, RW_WIDTH), lambda c: (0, c, 0))
    bwd2 = pl.BlockSpec((None, rows, RW_WIDTH), lambda c: (1, nc - 1 - c, 0))
    out = jax.ShapeDtypeStruct((s, RW_WIDTH), F32)
    return pl.pallas_call(
        _rwrec_kernel,
        out_shape=[out, out],
        grid=(nc,),
        in_specs=[fwd1, fwd1, fwd1, fwd2, fwd2, fwd2, bwd1, bwd1, bwd1, bwd2, bwd2, bwd2],
        out_specs=[fwd1, bwd1],
        scratch_shapes=[pltpu.VMEM((2, RW_WIDTH // LANES, LANES, LANES), F32)],
        compiler_params=_cparams(("arbitrary",)),
        name="rwkv_recurrence",
    )(r, v, kk, kd, lw, ba, r, v, kk, kd, lw, ba)


def _atprep_kernel(zq, zkv, cos_ref, sin_ref, qn, kn, qt_o, k_o, vt_o):
    ones = _head_ones()
    cos = cos_ref[...]
    sin = sin_ref[...]
    lane = lax.broadcasted_iota(jnp.int32, cos.shape, 1)
    even = ((lane // 16) % 2) == 0

    def norm_rope(t, gain):
        ms = _head_sum(t * t, ones) * (1.0 / AT_HEAD)
        tn = t * lax.rsqrt(ms + EPS) * gain
        swapped = jnp.where(even, pltpu.roll(tn, LANES - 16, 1), pltpu.roll(tn, 16, 1))
        return tn * cos + swapped * sin

    q_scale = (AT_HEAD ** -0.5) * LOG2E
    tm = zq.shape[0]
    ones_rows = jnp.where(lax.broadcasted_iota(jnp.int32, (V_AUG - AT_HEAD, tm), 0) == 0, 1.0, 0.0).astype(BF16)
    for sl in range(AT_Q // LANES):
        out = norm_rope(zq[:, sl * LANES:(sl + 1) * LANES].astype(F32), qn[...]) * q_scale
        out_t = out.T.astype(BF16)
        tq = qt_o.shape[3]
        for qi in range(qt_o.shape[1]):
            qt_o[2 * sl, qi] = out_t[:AT_HEAD, qi * tq:(qi + 1) * tq]
            qt_o[2 * sl + 1, qi] = out_t[AT_HEAD:, qi * tq:(qi + 1) * tq]
    for sl in range(AT_KV // LANES):
        out = norm_rope(zkv[:, sl * LANES:(sl + 1) * LANES].astype(F32), kn[...])
        k_o[2 * sl] = out[:, :AT_HEAD].astype(BF16)
        k_o[2 * sl + 1] = out[:, AT_HEAD:].astype(BF16)
        vt = zkv[:, AT_KV + sl * LANES:AT_KV + (sl + 1) * LANES].astype(F32).T.astype(BF16)
        vt_o[2 * sl] = jnp.concatenate([vt[:AT_HEAD], ones_rows], axis=0)
        vt_o[2 * sl + 1] = jnp.concatenate([vt[AT_HEAD:], ones_rows], axis=0)


def _atprep(z, cos, sin, qn, kn, *, tm, tq):
    s = z.shape[0]
    nb = s // tm
    return pl.pallas_call(
        _atprep_kernel,
        out_shape=[jax.ShapeDtypeStruct((AT_HEADS, s // tq, AT_HEAD, tq), BF16),
                   jax.ShapeDtypeStruct((AT_KV_HEADS, nb, tm, AT_HEAD), BF16),
                   jax.ShapeDtypeStruct((AT_KV_HEADS, nb, V_AUG, tm), BF16)],
        grid=(nb,),
        in_specs=[pl.BlockSpec((tm, AT_Q), lambda i: (i, COL_Q // AT_Q)),
                  pl.BlockSpec((tm, 2 * AT_KV), lambda i: (i, COL_ATKV // (2 * AT_KV))),
                  pl.BlockSpec((tm, LANES), lambda i: (i, 0)),
                  pl.BlockSpec((tm, LANES), lambda i: (i, 0)),
                  pl.BlockSpec((1, LANES), lambda i: (0, 0)),
                  pl.BlockSpec((1, LANES), lambda i: (0, 0))],
        out_specs=[pl.BlockSpec((AT_HEADS, tm // tq, AT_HEAD, tq), lambda i: (0, i, 0, 0)),
                   pl.BlockSpec((AT_KV_HEADS, None, tm, AT_HEAD), lambda i: (0, i, 0, 0)),
                   pl.BlockSpec((AT_KV_HEADS, None, V_AUG, tm), lambda i: (0, i, 0, 0))],
        compiler_params=_cparams(("parallel",)),
        name="attn_prep",
    )(z, z, cos, sin, qn, kn)


def _flash_kernel(qt_ref, k_ref, vt_ref, o_ref, m_ref, acc_ref, sa_ref, sb_ref):
    grp, n_tile, _, tq = qt_ref.shape
    nkb = k_ref.shape[0]
    m_ref[...] = jnp.full_like(m_ref, -jnp.inf)
    acc_ref[...] = jnp.zeros_like(acc_ref)

    def step(t_cur, kb_cur, cur_ref, t_next, kb_next, next_ref):
        k_next = k_ref[kb_next]
        vt = vt_ref[kb_cur]
        for g in range(grp):
            next_ref[g] = jnp.dot(k_next, qt_ref[g, t_next], preferred_element_type=F32)
            st = cur_ref[g]
            m_prev = m_ref[t_cur, g]
            m_new = jnp.maximum(m_prev, jnp.max(st, axis=0, keepdims=True))
            alpha = jnp.exp2(m_prev - m_new)
            pt = jnp.exp2(st - m_new).astype(BF16)
            m_ref[t_cur, g] = m_new
            acc_ref[t_cur, g] = alpha * acc_ref[t_cur, g] + jnp.dot(vt, pt, preferred_element_type=F32)

    k0 = k_ref[0]
    for g in range(grp):
        sa_ref[g] = jnp.dot(k0, qt_ref[g, 0], preferred_element_type=F32)

    unroll = max(u for u in (8, 4, 2) if nkb % u == 0)
    trips_per_tile = nkb // unroll

    def body(j, carry):
        t = j // trips_per_tile
        kb = unroll * (j % trips_per_tile)
        for u in range(0, unroll - 2, 2):
            step(t, kb + u, sa_ref, t, kb + u + 1, sb_ref)
            step(t, kb + u + 1, sb_ref, t, kb + u + 2, sa_ref)
        u = unroll - 2
        step(t, kb + u, sa_ref, t, kb + u + 1, sb_ref)
        wrap = (kb + unroll) // nkb
        step(t, kb + u + 1, sb_ref, jnp.minimum(t + wrap, n_tile - 1), (kb + unroll) * (1 - wrap), sa_ref)
        return carry

    lax.fori_loop(0, n_tile * trips_per_tile, body, 0)
    for t in range(n_tile):
        for pair in range(grp // 2):
            a0 = acc_ref[t, 2 * pair]
            a1 = acc_ref[t, 2 * pair + 1]
            o2 = jnp.concatenate([a0[:AT_HEAD] / a0[AT_HEAD:AT_HEAD + 1],
                                  a1[:AT_HEAD] / a1[AT_HEAD:AT_HEAD + 1]], axis=0)
            o_ref[t * tq:(t + 1) * tq, pair * LANES:(pair + 1) * LANES] = o2.T.astype(o_ref.dtype)


def _flash(qt, k, vt, *, n_tile):
    _, nqt, _, tq = qt.shape
    _, nkb, tk, _ = k.shape
    grp = AT_HEADS // AT_KV_HEADS
    return pl.pallas_call(
        _flash_kernel,
        out_shape=jax.ShapeDtypeStruct((nqt * tq, AT_Q), BF16),
        grid=(AT_KV_HEADS, nqt // n_tile),
        in_specs=[pl.BlockSpec((grp, n_tile, AT_HEAD, tq), lambda h, i: (h, i, 0, 0)),
                  pl.BlockSpec((None, nkb, tk, AT_HEAD), lambda h, i: (h, 0, 0, 0)),
                  pl.BlockSpec((None, nkb, V_AUG, tk), lambda h, i: (h, 0, 0, 0))],
        out_specs=pl.BlockSpec((n_tile * tq, grp * AT_HEAD), lambda h, i: (i, h)),
        scratch_shapes=[pltpu.VMEM((n_tile, grp, 1, tq), F32), pltpu.VMEM((n_tile, grp, V_AUG, tq), F32),
                        pltpu.VMEM((grp, tk, tq), F32), pltpu.VMEM((grp, tk, tq), F32)],
        compiler_params=_cparams(("parallel", "parallel")),
        name="flash_attention",
    )(qt, k, vt)


def _mix_out_kernel(yf_ref, yb_ref, g_ref, bon_ref, yat_ref, zg1_ref, zg2_ref, x_ref, lnw, lnb,
                    wrw, wat, wo, npost, o_ref):
    ones = _head_ones()
    y = yf_ref[...] + yb_ref[...]
    inv = 1.0 / RW_HEAD
    mu = _head_sum(y, ones) * inv
    dlt = y - mu
    var = _head_sum(dlt * dlt, ones) * inv
    yn = dlt * lax.rsqrt(var + RW_GN_EPS) * lnw[...] + lnb[...]
    y_rw = (yn + bon_ref[...].astype(F32)) * g_ref[...].astype(F32)
    pa = _dot(y_rw, wrw[...])
    pb = jnp.dot(yat_ref[...], wat[...], preferred_element_type=F32)
    merged = _sigmoid(zg1_ref[...].astype(F32)) * pa + _sigmoid(zg2_ref[...].astype(F32)) * pb
    out = _dot(merged, wo[...])
    o_ref[...] = x_ref[...] + _rms(out, npost[...])


def _layer_block(arr, layer, **kw):
    return pl.BlockSpec((None,) + arr.shape[1:], lambda *_: (layer, 0, 0), **kw)


def _mix_out(yf, yb, g, bon, yat, z, x, lnw, lnb, wrw, wat, wo, npost, layer, *, tm):
    s = x.shape[0]

    def const(arr):
        if arr.ndim == 3:
            return _layer_block(arr, layer, pipeline_mode=pl.Buffered(1))
        return pl.BlockSpec(arr.shape, lambda i: (0, 0), pipeline_mode=pl.Buffered(1))

    row1k = pl.BlockSpec((tm, RW_WIDTH), lambda i: (i, 0))
    return pl.pallas_call(
        _mix_out_kernel,
        out_shape=jax.ShapeDtypeStruct((s, D_MODEL), F32),
        grid=(s // tm,),
        in_specs=[row1k, row1k, row1k, row1k, row1k,
                  pl.BlockSpec((tm, D_MODEL), lambda i: (i, COL_G1 // D_MODEL)),
                  pl.BlockSpec((tm, D_MODEL), lambda i: (i, COL_G2 // D_MODEL)),
                  pl.BlockSpec((tm, D_MODEL), lambda i: (i, 0)),
                  const(lnw), const(lnb), const(wrw), const(wat), const(wo), const(npost)],
        out_specs=pl.BlockSpec((tm, D_MODEL), lambda i: (i, 0)),
        compiler_params=_cparams(("parallel",)),
        name="mix_out",
    )(yf, yb, g, bon, yat, z, z, x, lnw, lnb, wrw, wat, wo, npost)


def _xattn_kernel(x_ref, kv_ref, npre, wq, wo, npost, o_ref):
    x = x_ref[...]
    h = _rms(x, npre[...])
    q = _dot(h, wq[...]) * (X_HEAD ** -0.5)
    outs = []
    for hd in range(X_HEADS):
        kh = kv_ref[:, hd * X_HEAD:(hd + 1) * X_HEAD]
        vh = kv_ref[:, X_WIDTH + hd * X_HEAD:X_WIDTH + (hd + 1) * X_HEAD]
        sc = _dot_nt(q[:, hd * X_HEAD:(hd + 1) * X_HEAD], kh)
        sc = sc - jnp.max(sc, axis=-1, keepdims=True)
        p = jnp.exp(sc)
        p = p / jnp.sum(p, axis=-1, keepdims=True)
        outs.append(_dot(p, vh))
    o = jnp.concatenate(outs, axis=1)
    c = _dot(o, wo[...])
    o_ref[...] = x + _rms(c, npost[...])


def _xattn(x, kv, npre, wq, wo, npost, layer, *, tm):
    s = x.shape[0]

    def const(arr):
        if arr.ndim == 3:
            return _layer_block(arr, layer)
        return pl.BlockSpec(arr.shape, lambda i: (0, 0))

    return pl.pallas_call(
        _xattn_kernel,
        out_shape=jax.ShapeDtypeStruct((s, D_MODEL), F32),
        grid=(s // tm,),
        in_specs=[pl.BlockSpec((tm, D_MODEL), lambda i: (i, 0)), const(kv), const(npre),
                  const(wq), const(wo), const(npost)],
        out_specs=pl.BlockSpec((tm, D_MODEL), lambda i: (i, 0)),
        compiler_params=_cparams(("parallel",)),
        name="cross_attention",
    )(x, kv, npre, wq, wo, npost)


def _ffn_kernel(x_ref, npre, wg, wu, wd, npost, o_ref, h_ref):
    j = pl.program_id(1)

    @pl.when(j == 0)
    def _():
        h_ref[...] = _rms(x_ref[...], npre[...]).astype(BF16)
        o_ref[...] = jnp.zeros_like(o_ref)

    h = h_ref[...]
    gt = jnp.dot(h, wg[...], preferred_element_type=F32)
    ut = jnp.dot(h, wu[...], preferred_element_type=F32)
    act = gt * _sigmoid(gt) * ut
    o_ref[...] += jnp.dot(act.astype(BF16), wd[...], preferred_element_type=F32)

    @pl.when(j == pl.num_programs(1) - 1)
    def _():
        o_ref[...] = x_ref[...] + _rms(o_ref[...], npost[...])


def _ffn(x, npre, wg, wu, wd, npost, layer, *, tm, tf):
    s = x.shape[0]
    f = wg.shape[2]
    return pl.pallas_call(
        _ffn_kernel,
        out_shape=jax.ShapeDtypeStruct((s, D_MODEL), F32),
        grid=(s // tm, f // tf),
        in_specs=[pl.BlockSpec((tm, D_MODEL), lambda i, j: (i, 0)),
                  pl.BlockSpec((1, D_MODEL), lambda i, j: (0, 0)),
                  pl.BlockSpec((None, D_MODEL, tf), lambda i, j: (layer, 0, j)),
                  pl.BlockSpec((None, D_MODEL, tf), lambda i, j: (layer, 0, j)),
                  pl.BlockSpec((None, tf, D_MODEL), lambda i, j: (layer, j, 0)),
                  pl.BlockSpec((1, D_MODEL), lambda i, j: (0, 0))],
        out_specs=pl.BlockSpec((tm, D_MODEL), lambda i, j: (i, 0)),
        scratch_shapes=[pltpu.VMEM((tm, D_MODEL), BF16)],
        compiler_params=_cparams(("parallel", "arbitrary")),
        name="swiglu",
    )(x, npre, wg, wu, wd, npost)


def _pack_w_in_kernel(w_ref, o_ref):
    o_lora = 3 * RW_WIDTH
    o_at = o_lora + 2 * W_LORA + 2 * A_LORA + G_LORA
    o_gate = o_at + AT_Q + 2 * AT_KV
    n_lora = o_at - o_lora
    cols = o_ref.shape[1]
    o_ref[COL_G1:COL_G1 + 2 * D_MODEL, :] = w_ref[o_gate:o_gate + 2 * D_MODEL, :].astype(BF16)
    o_ref[COL_R:COL_R + 3 * RW_WIDTH, :] = w_ref[0:3 * RW_WIDTH, :].astype(BF16)
    o_ref[COL_Q:COL_Q + AT_Q, :] = w_ref[o_at:o_at + AT_Q, :].astype(BF16)
    o_ref[COL_LORA:COL_LORA + n_lora, :] = w_ref[o_lora:o_at, :].astype(BF16)
    o_ref[COL_LORA + n_lora:COL_ATKV, :] = jnp.zeros((COL_ATKV - COL_LORA - n_lora, cols), BF16)
    o_ref[COL_ATKV:IN_PACKED, :] = w_ref[o_at + AT_Q:o_gate, :].astype(BF16)


def _pack_w_in(w_t, *, tc):
    nl, n_in, d = w_t.shape
    return pl.pallas_call(
        _pack_w_in_kernel,
        out_shape=jax.ShapeDtypeStruct((nl, IN_PACKED, d), BF16),
        grid=(nl, d // tc),
        in_specs=[pl.BlockSpec((None, n_in, tc), lambda l, i: (l, 0, i))],
        out_specs=pl.BlockSpec((None, IN_PACKED, tc), lambda l, i: (l, 0, i)),
        compiler_params=_cparams(("parallel", "parallel")),
        name="pack_w_in",
    )(w_t)


def _pack_mu(mu):
    return jnp.concatenate([mu, jnp.zeros((G_PAD - G_LORA,), mu.dtype)])[None, :]


def _block_diag_up(up):
    z = jnp.zeros_like(up[0])
    return jnp.concatenate([jnp.concatenate([up[0], z], axis=1),
                            jnp.concatenate([z, up[1]], axis=1)], axis=0).astype(BF16)


def _rope_tables(s):
    rows = s // GRID_W
    row = jnp.repeat(jnp.arange(rows), GRID_W).astype(F32)
    col = jnp.tile(jnp.arange(GRID_W), rows).astype(F32)
    n_freq = AT_HEAD // 4
    inv = ROPE_THETA ** (-jnp.arange(n_freq, dtype=F32) / n_freq)
    ar = row[:, None] * inv
    ac = col[:, None] * inv
    cos = jnp.concatenate([jnp.cos(ar), jnp.cos(ar), jnp.cos(ac), jnp.cos(ac)], axis=1)
    sin = jnp.concatenate([-jnp.sin(ar), jnp.sin(ar), -jnp.sin(ac), jnp.sin(ac)], axis=1)
    return jnp.tile(cos, (1, 2)), jnp.tile(sin, (1, 2))


def kernel(x, mem, n_mix_pre, n_mix_post, n_x_pre, n_x_post, n_ffn_pre, n_ffn_post, n_mem, w_in, rw_mu_prev, rw_mu_next, rw_w0, rw_w_up, rw_a0, rw_a_up, rw_g_up, rw_v0, rw_v_down, rw_v_up, rw_k_k, rw_k_a, rw_r_k, rw_ln_w, rw_ln_b, w_rw_out, at_q_norm, at_k_norm, w_at_out, w_o, x_wq, x_wkv, x_wo, ffn_wg, ffn_wu, ffn_wd):
    b, s, d = x.shape
    assert b == 1 and d == D_MODEL and s % 1024 == 0
    xs = x[0]
    mems = mem[0]
    cos, sin = _rope_tables(s)
    row = lambda t: t[None, :]
    w_in_p = _pack_w_in(jnp.swapaxes(w_in, 1, 2), tc=256)
    wrw_b, wat_b, wo_b = w_rw_out.astype(BF16), w_at_out.astype(BF16), w_o.astype(BF16)
    xwq_b, xwkv_b, xwo_b = x_wq.astype(BF16), x_wkv.astype(BF16), x_wo.astype(BF16)
    wg_b, wu_b, wd_b = ffn_wg.astype(BF16), ffn_wu.astype(BF16), ffn_wd.astype(BF16)
    v_first = None
    for l in range(DEPTH):
        z = _norm_proj(xs, row(n_mix_pre[l]), w_in_p, l, tm=1024, tn=1536, out_dtype=F32, w_is_nk=True)

        vmix = None
        if l > 0:
            vmix = (v_first, row(rw_v0[l - 1]), rw_v_down[l - 1].astype(BF16), rw_v_up[l - 1].astype(BF16))
        gup = jnp.concatenate([rw_g_up[l], jnp.zeros((G_PAD - G_LORA, RW_WIDTH), F32)], axis=0).astype(BF16)
        r, v, kk, kd, lw, ba, g, bon = _rwprep(
            z, _pack_mu(rw_mu_prev[l]), _pack_mu(rw_mu_next[l]),
            rw_w0[l].reshape(1, 2 * RW_WIDTH), _block_diag_up(rw_w_up[l]),
            rw_a0[l].reshape(1, 2 * RW_WIDTH), _block_diag_up(rw_a_up[l]), gup,
            row(rw_k_k[l]), row(rw_k_a[l]), rw_r_k[l].reshape(1, RW_WIDTH), vmix, tm=256)
        if l == 0:
            v_first = v
        y_f, y_b = _rwrec(r, v, kk, kd, lw, ba, n_ck=2)

        qn = jnp.tile(at_q_norm[l], 2)[None, :]
        kn = jnp.tile(at_k_norm[l], 2)[None, :]
        tq = 256
        q_t, k_blk, v_t = _atprep(z, cos, sin, qn, kn, tm=512, tq=tq)
        y_at = _flash(q_t, k_blk, v_t, n_tile=min(8, s // tq))

        xs = _mix_out(y_f, y_b, g, bon, y_at, z, xs, row(rw_ln_w[l]), row(rw_ln_b[l]),
                      wrw_b, wat_b, wo_b, row(n_mix_post[l]), l, tm=256)

        kv = _norm_proj(mems, row(n_mem[l]), xwkv_b, l, tm=mems.shape[0], tn=512, out_dtype=BF16)
        xs = _xattn(xs, kv, row(n_x_pre[l]), xwq_b, xwo_b, row(n_x_post[l]), l, tm=512)

        xs = _ffn(xs, row(n_ffn_pre[l]), wg_b, wu_b, wd_b, row(n_ffn_post[l]), l, tm=512, tf=512)
    return xs[None]
```

```python
import functools

import jax
import jax.numpy as jnp
from jax import lax
from jax.experimental import pallas as pl
from jax.experimental.pallas import tpu as pltpu

F32 = jnp.float32
BF16 = jnp.bfloat16

D_MODEL = 2048
DEPTH = 4
GRID_W = 64
EPS = 1e-6
RW_HEAD = 64
RW_WIDTH = 1024
W_LORA = 64
A_LORA = 64
G_LORA = 160
RW_GN_EPS = 64e-5
AT_HEADS = 16
AT_KV_HEADS = 4
AT_HEAD = 64
AT_Q = AT_HEADS * AT_HEAD
AT_KV = AT_KV_HEADS * AT_HEAD
ROPE_THETA = 10000.0
X_HEADS = 4
X_HEAD = 128
X_WIDTH = X_HEADS * X_HEAD

LANES = 128
CHUNK = 64
SUB = 16
LOG2E = 1.4426950408889634
EXP_NEG_HALF = 0.6065306597126334
V_AUG = AT_HEAD + 16
HALO = 16
G_PAD = 256

COL_G1 = 0
COL_G2 = 2048
COL_R = 4096
COL_K = 5120
COL_V = 6144
COL_Q = 7168
COL_LORA = 8192
COL_ATKV = 8704
IN_PACKED = 9216

VMEM_LIMIT = 56 * 1024 * 1024

TILE_IN_PROJ = (1024, 1536)
TILE_PACK_COLS = 256
TILE_RWKV_PREP = 256
REC_CHUNKS_PER_STEP = 2
TILE_ATTN_PREP = 512
FLASH_TQ = 256
FLASH_TILES_PER_STEP = 8
TILE_MIX_OUT = 256
TILE_KV_PROJ_COLS = 512
TILE_XATTN = 512
TILE_FFN = (512, 512)


def _cparams(sem):
    return pltpu.CompilerParams(dimension_semantics=sem, vmem_limit_bytes=VMEM_LIMIT)


def _dot(a, b):
    return jnp.dot(a.astype(BF16), b.astype(BF16), preferred_element_type=F32)


def _dot_nt(a, b):
    return lax.dot_general(a.astype(BF16), b.astype(BF16), (((1,), (1,)), ((), ())),
                           preferred_element_type=F32)


def _split2(x):
    hi = x.astype(BF16)
    lo = (x - hi.astype(F32)).astype(BF16)
    return hi, lo


def _split3(x):
    hi = x.astype(BF16)
    r1 = x - hi.astype(F32)
    mid = r1.astype(BF16)
    lo = (r1 - mid.astype(F32)).astype(BF16)
    return hi, mid, lo


def _sigmoid(x):
    return 1.0 / (1.0 + jnp.exp(-x))


def _head_ones():
    r = lax.broadcasted_iota(jnp.int32, (LANES, LANES), 0) // RW_HEAD
    c = lax.broadcasted_iota(jnp.int32, (LANES, LANES), 1) // RW_HEAD
    return jnp.where(r == c, 1.0, 0.0).astype(BF16)


def _head_sum(x, ones):
    outs = []
    for s in range(x.shape[1] // LANES):
        hi, lo = _split2(x[:, s * LANES:(s + 1) * LANES])
        outs.append(jnp.dot(hi, ones, preferred_element_type=F32)
                    + jnp.dot(lo, ones, preferred_element_type=F32))
    return outs[0] if len(outs) == 1 else jnp.concatenate(outs, axis=1)


def _rms(x, gain):
    ms = jnp.mean(x * x, axis=-1, keepdims=True)
    return x * lax.rsqrt(ms + EPS) * gain


def _norm_proj_kernel(w_is_nk, x_ref, g_ref, w_ref, o_ref, h_ref):
    @pl.when(pl.program_id(1) == 0)
    def _():
        h_ref[...] = _rms(x_ref[...], g_ref[...]).astype(BF16)

    if w_is_nk:
        out = lax.dot_general(h_ref[...], w_ref[...], (((1,), (1,)), ((), ())), preferred_element_type=F32)
    else:
        out = jnp.dot(h_ref[...], w_ref[...], preferred_element_type=F32)
    o_ref[...] = out.astype(o_ref.dtype)


def _norm_proj(x, gain, w, layer, *, tm, tn, out_dtype, w_is_nk=False):
    m, d = x.shape
    if w_is_nk:
        n = w.shape[1]
        w_spec = pl.BlockSpec((None, tn, d), lambda i, j: (layer, j, 0))
    else:
        n = w.shape[2]
        w_spec = pl.BlockSpec((None, d, tn), lambda i, j: (layer, 0, j))
    return pl.pallas_call(
        functools.partial(_norm_proj_kernel, w_is_nk),
        out_shape=jax.ShapeDtypeStruct((m, n), out_dtype),
        grid=(m // tm, n // tn),
        in_specs=[pl.BlockSpec((tm, d), lambda i, j: (i, 0)),
                  pl.BlockSpec((1, d), lambda i, j: (0, 0)),
                  w_spec],
        out_specs=pl.BlockSpec((tm, tn), lambda i, j: (i, j)),
        scratch_shapes=[pltpu.VMEM((tm, d), BF16)],
        compiler_params=_cparams(("parallel", "arbitrary")),
        name="norm_proj",
    )(x, gain, w)


def _shift(z, prev_row, next_row, mu_p, mu_n):
    rows = z.shape[0]
    ridx = lax.broadcasted_iota(jnp.int32, z.shape, 0)
    zp = jnp.where(ridx == 0, prev_row, pltpu.roll(z, 1, 0))
    zn = jnp.where(ridx == rows - 1, next_row, pltpu.roll(z, rows - 1, 0))
    return (1.0 - mu_p - mu_n) * z + mu_p * zp + mu_n * zn


def _rwprep_kernel(has_vmix, *refs):
    (zr, zk, zv, zl, pr, pk, pv, pL, nr, nk, nv, nL, mup, mun, w0, wup, a0, aup, gup,
     kk_w, ka_w, rk_w) = refs[:22]
    pos = 22
    if has_vmix:
        vfirst, v0, vdown, vup = refs[pos:pos + 4]
        pos += 4
    r_o, v_o, kk_o, kd_o, lw_o, ba_o, g_o, bon_o = refs[pos:pos + 8]

    i = pl.program_id(0)
    last = pl.num_programs(0) - 1
    keep_p = jnp.where(i == 0, 0.0, 1.0)
    keep_n = jnp.where(i == last, 0.0, 1.0)

    def shifted(z_ref, p_ref, n_ref, lo, hi):
        p_row = p_ref[HALO - 1:HALO, :] * keep_p
        n_row = n_ref[0:1, :] * keep_n
        return _shift(z_ref[...], p_row, n_row, mup[:, lo:hi], mun[:, lo:hi])

    r = shifted(zr, pr, nr, 0, 1024)
    k = shifted(zk, pk, nk, 1024, 2048)
    v = shifted(zv, pv, nv, 2048, 3072)
    lora = shifted(zl, pL, nL, 3072, 3584)

    if has_vmix:
        mix = _sigmoid(v0[...] + _dot(_dot(v, vdown[...]), vup[...]))
        v = v + (vfirst[...] - v) * mix

    u = w0[...] + _dot(jnp.tanh(lora[:, 0:128]), wup[...])
    lw = -EXP_NEG_HALF * _sigmoid(u)
    a = _sigmoid(a0[...] + _dot(lora[:, 128:256], aup[...]))
    g = _dot(_sigmoid(lora[:, 256:512]), gup[...])

    ones = _head_ones()
    kk = k * kk_w[...]
    norm = jnp.sqrt(_head_sum(kk * kk, ones))
    kk = kk / jnp.maximum(norm, 1e-12)
    ka = ka_w[...]
    a_f = a[:, :RW_WIDTH]
    a_b = a[:, RW_WIDTH:]
    kd_f = k * (1.0 + (a_f - 1.0) * ka)
    kd_b = k * (1.0 + (a_b - 1.0) * ka)
    bonus = _head_sum(r * rk_w[...] * (kd_f + kd_b), ones) * v

    r_o[...] = r
    v_o[...] = v
    kk_o[...] = kk
    kd_o[0] = kd_f
    kd_o[1] = kd_b
    lw_o[0] = lw[:, :RW_WIDTH]
    lw_o[1] = lw[:, RW_WIDTH:]
    ba_o[0] = kk * a_f
    ba_o[1] = kk * a_b
    g_o[...] = g
    bon_o[...] = bonus


def _rwprep(z, mup, mun, w0, wup, a0, aup, gup, kk_w, ka_w, rk_w, vmix, *, tm):
    s = z.shape[0]
    nbh = s // HALO
    tb = tm // HALO
    has_vmix = vmix is not None

    def main(width, cb):
        return pl.BlockSpec((tm, width), lambda i: (i, cb))

    def prev(width, cb):
        return pl.BlockSpec((HALO, width), lambda i: (jnp.maximum(i * tb - 1, 0), cb))

    def nxt(width, cb):
        return pl.BlockSpec((HALO, width), lambda i: (jnp.minimum((i + 1) * tb, nbh - 1), cb))

    def full(arr):
        nd = arr.ndim
        return pl.BlockSpec(arr.shape, lambda i: (0,) * nd)

    cols = [(1024, COL_R // 1024), (1024, COL_K // 1024), (1024, COL_V // 1024), (512, COL_LORA // 512)]
    in_specs = ([main(w, c) for w, c in cols] + [prev(w, c) for w, c in cols]
                + [nxt(w, c) for w, c in cols])
    args = [z] * 12
    consts = [mup, mun, w0, wup, a0, aup, gup, kk_w, ka_w, rk_w]
    in_specs += [full(c) for c in consts]
    args += consts
    if has_vmix:
        vfirst, v0, vdown, vup = vmix
        in_specs += [pl.BlockSpec((tm, RW_WIDTH), lambda i: (i, 0)), full(v0), full(vdown), full(vup)]
        args += [vfirst, v0, vdown, vup]

    one = jax.ShapeDtypeStruct((s, RW_WIDTH), F32)
    two = jax.ShapeDtypeStruct((2, s, RW_WIDTH), F32)
    spec1 = pl.BlockSpec((tm, RW_WIDTH), lambda i: (i, 0))
    spec2 = pl.BlockSpec((2, tm, RW_WIDTH), lambda i: (0, i, 0))
    return pl.pallas_call(
        functools.partial(_rwprep_kernel, has_vmix),
        out_shape=[one, one, one, two, two, two, one, one],
        grid=(s // tm,),
        in_specs=in_specs,
        out_specs=[spec1, spec1, spec1, spec2, spec2, spec2, spec1, spec1],
        compiler_params=_cparams(("parallel",)),
        name="rwkv_prep",
    )(*args)


def _rwrec_kernel(rf_ref, vf_ref, kkf_ref, kdf_ref, lwf_ref, baf_ref,
                  rb_ref, vb_ref, kkb_ref, kdb_ref, lwb_ref, bab_ref, yf_ref, yb_ref, s_ref):
    T = CHUNK
    T2 = 2 * T
    n_slab = RW_WIDTH // LANES
    n_ck = rf_ref.shape[0] // T

    @pl.when(pl.program_id(0) == 0)
    def _():
        s_ref[...] = jnp.zeros_like(s_ref)

    row = lax.broadcasted_iota(jnp.int32, (T, T), 0)
    col = lax.broadcasted_iota(jnp.int32, (T, T), 1)
    R2 = lax.broadcasted_iota(jnp.int32, (T2, T2), 0)
    C2 = lax.broadcasted_iota(jnp.int32, (T2, T2), 1)
    same = (R2 // T) == (C2 // T)
    tdiff = (R2 % T) - (C2 % T)
    mask_d = same & (((R2 % T) // SUB) == ((C2 % T) // SUB))
    eye = R2 == C2
    head0 = lax.broadcasted_iota(jnp.int32, (T, LANES), 1) < RW_HEAD

    def stack_heads(x):
        return jnp.concatenate([jnp.where(head0, x, 0.0), jnp.where(head0, 0.0, x)], axis=0)

    def stack_dup(x):
        return jnp.concatenate([x, x], axis=0)

    cols = [slice(sl * LANES, (sl + 1) * LANES) for sl in range(n_slab)]
    ar2, bk2, v2, bke2, mask_a, mask_y, e_tot, u_rows = [], [], [], [], [], [], [], []
    dirs = ((1, (rf_ref, vf_ref, kkf_ref, kdf_ref, lwf_ref, baf_ref)),
            (-1, (rb_ref, vb_ref, kkb_ref, kdb_ref, lwb_ref, bab_ref)))
    for ck, (sgn, (r_ref, v_ref, kk_ref, kd_ref, lw_ref, ba_ref)) in (
            (ck, d) for ck in range(n_ck) for d in dirs):
        c_in_blk = ck if sgn > 0 else n_ck - 1 - ck
        rows = slice(c_in_blk * T, (c_in_blk + 1) * T)
        tri = jnp.where((row - col) * sgn >= 0, 1.0, 0.0).astype(BF16)
        lw = lw_ref[rows, :]
        hi, mid, lo = _split3(lw)
        cin = (jnp.dot(tri, hi, preferred_element_type=F32)
               + jnp.dot(tri, mid, preferred_element_type=F32)
               + jnp.dot(tri, lo, preferred_element_type=F32))
        ctot = jnp.sum(lw, axis=0, keepdims=True)
        e_in = jnp.exp(cin)
        e_ex = jnp.exp(cin - lw)
        e_neg = jnp.exp(-cin)
        e_end = jnp.exp(ctot - cin)
        etot = jnp.exp(ctot)
        kd = kd_ref[rows, :]
        ba = ba_ref[rows, :]
        r_t = r_ref[rows, :] * e_in
        a_t = -kk_ref[rows, :] * e_ex
        b_t = ba * e_neg
        k_t = kd * e_neg
        b_e = ba * e_end
        k_e = kd * e_end
        v = v_ref[rows, :]
        m_a = same & (tdiff * sgn > 0)
        m_y = same & (tdiff * sgn >= 0)
        for cs in cols:
            ar2.append(jnp.concatenate([stack_heads(a_t[:, cs]), stack_heads(r_t[:, cs])],
                                       axis=0).astype(BF16))
            bk2.append(jnp.concatenate([stack_dup(b_t[:, cs]), stack_dup(k_t[:, cs])],
                                       axis=0).astype(BF16))
            v2.append(stack_heads(v[:, cs]).astype(BF16))
            bke2.append(jnp.concatenate([stack_heads(b_e[:, cs]), stack_heads(k_e[:, cs])],
                                        axis=0).astype(BF16))
            mask_a.append(m_a)
            mask_y.append(m_y)
            e_tot.append(etot[:, cs])
            u_rows.append(rows)

    n_unit = 2 * n_slab
    slabs = range(n_ck * n_unit)
    sc = [_dot_nt(ar2[s], bk2[s]) for s in slabs]
    n_ab = [jnp.where(mask_a[s], sc[s][:T2, :T2], 0.0) for s in slabs]
    n_ak = [jnp.where(mask_a[s], sc[s][:T2, T2:], 0.0) for s in slabs]
    m_rbk = [jnp.concatenate([jnp.where(mask_y[s], sc[s][T2:, :T2], 0.0),
                              jnp.where(mask_y[s], sc[s][T2:, T2:], 0.0)], axis=1).astype(BF16)
             for s in slabs]

    def dot2(lhs, a, b):
        t = _dot(lhs, jnp.concatenate([a, b], axis=1))
        return t[:, :T2], t[:, T2:]

    p = [jnp.where(mask_d, n_ab[s], 0.0) for s in slabs]
    q = [jnp.where(eye, 1.0, p[s]) for s in slabs]
    p = [_dot(p[s], p[s]) for s in slabs]
    for _ in range(2):
        pq = [dot2(p[s], p[s], q[s]) for s in slabs]
        p = [pq[s][0] for s in slabs]
        q = [q[s] + pq[s][1] for s in slabs]
    q = [q[s] + _dot(p[s], q[s]) for s in slabs]
    akv = [_dot(n_ak[s], v2[s]) for s in slabs]

    for ck in range(n_ck):
        us = range(ck * n_unit, (ck + 1) * n_unit)
        sidx = {u: ((u % n_unit) // n_slab, u % n_slab) for u in us}
        st = {u: s_ref[sidx[u]] for u in us}
        ars = {u: _dot_nt(ar2[u], st[u]) for u in us}
        mx = {u: dot2(q[u], jnp.where(mask_d, 0.0, n_ab[u]), ars[u][:T2] + akv[u]) for u in us}
        m = {u: mx[u][0] for u in us}
        x = {u: mx[u][1] for u in us}
        mx = {u: dot2(m[u], m[u], x[u]) for u in us}
        x = {u: x[u] + mx[u][1] for u in us}
        x = {u: x[u] + _dot(mx[u][0], x[u]) for u in us}
        for u in us:
            y2 = ars[u][T2:] + _dot(m_rbk[u], jnp.concatenate([x[u].astype(BF16), v2[u]], axis=0))
            y_ref = yf_ref if sidx[u][0] == 0 else yb_ref
            y_ref[u_rows[u], cols[sidx[u][1]]] = y2[:T] + y2[T:]
        for u in us:
            upd = _dot(jnp.concatenate([x[u].T, v2[u].astype(F32).T], axis=1), bke2[u])
            s_ref[sidx[u]] = st[u] * e_tot[u] + upd


def _rwrec(r, v, kk, kd, lw, ba, *, n_ck):
    s = r.shape[0]
    rows = n_ck * CHUNK
    nc = s // rows
    fwd1 = pl.BlockSpec((rows, RW_WIDTH), lambda c: (c, 0))
    bwd1 = pl.BlockSpec((rows, RW_WIDTH), lambda c: (nc - 1 - c, 0))
    fwd2 = pl.BlockSpec((None, rows, RW_WIDTH), lambda c: (0, c, 0))
    bwd2 = pl.BlockSpec((None, rows, RW_WIDTH), lambda c: (1, nc - 1 - c, 0))
    out = jax.ShapeDtypeStruct((s, RW_WIDTH), F32)
    return pl.pallas_call(
        _rwrec_kernel,
        out_shape=[out, out],
        grid=(nc,),
        in_specs=[fwd1, fwd1, fwd1, fwd2, fwd2, fwd2, bwd1, bwd1, bwd1, bwd2, bwd2, bwd2],
        out_specs=[fwd1, bwd1],
        scratch_shapes=[pltpu.VMEM((2, RW_WIDTH // LANES, LANES, LANES), F32)],
        compiler_params=_cparams(("arbitrary",)),
        name="rwkv_recurrence",
    )(r, v, kk, kd, lw, ba, r, v, kk, kd, lw, ba)


def _atprep_kernel(zq, zkv, cos_ref, sin_ref, qn, kn, qt_o, k_o, vt_o):
    ones = _head_ones()
    cos = cos_ref[...]
    sin = sin_ref[...]
    lane = lax.broadcasted_iota(jnp.int32, cos.shape, 1)
    even = ((lane // 16) % 2) == 0

    def norm_rope(t, gain):
        ms = _head_sum(t * t, ones) * (1.0 / AT_HEAD)
        tn = t * lax.rsqrt(ms + EPS) * gain
        swapped = jnp.where(even, pltpu.roll(tn, LANES - 16, 1), pltpu.roll(tn, 16, 1))
        return tn * cos + swapped * sin

    q_scale = (AT_HEAD ** -0.5) * LOG2E
    tm = zq.shape[0]
    ones_rows = jnp.where(lax.broadcasted_iota(jnp.int32, (V_AUG - AT_HEAD, tm), 0) == 0, 1.0, 0.0).astype(BF16)
    for sl in range(AT_Q // LANES):
        out = norm_rope(zq[:, sl * LANES:(sl + 1) * LANES], qn[...]) * q_scale
        out_t = out.T.astype(BF16)
        tq = qt_o.shape[3]
        for qi in range(qt_o.shape[1]):
            qt_o[2 * sl, qi] = out_t[:AT_HEAD, qi * tq:(qi + 1) * tq]
            qt_o[2 * sl + 1, qi] = out_t[AT_HEAD:, qi * tq:(qi + 1) * tq]
    for sl in range(AT_KV // LANES):
        out = norm_rope(zkv[:, sl * LANES:(sl + 1) * LANES], kn[...])
        k_o[2 * sl] = out[:, :AT_HEAD].astype(BF16)
        k_o[2 * sl + 1] = out[:, AT_HEAD:].astype(BF16)
        vt = zkv[:, AT_KV + sl * LANES:AT_KV + (sl + 1) * LANES].T.astype(BF16)
        vt_o[2 * sl] = jnp.concatenate([vt[:AT_HEAD], ones_rows], axis=0)
        vt_o[2 * sl + 1] = jnp.concatenate([vt[AT_HEAD:], ones_rows], axis=0)


def _atprep(z, cos, sin, qn, kn, *, tm, tq):
    s = z.shape[0]
    nb = s // tm
    return pl.pallas_call(
        _atprep_kernel,
        out_shape=[jax.ShapeDtypeStruct((AT_HEADS, s // tq, AT_HEAD, tq), BF16),
                   jax.ShapeDtypeStruct((AT_KV_HEADS, nb, tm, AT_HEAD), BF16),
                   jax.ShapeDtypeStruct((AT_KV_HEADS, nb, V_AUG, tm), BF16)],
        grid=(nb,),
        in_specs=[pl.BlockSpec((tm, AT_Q), lambda i: (i, COL_Q // AT_Q)),
                  pl.BlockSpec((tm, 2 * AT_KV), lambda i: (i, COL_ATKV // (2 * AT_KV))),
                  pl.BlockSpec((tm, LANES), lambda i: (i, 0)),
                  pl.BlockSpec((tm, LANES), lambda i: (i, 0)),
                  pl.BlockSpec((1, LANES), lambda i: (0, 0)),
                  pl.BlockSpec((1, LANES), lambda i: (0, 0))],
        out_specs=[pl.BlockSpec((AT_HEADS, tm // tq, AT_HEAD, tq), lambda i: (0, i, 0, 0)),
                   pl.BlockSpec((AT_KV_HEADS, None, tm, AT_HEAD), lambda i: (0, i, 0, 0)),
                   pl.BlockSpec((AT_KV_HEADS, None, V_AUG, tm), lambda i: (0, i, 0, 0))],
        compiler_params=_cparams(("parallel",)),
        name="attn_prep",
    )(z, z, cos, sin, qn, kn)


def _flash_kernel(qt_ref, k_ref, vt_ref, o_ref, m_ref, acc_ref, sa_ref, sb_ref):
    grp, n_tile, _, tq = qt_ref.shape
    nkb = k_ref.shape[0]
    m_ref[...] = jnp.full_like(m_ref, -jnp.inf)
    acc_ref[...] = jnp.zeros_like(acc_ref)

    def step(t_cur, kb_cur, cur_ref, t_next, kb_next, next_ref):
        k_next = k_ref[kb_next]
        vt = vt_ref[kb_cur]
        for g in range(grp):
            next_ref[g] = jnp.dot(k_next, qt_ref[g, t_next], preferred_element_type=F32)
            st = cur_ref[g]
            m_prev = m_ref[t_cur, g]
            m_new = jnp.maximum(m_prev, jnp.max(st, axis=0, keepdims=True))
            alpha = jnp.exp2(m_prev - m_new)
            pt = jnp.exp2(st - m_new).astype(BF16)
            m_ref[t_cur, g] = m_new
            acc_ref[t_cur, g] = alpha * acc_ref[t_cur, g] + jnp.dot(vt, pt, preferred_element_type=F32)

    k0 = k_ref[0]
    for g in range(grp):
        sa_ref[g] = jnp.dot(k0, qt_ref[g, 0], preferred_element_type=F32)

    unroll = max(u for u in (8, 4, 2) if nkb % u == 0)
    trips_per_tile = nkb // unroll

    def body(j, carry):
        t = j // trips_per_tile
        kb = unroll * (j % trips_per_tile)
        for u in range(0, unroll - 2, 2):
            step(t, kb + u, sa_ref, t, kb + u + 1, sb_ref)
            step(t, kb + u + 1, sb_ref, t, kb + u + 2, sa_ref)
        u = unroll - 2
        step(t, kb + u, sa_ref, t, kb + u + 1, sb_ref)
        wrap = (kb + unroll) // nkb
        step(t, kb + u + 1, sb_ref, jnp.minimum(t + wrap, n_tile - 1), (kb + unroll) * (1 - wrap), sa_ref)
        return carry

    lax.fori_loop(0, n_tile * trips_per_tile, body, 0)
    for t in range(n_tile):
        for pair in range(grp // 2):
            a0 = acc_ref[t, 2 * pair]
            a1 = acc_ref[t, 2 * pair + 1]
            o2 = jnp.concatenate([a0[:AT_HEAD] / a0[AT_HEAD:AT_HEAD + 1],
                                  a1[:AT_HEAD] / a1[AT_HEAD:AT_HEAD + 1]], axis=0)
            o_ref[t * tq:(t + 1) * tq, pair * LANES:(pair + 1) * LANES] = o2.T.astype(o_ref.dtype)


def _flash(qt, k, vt, *, n_tile):
    _, nqt, _, tq = qt.shape
    _, nkb, tk, _ = k.shape
    grp = AT_HEADS // AT_KV_HEADS
    return pl.pallas_call(
        _flash_kernel,
        out_shape=jax.ShapeDtypeStruct((nqt * tq, AT_Q), BF16),
        grid=(AT_KV_HEADS, nqt // n_tile),
        in_specs=[pl.BlockSpec((grp, n_tile, AT_HEAD, tq), lambda h, i: (h, i, 0, 0)),
                  pl.BlockSpec((None, nkb, tk, AT_HEAD), lambda h, i: (h, 0, 0, 0)),
                  pl.BlockSpec((None, nkb, V_AUG, tk), lambda h, i: (h, 0, 0, 0))],
        out_specs=pl.BlockSpec((n_tile * tq, grp * AT_HEAD), lambda h, i: (i, h)),
        scratch_shapes=[pltpu.VMEM((n_tile, grp, 1, tq), F32), pltpu.VMEM((n_tile, grp, V_AUG, tq), F32),
                        pltpu.VMEM((grp, tk, tq), F32), pltpu.VMEM((grp, tk, tq), F32)],
        compiler_params=_cparams(("parallel", "parallel")),
        name="flash_attention",
    )(qt, k, vt)


def _mix_out_kernel(yf_ref, yb_ref, g_ref, bon_ref, yat_ref, zg1_ref, zg2_ref, x_ref, lnw, lnb,
                    wrw, wat, wo, npost, o_ref):
    ones = _head_ones()
    y = yf_ref[...] + yb_ref[...]
    inv = 1.0 / RW_HEAD
    mu = _head_sum(y, ones) * inv
    dlt = y - mu
    var = _head_sum(dlt * dlt, ones) * inv
    yn = dlt * lax.rsqrt(var + RW_GN_EPS) * lnw[...] + lnb[...]
    y_rw = (yn + bon_ref[...]) * g_ref[...]
    pa = _dot(y_rw, wrw[...])
    pb = jnp.dot(yat_ref[...], wat[...], preferred_element_type=F32)
    merged = _sigmoid(zg1_ref[...]) * pa + _sigmoid(zg2_ref[...]) * pb
    out = _dot(merged, wo[...])
    o_ref[...] = x_ref[...] + _rms(out, npost[...])


def _layer_block(arr, layer, **kw):
    return pl.BlockSpec((None,) + arr.shape[1:], lambda *_: (layer, 0, 0), **kw)


def _mix_out(yf, yb, g, bon, yat, z, x, lnw, lnb, wrw, wat, wo, npost, layer, *, tm):
    s = x.shape[0]

    def const(arr):
        if arr.ndim == 3:
            return _layer_block(arr, layer, pipeline_mode=pl.Buffered(1))
        return pl.BlockSpec(arr.shape, lambda i: (0, 0), pipeline_mode=pl.Buffered(1))

    row1k = pl.BlockSpec((tm, RW_WIDTH), lambda i: (i, 0))
    return pl.pallas_call(
        _mix_out_kernel,
        out_shape=jax.ShapeDtypeStruct((s, D_MODEL), F32),
        grid=(s // tm,),
        in_specs=[row1k, row1k, row1k, row1k, row1k,
                  pl.BlockSpec((tm, D_MODEL), lambda i: (i, COL_G1 // D_MODEL)),
                  pl.BlockSpec((tm, D_MODEL), lambda i: (i, COL_G2 // D_MODEL)),
                  pl.BlockSpec((tm, D_MODEL), lambda i: (i, 0)),
                  const(lnw), const(lnb), const(wrw), const(wat), const(wo), const(npost)],
        out_specs=pl.BlockSpec((tm, D_MODEL), lambda i: (i, 0)),
        compiler_params=_cparams(("parallel",)),
        name="mix_out",
    )(yf, yb, g, bon, yat, z, z, x, lnw, lnb, wrw, wat, wo, npost)


def _xattn_kernel(x_ref, kv_ref, npre, wq, wo, npost, o_ref):
    x = x_ref[...]
    h = _rms(x, npre[...])
    q = _dot(h, wq[...]) * (X_HEAD ** -0.5)
    outs = []
    for hd in range(X_HEADS):
        kh = kv_ref[:, hd * X_HEAD:(hd + 1) * X_HEAD]
        vh = kv_ref[:, X_WIDTH + hd * X_HEAD:X_WIDTH + (hd + 1) * X_HEAD]
        sc = _dot_nt(q[:, hd * X_HEAD:(hd + 1) * X_HEAD], kh)
        sc = sc - jnp.max(sc, axis=-1, keepdims=True)
        p = jnp.exp(sc)
        p = p / jnp.sum(p, axis=-1, keepdims=True)
        outs.append(_dot(p, vh))
    o = jnp.concatenate(outs, axis=1)
    c = _dot(o, wo[...])
    o_ref[...] = x + _rms(c, npost[...])


def _xattn(x, kv, npre, wq, wo, npost, layer, *, tm):
    s = x.shape[0]

    def const(arr):
        if arr.ndim == 3:
            return _layer_block(arr, layer)
        return pl.BlockSpec(arr.shape, lambda i: (0, 0))

    return pl.pallas_call(
        _xattn_kernel,
        out_shape=jax.ShapeDtypeStruct((s, D_MODEL), F32),
        grid=(s // tm,),
        in_specs=[pl.BlockSpec((tm, D_MODEL), lambda i: (i, 0)), const(kv), const(npre),
                  const(wq), const(wo), const(npost)],
        out_specs=pl.BlockSpec((tm, D_MODEL), lambda i: (i, 0)),
        compiler_params=_cparams(("parallel",)),
        name="cross_attention",
    )(x, kv, npre, wq, wo, npost)


def _ffn_kernel(x_ref, npre, wg, wu, wd, npost, o_ref, h_ref):
    j = pl.program_id(1)

    @pl.when(j == 0)
    def _():
        h_ref[...] = _rms(x_ref[...], npre[...]).astype(BF16)
        o_ref[...] = jnp.zeros_like(o_ref)

    h = h_ref[...]
    gt = jnp.dot(h, wg[...], preferred_element_type=F32)
    ut = jnp.dot(h, wu[...], preferred_element_type=F32)
    act = gt * _sigmoid(gt) * ut
    o_ref[...] += jnp.dot(act.astype(BF16), wd[...], preferred_element_type=F32)

    @pl.when(j == pl.num_programs(1) - 1)
    def _():
        o_ref[...] = x_ref[...] + _rms(o_ref[...], npost[...])


def _ffn(x, npre, wg, wu, wd, npost, layer, *, tm, tf):
    s = x.shape[0]
    f = wg.shape[2]
    return pl.pallas_call(
        _ffn_kernel,
        out_shape=jax.ShapeDtypeStruct((s, D_MODEL), F32),
        grid=(s // tm, f // tf),
        in_specs=[pl.BlockSpec((tm, D_MODEL), lambda i, j: (i, 0)),
                  pl.BlockSpec((1, D_MODEL), lambda i, j: (0, 0)),
                  pl.BlockSpec((None, D_MODEL, tf), lambda i, j: (layer, 0, j)),
                  pl.BlockSpec((None, D_MODEL, tf), lambda i, j: (layer, 0, j)),
                  pl.BlockSpec((None, tf, D_MODEL), lambda i, j: (layer, j, 0)),
                  pl.BlockSpec((1, D_MODEL), lambda i, j: (0, 0))],
        out_specs=pl.BlockSpec((tm, D_MODEL), lambda i, j: (i, 0)),
        scratch_shapes=[pltpu.VMEM((tm, D_MODEL), BF16)],
        compiler_params=_cparams(("parallel", "arbitrary")),
        name="swiglu",
    )(x, npre, wg, wu, wd, npost)


def _pack_w_in_kernel(w_ref, o_ref):
    o_lora = 3 * RW_WIDTH
    o_at = o_lora + 2 * W_LORA + 2 * A_LORA + G_LORA
    o_gate = o_at + AT_Q + 2 * AT_KV
    n_lora = o_at - o_lora
    cols = o_ref.shape[1]
    o_ref[COL_G1:COL_G1 + 2 * D_MODEL, :] = w_ref[o_gate:o_gate + 2 * D_MODEL, :].astype(BF16)
    o_ref[COL_R:COL_R + 3 * RW_WIDTH, :] = w_ref[0:3 * RW_WIDTH, :].astype(BF16)
    o_ref[COL_Q:COL_Q + AT_Q, :] = w_ref[o_at:o_at + AT_Q, :].astype(BF16)
    o_ref[COL_LORA:COL_LORA + n_lora, :] = w_ref[o_lora:o_at, :].astype(BF16)
    o_ref[COL_LORA + n_lora:COL_ATKV, :] = jnp.zeros((COL_ATKV - COL_LORA - n_lora, cols), BF16)
    o_ref[COL_ATKV:IN_PACKED, :] = w_ref[o_at + AT_Q:o_gate, :].astype(BF16)


def _pack_w_in(w_t, *, tc):
    nl, n_in, d = w_t.shape
    return pl.pallas_call(
        _pack_w_in_kernel,
        out_shape=jax.ShapeDtypeStruct((nl, IN_PACKED, d), BF16),
        grid=(nl, d // tc),
        in_specs=[pl.BlockSpec((None, n_in, tc), lambda l, i: (l, 0, i))],
        out_specs=pl.BlockSpec((None, IN_PACKED, tc), lambda l, i: (l, 0, i)),
        compiler_params=_cparams(("parallel", "parallel")),
        name="pack_w_in",
    )(w_t)


def _pack_mu(mu):
    return jnp.concatenate([mu, jnp.zeros((G_PAD - G_LORA,), mu.dtype)])[None, :]


def _block_diag_up(up):
    z = jnp.zeros_like(up[0])
    return jnp.concatenate([jnp.concatenate([up[0], z], axis=1),
                            jnp.concatenate([z, up[1]], axis=1)], axis=0).astype(BF16)


def _rope_tables(s):
    rows = s // GRID_W
    row = jnp.repeat(jnp.arange(rows), GRID_W).astype(F32)
    col = jnp.tile(jnp.arange(GRID_W), rows).astype(F32)
    n_freq = AT_HEAD // 4
    inv = ROPE_THETA ** (-jnp.arange(n_freq, dtype=F32) / n_freq)
    ar = row[:, None] * inv
    ac = col[:, None] * inv
    cos = jnp.concatenate([jnp.cos(ar), jnp.cos(ar), jnp.cos(ac), jnp.cos(ac)], axis=1)
    sin = jnp.concatenate([-jnp.sin(ar), jnp.sin(ar), -jnp.sin(ac), jnp.sin(ac)], axis=1)
    return jnp.tile(cos, (1, 2)), jnp.tile(sin, (1, 2))


def kernel(x, mem, n_mix_pre, n_mix_post, n_x_pre, n_x_post, n_ffn_pre, n_ffn_post, n_mem, w_in, rw_mu_prev, rw_mu_next, rw_w0, rw_w_up, rw_a0, rw_a_up, rw_g_up, rw_v0, rw_v_down, rw_v_up, rw_k_k, rw_k_a, rw_r_k, rw_ln_w, rw_ln_b, w_rw_out, at_q_norm, at_k_norm, w_at_out, w_o, x_wq, x_wkv, x_wo, ffn_wg, ffn_wu, ffn_wd):
    b, s, d = x.shape
    assert b == 1 and d == D_MODEL and s % TILE_IN_PROJ[0] == 0 and s % GRID_W == 0
    xs = x[0]
    mems = mem[0]
    cos, sin = _rope_tables(s)
    row = lambda t: t[None, :]
    w_in_p = _pack_w_in(jnp.swapaxes(w_in, 1, 2), tc=TILE_PACK_COLS)
    wrw_b, wat_b, wo_b = w_rw_out.astype(BF16), w_at_out.astype(BF16), w_o.astype(BF16)
    xwq_b, xwkv_b, xwo_b = x_wq.astype(BF16), x_wkv.astype(BF16), x_wo.astype(BF16)
    wg_b, wu_b, wd_b = ffn_wg.astype(BF16), ffn_wu.astype(BF16), ffn_wd.astype(BF16)
    v_first = None
    for l in range(DEPTH):
        z = _norm_proj(xs, row(n_mix_pre[l]), w_in_p, l, tm=TILE_IN_PROJ[0], tn=TILE_IN_PROJ[1],
                       out_dtype=F32, w_is_nk=True)

        vmix = None
        if l > 0:
            vmix = (v_first, row(rw_v0[l - 1]), rw_v_down[l - 1].astype(BF16), rw_v_up[l - 1].astype(BF16))
        gup = jnp.concatenate([rw_g_up[l], jnp.zeros((G_PAD - G_LORA, RW_WIDTH), F32)], axis=0).astype(BF16)
        r, v, kk, kd, lw, ba, g, bon = _rwprep(
            z, _pack_mu(rw_mu_prev[l]), _pack_mu(rw_mu_next[l]),
            rw_w0[l].reshape(1, 2 * RW_WIDTH), _block_diag_up(rw_w_up[l]),
            rw_a0[l].reshape(1, 2 * RW_WIDTH), _block_diag_up(rw_a_up[l]), gup,
            row(rw_k_k[l]), row(rw_k_a[l]), rw_r_k[l].reshape(1, RW_WIDTH), vmix, tm=TILE_RWKV_PREP)
        if l == 0:
            v_first = v
        y_f, y_b = _rwrec(r, v, kk, kd, lw, ba, n_ck=REC_CHUNKS_PER_STEP)

        qn = jnp.tile(at_q_norm[l], 2)[None, :]
        kn = jnp.tile(at_k_norm[l], 2)[None, :]
        q_t, k_blk, v_t = _atprep(z, cos, sin, qn, kn, tm=TILE_ATTN_PREP, tq=FLASH_TQ)
        y_at = _flash(q_t, k_blk, v_t, n_tile=min(FLASH_TILES_PER_STEP, s // FLASH_TQ))

        xs = _mix_out(y_f, y_b, g, bon, y_at, z, xs, row(rw_ln_w[l]), row(rw_ln_b[l]),
                      wrw_b, wat_b, wo_b, row(n_mix_post[l]), l, tm=TILE_MIX_OUT)

        kv = _norm_proj(mems, row(n_mem[l]), xwkv_b, l, tm=mems.shape[0], tn=TILE_KV_PROJ_COLS, out_dtype=BF16)
        xs = _xattn(xs, kv, row(n_x_pre[l]), xwq_b, xwo_b, row(n_x_post[l]), l, tm=TILE_XATTN)

        xs = _ffn(xs, row(n_ffn_pre[l]), wg_b, wu_b, wd_b, row(n_ffn_post[l]), l,
                  tm=TILE_FFN[0], tf=TILE_FFN[1])
    return xs[None]
```

```python
import functools

import jax
import jax.numpy as jnp
from jax import lax
from jax.experimental import pallas as pl
from jax.experimental.pallas import tpu as pltpu

F32 = jnp.float32
BF16 = jnp.bfloat16

D_MODEL = 2048
DEPTH = 4
GRID_W = 64
EPS = 1e-6
RW_HEAD = 64
RW_WIDTH = 1024
W_LORA = 64
A_LORA = 64
G_LORA = 160
RW_GN_EPS = 64e-5
AT_HEADS = 16
AT_KV_HEADS = 4
AT_HEAD = 64
AT_Q = AT_HEADS * AT_HEAD
AT_KV = AT_KV_HEADS * AT_HEAD
ROPE_THETA = 10000.0
X_HEADS = 4
X_HEAD = 128
X_WIDTH = X_HEADS * X_HEAD

LANES = 128
CHUNK = 64
SUB = 16
LOG2E = 1.4426950408889634
EXP_NEG_HALF = 0.6065306597126334
V_AUG = AT_HEAD + 16
HALO = 16
G_PAD = 256

COL_G1 = 0
COL_G2 = 2048
COL_R = 4096
COL_K = 5120
COL_V = 6144
COL_Q = 7168
COL_LORA = 8192
COL_ATKV = 8704
IN_PACKED = 9216

VMEM_LIMIT = 56 * 1024 * 1024

TILE_IN_PROJ = (1024, 1536)
TILE_PACK_COLS = 256
TILE_RWKV_PREP = 256
REC_CHUNKS_PER_STEP = 2
TILE_ATTN_PREP = 512
FLASH_TQ = 256
FLASH_TILES_PER_STEP = 8
TILE_MIX_OUT = 256
TILE_KV_PROJ_COLS = 512
TILE_XATTN = 512
TILE_FFN = (1024, 512)


def _cparams(sem):
    return pltpu.CompilerParams(dimension_semantics=sem, vmem_limit_bytes=VMEM_LIMIT)


def _dot(a, b):
    return jnp.dot(a.astype(BF16), b.astype(BF16), preferred_element_type=F32)


def _dot_nt(a, b):
    return lax.dot_general(a.astype(BF16), b.astype(BF16), (((1,), (1,)), ((), ())),
                           preferred_element_type=F32)


def _split2(x):
    hi = x.astype(BF16)
    lo = (x - hi.astype(F32)).astype(BF16)
    return hi, lo


def _split3(x):
    hi = x.astype(BF16)
    r1 = x - hi.astype(F32)
    mid = r1.astype(BF16)
    lo = (r1 - mid.astype(F32)).astype(BF16)
    return hi, mid, lo


def _sigmoid(x):
    return 1.0 / (1.0 + jnp.exp(-x))


def _head_ones():
    r = lax.broadcasted_iota(jnp.int32, (LANES, LANES), 0) // RW_HEAD
    c = lax.broadcasted_iota(jnp.int32, (LANES, LANES), 1) // RW_HEAD
    return jnp.where(r == c, 1.0, 0.0).astype(BF16)


def _head_sum(x, ones):
    outs = []
    for s in range(x.shape[1] // LANES):
        hi, lo = _split2(x[:, s * LANES:(s + 1) * LANES])
        outs.append(jnp.dot(hi, ones, preferred_element_type=F32)
                    + jnp.dot(lo, ones, preferred_element_type=F32))
    return outs[0] if len(outs) == 1 else jnp.concatenate(outs, axis=1)


def _rms(x, gain):
    ms = jnp.mean(x * x, axis=-1, keepdims=True)
    return x * lax.rsqrt(ms + EPS) * gain


def _norm_proj_kernel(w_is_nk, x_ref, g_ref, w_ref, o_ref, h_ref):
    @pl.when(pl.program_id(1) == 0)
    def _():
        h_ref[...] = _rms(x_ref[...], g_ref[...]).astype(BF16)

    if w_is_nk:
        out = lax.dot_general(h_ref[...], w_ref[...], (((1,), (1,)), ((), ())), preferred_element_type=F32)
    else:
        out = jnp.dot(h_ref[...], w_ref[...], preferred_element_type=F32)
    o_ref[...] = out.astype(o_ref.dtype)


def _norm_proj(x, gain, w, layer, *, tm, tn, out_dtype, w_is_nk=False):
    m, d = x.shape
    if w_is_nk:
        n = w.shape[1]
        w_spec = pl.BlockSpec((None, tn, d), lambda i, j: (layer, j, 0))
    else:
        n = w.shape[2]
        w_spec = pl.BlockSpec((None, d, tn), lambda i, j: (layer, 0, j))
    return pl.pallas_call(
        functools.partial(_norm_proj_kernel, w_is_nk),
        out_shape=jax.ShapeDtypeStruct((m, n), out_dtype),
        grid=(m // tm, n // tn),
        in_specs=[pl.BlockSpec((tm, d), lambda i, j: (i, 0)),
                  pl.BlockSpec((1, d), lambda i, j: (0, 0)),
                  w_spec],
        out_specs=pl.BlockSpec((tm, tn), lambda i, j: (i, j)),
        scratch_shapes=[pltpu.VMEM((tm, d), BF16)],
        compiler_params=_cparams(("parallel", "arbitrary")),
        name="norm_proj",
    )(x, gain, w)


def _shift(z, prev_row, next_row, mu_p, mu_n):
    rows = z.shape[0]
    ridx = lax.broadcasted_iota(jnp.int32, z.shape, 0)
    zp = jnp.where(ridx == 0, prev_row, pltpu.roll(z, 1, 0))
    zn = jnp.where(ridx == rows - 1, next_row, pltpu.roll(z, rows - 1, 0))
    return (1.0 - mu_p - mu_n) * z + mu_p * zp + mu_n * zn


def _rwprep_kernel(has_vmix, *refs):
    (zr, zk, zv, zl, pr, pk, pv, pL, nr, nk, nv, nL, mup, mun, w0, wup, a0, aup, gup,
     kk_w, ka_w, rk_w) = refs[:22]
    pos = 22
    if has_vmix:
        vfirst, v0, vdown, vup = refs[pos:pos + 4]
        pos += 4
    r_o, v_o, kk_o, kd_o, lw_o, ba_o, g_o, bon_o = refs[pos:pos + 8]

    i = pl.program_id(0)
    last = pl.num_programs(0) - 1
    keep_p = jnp.where(i == 0, 0.0, 1.0)
    keep_n = jnp.where(i == last, 0.0, 1.0)

    def shifted(z_ref, p_ref, n_ref, lo, hi):
        p_row = p_ref[HALO - 1:HALO, :] * keep_p
        n_row = n_ref[0:1, :] * keep_n
        return _shift(z_ref[...], p_row, n_row, mup[:, lo:hi], mun[:, lo:hi])

    r = shifted(zr, pr, nr, 0, 1024)
    k = shifted(zk, pk, nk, 1024, 2048)
    v = shifted(zv, pv, nv, 2048, 3072)
    lora = shifted(zl, pL, nL, 3072, 3584)

    if has_vmix:
        mix = _sigmoid(v0[...] + _dot(_dot(v, vdown[...]), vup[...]))
        v = v + (vfirst[...] - v) * mix

    u = w0[...] + _dot(jnp.tanh(lora[:, 0:128]), wup[...])
    lw = -EXP_NEG_HALF * _sigmoid(u)
    a = _sigmoid(a0[...] + _dot(lora[:, 128:256], aup[...]))
    g = _dot(_sigmoid(lora[:, 256:512]), gup[...])

    ones = _head_ones()
    kk = k * kk_w[...]
    norm = jnp.sqrt(_head_sum(kk * kk, ones))
    kk = kk / jnp.maximum(norm, 1e-12)
    ka = ka_w[...]
    a_f = a[:, :RW_WIDTH]
    a_b = a[:, RW_WIDTH:]
    kd_f = k * (1.0 + (a_f - 1.0) * ka)
    kd_b = k * (1.0 + (a_b - 1.0) * ka)
    bonus = _head_sum(r * rk_w[...] * (kd_f + kd_b), ones) * v

    r_o[...] = r
    v_o[...] = v
    kk_o[...] = kk
    kd_o[0] = kd_f
    kd_o[1] = kd_b
    lw_o[0] = lw[:, :RW_WIDTH]
    lw_o[1] = lw[:, RW_WIDTH:]
    ba_o[0] = kk * a_f
    ba_o[1] = kk * a_b
    g_o[...] = g
    bon_o[...] = bonus


def _rwprep(z, mup, mun, w0, wup, a0, aup, gup, kk_w, ka_w, rk_w, vmix, *, tm):
    s = z.shape[0]
    nbh = s // HALO
    tb = tm // HALO
    has_vmix = vmix is not None

    def main(width, cb):
        return pl.BlockSpec((tm, width), lambda i: (i, cb))

    def prev(width, cb):
        return pl.BlockSpec((HALO, width), lambda i: (jnp.maximum(i * tb - 1, 0), cb))

    def nxt(width, cb):
        return pl.BlockSpec((HALO, width), lambda i: (jnp.minimum((i + 1) * tb, nbh - 1), cb))

    def full(arr):
        nd = arr.ndim
        return pl.BlockSpec(arr.shape, lambda i: (0,) * nd)

    cols = [(1024, COL_R // 1024), (1024, COL_K // 1024), (1024, COL_V // 1024), (512, COL_LORA // 512)]
    in_specs = ([main(w, c) for w, c in cols] + [prev(w, c) for w, c in cols]
                + [nxt(w, c) for w, c in cols])
    args = [z] * 12
    consts = [mup, mun, w0, wup, a0, aup, gup, kk_w, ka_w, rk_w]
    in_specs += [full(c) for c in consts]
    args += consts
    if has_vmix:
        vfirst, v0, vdown, vup = vmix
        in_specs += [pl.BlockSpec((tm, RW_WIDTH), lambda i: (i, 0)), full(v0), full(vdown), full(vup)]
        args += [vfirst, v0, vdown, vup]

    one = jax.ShapeDtypeStruct((s, RW_WIDTH), F32)
    two = jax.ShapeDtypeStruct((2, s, RW_WIDTH), F32)
    spec1 = pl.BlockSpec((tm, RW_WIDTH), lambda i: (i, 0))
    spec2 = pl.BlockSpec((2, tm, RW_WIDTH), lambda i: (0, i, 0))
    return pl.pallas_call(
        functools.partial(_rwprep_kernel, has_vmix),
        out_shape=[one, one, one, two, two, two, one, one],
        grid=(s // tm,),
        in_specs=in_specs,
        out_specs=[spec1, spec1, spec1, spec2, spec2, spec2, spec1, spec1],
        compiler_params=_cparams(("parallel",)),
        name="rwkv_prep",
    )(*args)


def _rwrec_kernel(rf_ref, vf_ref, kkf_ref, kdf_ref, lwf_ref, baf_ref,
                  rb_ref, vb_ref, kkb_ref, kdb_ref, lwb_ref, bab_ref, yf_ref, yb_ref, s_ref):
    T = CHUNK
    T2 = 2 * T
    n_slab = RW_WIDTH // LANES
    n_ck = rf_ref.shape[0] // T

    @pl.when(pl.program_id(0) == 0)
    def _():
        s_ref[...] = jnp.zeros_like(s_ref)

    row = lax.broadcasted_iota(jnp.int32, (T, T), 0)
    col = lax.broadcasted_iota(jnp.int32, (T, T), 1)
    R2 = lax.broadcasted_iota(jnp.int32, (T2, T2), 0)
    C2 = lax.broadcasted_iota(jnp.int32, (T2, T2), 1)
    same = (R2 // T) == (C2 // T)
    tdiff = (R2 % T) - (C2 % T)
    mask_d = same & (((R2 % T) // SUB) == ((C2 % T) // SUB))
    eye = R2 == C2
    head0 = lax.broadcasted_iota(jnp.int32, (T, LANES), 1) < RW_HEAD

    def stack_heads(x):
        return jnp.concatenate([jnp.where(head0, x, 0.0), jnp.where(head0, 0.0, x)], axis=0)

    def stack_dup(x):
        return jnp.concatenate([x, x], axis=0)

    cols = [slice(sl * LANES, (sl + 1) * LANES) for sl in range(n_slab)]
    ar2, bk2, v2, bke2, mask_a, mask_y, e_tot, u_rows = [], [], [], [], [], [], [], []
    dirs = ((1, (rf_ref, vf_ref, kkf_ref, kdf_ref, lwf_ref, baf_ref)),
            (-1, (rb_ref, vb_ref, kkb_ref, kdb_ref, lwb_ref, bab_ref)))
    for ck, (sgn, (r_ref, v_ref, kk_ref, kd_ref, lw_ref, ba_ref)) in (
            (ck, d) for ck in range(n_ck) for d in dirs):
        c_in_blk = ck if sgn > 0 else n_ck - 1 - ck
        rows = slice(c_in_blk * T, (c_in_blk + 1) * T)
        tri = jnp.where((row - col) * sgn >= 0, 1.0, 0.0).astype(BF16)
        lw = lw_ref[rows, :]
        hi, mid, lo = _split3(lw)
        cin = (jnp.dot(tri, hi, preferred_element_type=F32)
               + jnp.dot(tri, mid, preferred_element_type=F32)
               + jnp.dot(tri, lo, preferred_element_type=F32))
        ctot = jnp.sum(lw, axis=0, keepdims=True)
        e_in = jnp.exp(cin)
        e_ex = jnp.exp(cin - lw)
        e_neg = jnp.exp(-cin)
        e_end = jnp.exp(ctot - cin)
        etot = jnp.exp(ctot)
        kd = kd_ref[rows, :]
        ba = ba_ref[rows, :]
        r_t = r_ref[rows, :] * e_in
        a_t = -kk_ref[rows, :] * e_ex
        b_t = ba * e_neg
        k_t = kd * e_neg
        b_e = ba * e_end
        k_e = kd * e_end
        v = v_ref[rows, :]
        m_a = same & (tdiff * sgn > 0)
        m_y = same & (tdiff * sgn >= 0)
        for cs in cols:
            ar2.append(jnp.concatenate([stack_heads(a_t[:, cs]), stack_heads(r_t[:, cs])],
                                       axis=0).astype(BF16))
            bk2.append(jnp.concatenate([stack_dup(b_t[:, cs]), stack_dup(k_t[:, cs])],
                                       axis=0).astype(BF16))
            v2.append(stack_heads(v[:, cs]).astype(BF16))
            bke2.append(jnp.concatenate([stack_heads(b_e[:, cs]), stack_heads(k_e[:, cs])],
                                        axis=0).astype(BF16))
            mask_a.append(m_a)
            mask_y.append(m_y)
            e_tot.append(etot[:, cs])
            u_rows.append(rows)

    n_unit = 2 * n_slab
    slabs = range(n_ck * n_unit)
    sc = [_dot_nt(ar2[s], bk2[s]) for s in slabs]
    n_ab = [jnp.where(mask_a[s], sc[s][:T2, :T2], 0.0) for s in slabs]
    n_ak = [jnp.where(mask_a[s], sc[s][:T2, T2:], 0.0) for s in slabs]
    m_rbk = [jnp.concatenate([jnp.where(mask_y[s], sc[s][T2:, :T2], 0.0),
                              jnp.where(mask_y[s], sc[s][T2:, T2:], 0.0)], axis=1).astype(BF16)
             for s in slabs]

    def dot2(lhs, a, b):
        t = _dot(lhs, jnp.concatenate([a, b], axis=1))
        return t[:, :T2], t[:, T2:]

    p = [jnp.where(mask_d, n_ab[s], 0.0) for s in slabs]
    q = [jnp.where(eye, 1.0, p[s]) for s in slabs]
    p = [_dot(p[s], p[s]) for s in slabs]
    for _ in range(2):
        pq = [dot2(p[s], p[s], q[s]) for s in slabs]
        p = [pq[s][0] for s in slabs]
        q = [q[s] + pq[s][1] for s in slabs]
    q = [q[s] + _dot(p[s], q[s]) for s in slabs]
    akv = [_dot(n_ak[s], v2[s]) for s in slabs]

    for ck in range(n_ck):
        us = range(ck * n_unit, (ck + 1) * n_unit)
        sidx = {u: ((u % n_unit) // n_slab, u % n_slab) for u in us}
        st = {u: s_ref[sidx[u]] for u in us}
        ars = {u: _dot_nt(ar2[u], st[u]) for u in us}
        mx = {u: dot2(q[u], jnp.where(mask_d, 0.0, n_ab[u]), ars[u][:T2] + akv[u]) for u in us}
        m = {u: mx[u][0] for u in us}
        x = {u: mx[u][1] for u in us}
        mx = {u: dot2(m[u], m[u], x[u]) for u in us}
        x = {u: x[u] + mx[u][1] for u in us}
        x = {u: x[u] + _dot(mx[u][0], x[u]) for u in us}
        for u in us:
            y2 = ars[u][T2:] + _dot(m_rbk[u], jnp.concatenate([x[u].astype(BF16), v2[u]], axis=0))
            y_ref = yf_ref if sidx[u][0] == 0 else yb_ref
            y_ref[u_rows[u], cols[sidx[u][1]]] = y2[:T] + y2[T:]
        for u in us:
            upd = _dot(jnp.concatenate([x[u].T, v2[u].astype(F32).T], axis=1), bke2[u])
            s_ref[sidx[u]] = st[u] * e_tot[u] + upd


def _rwrec(r, v, kk, kd, lw, ba, *, n_ck):
    s = r.shape[0]
    rows = n_ck * CHUNK
    nc = s // rows
    fwd1 = pl.BlockSpec((rows, RW_WIDTH), lambda c: (c, 0))
    bwd1 = pl.BlockSpec((rows, RW_WIDTH), lambda c: (nc - 1 - c, 0))
    fwd2 = pl.BlockSpec((None, rows, RW_WIDTH), lambda c: (0, c, 0))
    bwd2 = pl.BlockSpec((None, rows, RW_WIDTH), lambda c: (1, nc - 1 - c, 0))
    out = jax.ShapeDtypeStruct((s, RW_WIDTH), F32)
    return pl.pallas_call(
        _rwrec_kernel,
        out_shape=[out, out],
        grid=(nc,),
        in_specs=[fwd1, fwd1, fwd1, fwd2, fwd2, fwd2, bwd1, bwd1, bwd1, bwd2, bwd2, bwd2],
        out_specs=[fwd1, bwd1],
        scratch_shapes=[pltpu.VMEM((2, RW_WIDTH // LANES, LANES, LANES), F32)],
        compiler_params=_cparams(("arbitrary",)),
        name="rwkv_recurrence",
    )(r, v, kk, kd, lw, ba, r, v, kk, kd, lw, ba)


def _atprep_kernel(zq, zkv, cos_ref, sin_ref, qn, kn, qt_o, k_o, vt_o):
    ones = _head_ones()
    cos = cos_ref[...]
    sin = sin_ref[...]
    lane = lax.broadcasted_iota(jnp.int32, cos.shape, 1)
    even = ((lane // 16) % 2) == 0

    def norm_rope(t, gain):
        ms = _head_sum(t * t, ones) * (1.0 / AT_HEAD)
        tn = t * lax.rsqrt(ms + EPS) * gain
        swapped = jnp.where(even, pltpu.roll(tn, LANES - 16, 1), pltpu.roll(tn, 16, 1))
        return tn * cos + swapped * sin

    q_scale = (AT_HEAD ** -0.5) * LOG2E
    tm = zq.shape[0]
    ones_rows = jnp.where(lax.broadcasted_iota(jnp.int32, (V_AUG - AT_HEAD, tm), 0) == 0, 1.0, 0.0).astype(BF16)
    for sl in range(AT_Q // LANES):
        out = norm_rope(zq[:, sl * LANES:(sl + 1) * LANES], qn[...]) * q_scale
        out_t = out.T.astype(BF16)
        tq = qt_o.shape[3]
        for qi in range(qt_o.shape[1]):
            qt_o[2 * sl, qi] = out_t[:AT_HEAD, qi * tq:(qi + 1) * tq]
            qt_o[2 * sl + 1, qi] = out_t[AT_HEAD:, qi * tq:(qi + 1) * tq]
    for sl in range(AT_KV // LANES):
        out = norm_rope(zkv[:, sl * LANES:(sl + 1) * LANES], kn[...])
        k_o[2 * sl] = out[:, :AT_HEAD].astype(BF16)
        k_o[2 * sl + 1] = out[:, AT_HEAD:].astype(BF16)
        vt = zkv[:, AT_KV + sl * LANES:AT_KV + (sl + 1) * LANES].T.astype(BF16)
        vt_o[2 * sl] = jnp.concatenate([vt[:AT_HEAD], ones_rows], axis=0)
        vt_o[2 * sl + 1] = jnp.concatenate([vt[AT_HEAD:], ones_rows], axis=0)


def _atprep(z, cos, sin, qn, kn, *, tm, tq):
    s = z.shape[0]
    nb = s // tm
    return pl.pallas_call(
        _atprep_kernel,
        out_shape=[jax.ShapeDtypeStruct((AT_HEADS, s // tq, AT_HEAD, tq), BF16),
                   jax.ShapeDtypeStruct((AT_KV_HEADS, nb, tm, AT_HEAD), BF16),
                   jax.ShapeDtypeStruct((AT_KV_HEADS, nb, V_AUG, tm), BF16)],
        grid=(nb,),
        in_specs=[pl.BlockSpec((tm, AT_Q), lambda i: (i, COL_Q // AT_Q)),
                  pl.BlockSpec((tm, 2 * AT_KV), lambda i: (i, COL_ATKV // (2 * AT_KV))),
                  pl.BlockSpec((tm, LANES), lambda i: (i, 0)),
                  pl.BlockSpec((tm, LANES), lambda i: (i, 0)),
                  pl.BlockSpec((1, LANES), lambda i: (0, 0)),
                  pl.BlockSpec((1, LANES), lambda i: (0, 0))],
        out_specs=[pl.BlockSpec((AT_HEADS, tm // tq, AT_HEAD, tq), lambda i: (0, i, 0, 0)),
                   pl.BlockSpec((AT_KV_HEADS, None, tm, AT_HEAD), lambda i: (0, i, 0, 0)),
                   pl.BlockSpec((AT_KV_HEADS, None, V_AUG, tm), lambda i: (0, i, 0, 0))],
        compiler_params=_cparams(("parallel",)),
        name="attn_prep",
    )(z, z, cos, sin, qn, kn)


def _flash_kernel(qt_ref, k_ref, vt_ref, o_ref, m_ref, acc_ref, sa_ref, sb_ref):
    grp, n_tile, _, tq = qt_ref.shape
    nkb = k_ref.shape[0]
    m_ref[...] = jnp.full_like(m_ref, -jnp.inf)
    acc_ref[...] = jnp.zeros_like(acc_ref)

    def step(t_cur, kb_cur, cur_ref, t_next, kb_next, next_ref):
        k_next = k_ref[kb_next]
        vt = vt_ref[kb_cur]
        for g in range(grp):
            next_ref[g] = jnp.dot(k_next, qt_ref[g, t_next], preferred_element_type=F32)
            st = cur_ref[g]
            m_prev = m_ref[t_cur, g]
            m_new = jnp.maximum(m_prev, jnp.max(st, axis=0, keepdims=True))
            alpha = jnp.exp2(m_prev - m_new)
            pt = jnp.exp2(st - m_new).astype(BF16)
            m_ref[t_cur, g] = m_new
            acc_ref[t_cur, g] = alpha * acc_ref[t_cur, g] + jnp.dot(vt, pt, preferred_element_type=F32)

    k0 = k_ref[0]
    for g in range(grp):
        sa_ref[g] = jnp.dot(k0, qt_ref[g, 0], preferred_element_type=F32)

    unroll = max(u for u in (8, 4, 2) if nkb % u == 0)
    trips_per_tile = nkb // unroll

    def body(j, carry):
        t = j // trips_per_tile
        kb = unroll * (j % trips_per_tile)
        for u in range(0, unroll - 2, 2):
            step(t, kb + u, sa_ref, t, kb + u + 1, sb_ref)
            step(t, kb + u + 1, sb_ref, t, kb + u + 2, sa_ref)
        u = unroll - 2
        step(t, kb + u, sa_ref, t, kb + u + 1, sb_ref)
        wrap = (kb + unroll) // nkb
        step(t, kb + u + 1, sb_ref, jnp.minimum(t + wrap, n_tile - 1), (kb + unroll) * (1 - wrap), sa_ref)
        return carry

    lax.fori_loop(0, n_tile * trips_per_tile, body, 0)
    for t in range(n_tile):
        for pair in range(grp // 2):
            a0 = acc_ref[t, 2 * pair]
            a1 = acc_ref[t, 2 * pair + 1]
            o2 = jnp.concatenate([a0[:AT_HEAD] / a0[AT_HEAD:AT_HEAD + 1],
                                  a1[:AT_HEAD] / a1[AT_HEAD:AT_HEAD + 1]], axis=0)
            o_ref[t * tq:(t + 1) * tq, pair * LANES:(pair + 1) * LANES] = o2.T.astype(o_ref.dtype)


def _flash(qt, k, vt, *, n_tile):
    _, nqt, _, tq = qt.shape
    _, nkb, tk, _ = k.shape
    grp = AT_HEADS // AT_KV_HEADS
    return pl.pallas_call(
        _flash_kernel,
        out_shape=jax.ShapeDtypeStruct((nqt * tq, AT_Q), BF16),
        grid=(AT_KV_HEADS, nqt // n_tile),
        in_specs=[pl.BlockSpec((grp, n_tile, AT_HEAD, tq), lambda h, i: (h, i, 0, 0)),
                  pl.BlockSpec((None, nkb, tk, AT_HEAD), lambda h, i: (h, 0, 0, 0)),
                  pl.BlockSpec((None, nkb, V_AUG, tk), lambda h, i: (h, 0, 0, 0))],
        out_specs=pl.BlockSpec((n_tile * tq, grp * AT_HEAD), lambda h, i: (i, h)),
        scratch_shapes=[pltpu.VMEM((n_tile, grp, 1, tq), F32), pltpu.VMEM((n_tile, grp, V_AUG, tq), F32),
                        pltpu.VMEM((grp, tk, tq), F32), pltpu.VMEM((grp, tk, tq), F32)],
        compiler_params=_cparams(("parallel", "parallel")),
        name="flash_attention",
    )(qt, k, vt)


def _mix_out_kernel(yf_ref, yb_ref, g_ref, bon_ref, yat_ref, zg1_ref, zg2_ref, x_ref, lnw, lnb,
                    wrw, wat, wo, npost, o_ref):
    ones = _head_ones()
    y = yf_ref[...] + yb_ref[...]
    inv = 1.0 / RW_HEAD
    mu = _head_sum(y, ones) * inv
    dlt = y - mu
    var = _head_sum(dlt * dlt, ones) * inv
    yn = dlt * lax.rsqrt(var + RW_GN_EPS) * lnw[...] + lnb[...]
    y_rw = (yn + bon_ref[...]) * g_ref[...]
    pa = _dot(y_rw, wrw[...])
    pb = jnp.dot(yat_ref[...], wat[...], preferred_element_type=F32)
    merged = _sigmoid(zg1_ref[...]) * pa + _sigmoid(zg2_ref[...]) * pb
    out = _dot(merged, wo[...])
    o_ref[...] = x_ref[...] + _rms(out, npost[...])


def _layer_block(arr, layer, **kw):
    return pl.BlockSpec((None,) + arr.shape[1:], lambda *_: (layer, 0, 0), **kw)


def _mix_out(yf, yb, g, bon, yat, z, x, lnw, lnb, wrw, wat, wo, npost, layer, *, tm):
    s = x.shape[0]

    def const(arr):
        if arr.ndim == 3:
            return _layer_block(arr, layer, pipeline_mode=pl.Buffered(1))
        return pl.BlockSpec(arr.shape, lambda i: (0, 0), pipeline_mode=pl.Buffered(1))

    row1k = pl.BlockSpec((tm, RW_WIDTH), lambda i: (i, 0))
    return pl.pallas_call(
        _mix_out_kernel,
        out_shape=jax.ShapeDtypeStruct((s, D_MODEL), F32),
        grid=(s // tm,),
        in_specs=[row1k, row1k, row1k, row1k, row1k,
                  pl.BlockSpec((tm, D_MODEL), lambda i: (i, COL_G1 // D_MODEL)),
                  pl.BlockSpec((tm, D_MODEL), lambda i: (i, COL_G2 // D_MODEL)),
                  pl.BlockSpec((tm, D_MODEL), lambda i: (i, 0)),
                  const(lnw), const(lnb), const(wrw), const(wat), const(wo), const(npost)],
        out_specs=pl.BlockSpec((tm, D_MODEL), lambda i: (i, 0)),
        compiler_params=_cparams(("parallel",)),
        name="mix_out",
    )(yf, yb, g, bon, yat, z, z, x, lnw, lnb, wrw, wat, wo, npost)


def _xattn_kernel(x_ref, kv_ref, npre, wq, wo, npost, o_ref):
    x = x_ref[...]
    h = _rms(x, npre[...])
    q = _dot(h, wq[...]) * (X_HEAD ** -0.5)
    outs = []
    for hd in range(X_HEADS):
        kh = kv_ref[:, hd * X_HEAD:(hd + 1) * X_HEAD]
        vh = kv_ref[:, X_WIDTH + hd * X_HEAD:X_WIDTH + (hd + 1) * X_HEAD]
        sc = _dot_nt(q[:, hd * X_HEAD:(hd + 1) * X_HEAD], kh)
        sc = sc - jnp.max(sc, axis=-1, keepdims=True)
        p = jnp.exp(sc)
        p = p / jnp.sum(p, axis=-1, keepdims=True)
        outs.append(_dot(p, vh))
    o = jnp.concatenate(outs, axis=1)
    c = _dot(o, wo[...])
    o_ref[...] = x + _rms(c, npost[...])


def _xattn(x, kv, npre, wq, wo, npost, layer, *, tm):
    s = x.shape[0]

    def const(arr):
        if arr.ndim == 3:
            return _layer_block(arr, layer)
        return pl.BlockSpec(arr.shape, lambda i: (0, 0))

    return pl.pallas_call(
        _xattn_kernel,
        out_shape=jax.ShapeDtypeStruct((s, D_MODEL), F32),
        grid=(s // tm,),
        in_specs=[pl.BlockSpec((tm, D_MODEL), lambda i: (i, 0)), const(kv), const(npre),
                  const(wq), const(wo), const(npost)],
        out_specs=pl.BlockSpec((tm, D_MODEL), lambda i: (i, 0)),
        compiler_params=_cparams(("parallel",)),
        name="cross_attention",
    )(x, kv, npre, wq, wo, npost)


def _ffn_kernel(x_ref, npre, wg, wu, wd, npost, o_ref, h_ref):
    j = pl.program_id(1)

    @pl.when(j == 0)
    def _():
        h_ref[...] = _rms(x_ref[...], npre[...]).astype(BF16)
        o_ref[...] = jnp.zeros_like(o_ref)

    h = h_ref[...]
    gt = jnp.dot(h, wg[...], preferred_element_type=F32)
    ut = jnp.dot(h, wu[...], preferred_element_type=F32)
    act = gt * _sigmoid(gt) * ut
    o_ref[...] += jnp.dot(act.astype(BF16), wd[...], preferred_element_type=F32)

    @pl.when(j == pl.num_programs(1) - 1)
    def _():
        o_ref[...] = x_ref[...] + _rms(o_ref[...], npost[...])


def _ffn(x, npre, wg, wu, wd, npost, layer, *, tm, tf):
    s = x.shape[0]
    f = wg.shape[2]
    return pl.pallas_call(
        _ffn_kernel,
        out_shape=jax.ShapeDtypeStruct((s, D_MODEL), F32),
        grid=(s // tm, f // tf),
        in_specs=[pl.BlockSpec((tm, D_MODEL), lambda i, j: (i, 0), pipeline_mode=pl.Buffered(1)),
                  pl.BlockSpec((1, D_MODEL), lambda i, j: (0, 0)),
                  pl.BlockSpec((None, D_MODEL, tf), lambda i, j: (layer, 0, j)),
                  pl.BlockSpec((None, D_MODEL, tf), lambda i, j: (layer, 0, j)),
                  pl.BlockSpec((None, tf, D_MODEL), lambda i, j: (layer, j, 0)),
                  pl.BlockSpec((1, D_MODEL), lambda i, j: (0, 0))],
        out_specs=pl.BlockSpec((tm, D_MODEL), lambda i, j: (i, 0)),
        scratch_shapes=[pltpu.VMEM((tm, D_MODEL), BF16)],
        compiler_params=_cparams(("parallel", "arbitrary")),
        name="swiglu",
    )(x, npre, wg, wu, wd, npost)


def _pack_w_in_kernel(w_ref, o_ref):
    o_lora = 3 * RW_WIDTH
    o_at = o_lora + 2 * W_LORA + 2 * A_LORA + G_LORA
    o_gate = o_at + AT_Q + 2 * AT_KV
    n_lora = o_at - o_lora
    cols = o_ref.shape[1]
    o_ref[COL_G1:COL_G1 + 2 * D_MODEL, :] = w_ref[o_gate:o_gate + 2 * D_MODEL, :].astype(BF16)
    o_ref[COL_R:COL_R + 3 * RW_WIDTH, :] = w_ref[0:3 * RW_WIDTH, :].astype(BF16)
    o_ref[COL_Q:COL_Q + AT_Q, :] = w_ref[o_at:o_at + AT_Q, :].astype(BF16)
    o_ref[COL_LORA:COL_LORA + n_lora, :] = w_ref[o_lora:o_at, :].astype(BF16)
    o_ref[COL_LORA + n_lora:COL_ATKV, :] = jnp.zeros((COL_ATKV - COL_LORA - n_lora, cols), BF16)
    o_ref[COL_ATKV:IN_PACKED, :] = w_ref[o_at + AT_Q:o_gate, :].astype(BF16)


def _pack_w_in(w_t, *, tc):
    nl, n_in, d = w_t.shape
    return pl.pallas_call(
        _pack_w_in_kernel,
        out_shape=jax.ShapeDtypeStruct((nl, IN_PACKED, d), BF16),
        grid=(nl, d // tc),
        in_specs=[pl.BlockSpec((None, n_in, tc), lambda l, i: (l, 0, i))],
        out_specs=pl.BlockSpec((None, IN_PACKED, tc), lambda l, i: (l, 0, i)),
        compiler_params=_cparams(("parallel", "parallel")),
        name="pack_w_in",
    )(w_t)


def _pack_mu(mu):
    return jnp.concatenate([mu, jnp.zeros((G_PAD - G_LORA,), mu.dtype)])[None, :]


def _block_diag_up(up):
    z = jnp.zeros_like(up[0])
    return jnp.concatenate([jnp.concatenate([up[0], z], axis=1),
                            jnp.concatenate([z, up[1]], axis=1)], axis=0).astype(BF16)


def _rope_tables(s):
    rows = s // GRID_W
    row = jnp.repeat(jnp.arange(rows), GRID_W).astype(F32)
    col = jnp.tile(jnp.arange(GRID_W), rows).astype(F32)
    n_freq = AT_HEAD // 4
    inv = ROPE_THETA ** (-jnp.arange(n_freq, dtype=F32) / n_freq)
    ar = row[:, None] * inv
    ac = col[:, None] * inv
    cos = jnp.concatenate([jnp.cos(ar), jnp.cos(ar), jnp.cos(ac), jnp.cos(ac)], axis=1)
    sin = jnp.concatenate([-jnp.sin(ar), jnp.sin(ar), -jnp.sin(ac), jnp.sin(ac)], axis=1)
    return jnp.tile(cos, (1, 2)), jnp.tile(sin, (1, 2))


def kernel(x, mem, n_mix_pre, n_mix_post, n_x_pre, n_x_post, n_ffn_pre, n_ffn_post, n_mem, w_in, rw_mu_prev, rw_mu_next, rw_w0, rw_w_up, rw_a0, rw_a_up, rw_g_up, rw_v0, rw_v_down, rw_v_up, rw_k_k, rw_k_a, rw_r_k, rw_ln_w, rw_ln_b, w_rw_out, at_q_norm, at_k_norm, w_at_out, w_o, x_wq, x_wkv, x_wo, ffn_wg, ffn_wu, ffn_wd):
    b, s, d = x.shape
    assert b == 1 and d == D_MODEL and s % TILE_IN_PROJ[0] == 0 and s % GRID_W == 0
    xs = x[0]
    mems = mem[0]
    cos, sin = _rope_tables(s)
    row = lambda t: t[None, :]
    w_in_p = _pack_w_in(jnp.swapaxes(w_in, 1, 2), tc=TILE_PACK_COLS)
    wrw_b, wat_b, wo_b = w_rw_out.astype(BF16), w_at_out.astype(BF16), w_o.astype(BF16)
    xwq_b, xwkv_b, xwo_b = x_wq.astype(BF16), x_wkv.astype(BF16), x_wo.astype(BF16)
    wg_b, wu_b, wd_b = ffn_wg.astype(BF16), ffn_wu.astype(BF16), ffn_wd.astype(BF16)
    v_first = None
    for l in range(DEPTH):
        z = _norm_proj(xs, row(n_mix_pre[l]), w_in_p, l, tm=TILE_IN_PROJ[0], tn=TILE_IN_PROJ[1],
                       out_dtype=F32, w_is_nk=True)

        vmix = None
        if l > 0:
            vmix = (v_first, row(rw_v0[l - 1]), rw_v_down[l - 1].astype(BF16), rw_v_up[l - 1].astype(BF16))
        gup = jnp.concatenate([rw_g_up[l], jnp.zeros((G_PAD - G_LORA, RW_WIDTH), F32)], axis=0).astype(BF16)
        r, v, kk, kd, lw, ba, g, bon = _rwprep(
            z, _pack_mu(rw_mu_prev[l]), _pack_mu(rw_mu_next[l]),
            rw_w0[l].reshape(1, 2 * RW_WIDTH), _block_diag_up(rw_w_up[l]),
            rw_a0[l].reshape(1, 2 * RW_WIDTH), _block_diag_up(rw_a_up[l]), gup,
            row(rw_k_k[l]), row(rw_k_a[l]), rw_r_k[l].reshape(1, RW_WIDTH), vmix, tm=TILE_RWKV_PREP)
        if l == 0:
            v_first = v
        y_f, y_b = _rwrec(r, v, kk, kd, lw, ba, n_ck=REC_CHUNKS_PER_STEP)

        qn = jnp.tile(at_q_norm[l], 2)[None, :]
        kn = jnp.tile(at_k_norm[l], 2)[None, :]
        q_t, k_blk, v_t = _atprep(z, cos, sin, qn, kn, tm=TILE_ATTN_PREP, tq=FLASH_TQ)
        y_at = _flash(q_t, k_blk, v_t, n_tile=min(FLASH_TILES_PER_STEP, s // FLASH_TQ))

        xs = _mix_out(y_f, y_b, g, bon, y_at, z, xs, row(rw_ln_w[l]), row(rw_ln_b[l]),
                      wrw_b, wat_b, wo_b, row(n_mix_post[l]), l, tm=TILE_MIX_OUT)

        kv = _norm_proj(mems, row(n_mem[l]), xwkv_b, l, tm=mems.shape[0], tn=TILE_KV_PROJ_COLS, out_dtype=BF16)
        xs = _xattn(xs, kv, row(n_x_pre[l]), xwq_b, xwo_b, row(n_x_post[l]), l, tm=TILE_XATTN)

        xs = _ffn(xs, row(n_ffn_pre[l]), wg_b, wu_b, wd_b, row(n_ffn_post[l]), l,
                  tm=TILE_FFN[0], tf=TILE_FFN[1])
    return xs[None]
```
